```python
import math
import jax, jax.numpy as jnp
from jax import lax
import numpy as np

D_MODEL = 2048
BATCH = 4
SEQ = 4096
DEPTH = 2

D_POOL = D_MODEL // 2
N_POOL_GROUPS = 4
POOL_GROUP_DIM = D_POOL // N_POOL_GROUPS
POOL_WINDOWS = (2, 4, 8, 16)
D_CONV = D_MODEL // 2
CONV_WIDTH = 31
HEAD_DIM = 64
N_HEADS = D_MODEL // HEAD_DIM
N_KV_HEADS = N_HEADS // 8
WINDOW = 128
ATTN_BLOCK = WINDOW
ROPE_THETA = 10000.0
D_FF = ((8 * D_MODEL // 3 + 255) // 256) * 256
N_EXPERTS = 8
TOP_K = 2
D_FF_EXPERT = 7 * D_MODEL // 2
MOE_BLOCK = 512
NORM_EPS = 1e-6
LN_EPS = 1e-5

kernel_name = "hybrid_pool_conv_swa_moe_adaln"


def rms_norm(x):
    xf = x.astype(jnp.float32)
    return xf * lax.rsqrt(jnp.mean(xf * xf, axis=-1, keepdims=True) + NORM_EPS)


def adaln_params(c, w_mod, b_mod):
    mod = jax.nn.silu(c) @ w_mod + b_mod
    return jnp.split(mod, 6, axis=-1)


def modulate(x, shift, scale):
    y = rms_norm(x) * (1.0 + scale[:, None, :]) + shift[:, None, :]
    return y.astype(x.dtype)


def swiglu(h, w_gate, w_up, w_down):
    return (jax.nn.silu(h @ w_gate) * (h @ w_up)) @ w_down


def pool_mixer(u, pool_w, pool_scale):
    B, L, _ = u.shape
    ug = u.astype(jnp.float32).reshape(B, L, N_POOL_GROUPS, POOL_GROUP_DIM)
    pos1 = jnp.arange(1, L + 1)
    outs = []
    for g, w in enumerate(POOL_WINDOWS):
        ch = ug[:, :, g]
        cs = jnp.cumsum(ch, axis=1)
        lag = jnp.pad(cs, ((0, 0), (w, 0), (0, 0)))[:, :L]
        cnt = jnp.minimum(pos1, w).astype(jnp.float32)[None, :, None]
        outs.append((cs - lag) / cnt - ch)
    pooled = jnp.stack(outs, axis=2)
    mixed = jnp.einsum('blgc,gcd->blgd', pooled, pool_w.astype(jnp.float32))
    return (mixed.reshape(B, L, D_POOL) * pool_scale).astype(u.dtype)


def conv_module(val, gate, conv_w, conv_b, ln_g, ln_b):
    glu = val * jax.nn.sigmoid(gate)
    y = lax.conv_general_dilated(
        glu, conv_w.astype(glu.dtype), window_strides=(1,),
        padding=[(CONV_WIDTH - 1, 0)],
        dimension_numbers=('NWC', 'WIO', 'NWC'),
        feature_group_count=D_CONV) + conv_b
    yf = y.astype(jnp.float32)
    mu = jnp.mean(yf, axis=-1, keepdims=True)
    var = jnp.mean(jnp.square(yf - mu), axis=-1, keepdims=True)
    ln = (yf - mu) * lax.rsqrt(var + LN_EPS) * ln_g + ln_b
    return jax.nn.silu(ln).astype(val.dtype)


def rope(t, positions):
    half = HEAD_DIM // 2
    inv = ROPE_THETA ** (-jnp.arange(half, dtype=jnp.float32) / half)
    ang = positions.astype(jnp.float32)[..., None] * inv
    cos = jnp.cos(ang)[:, :, None, :]
    sin = jnp.sin(ang)[:, :, None, :]
    t1, t2 = t[..., :half], t[..., half:]
    return jnp.concatenate([t1 * cos - t2 * sin, t2 * cos + t1 * sin], axis=-1)


def sliding_window_attention(q, k, v, sinks):
    B, L, H, Dh = q.shape
    nb = L // ATTN_BLOCK
    G = H // N_KV_HEADS
    qb = q.reshape(B, nb, ATTN_BLOCK, N_KV_HEADS, G, Dh)

    def band(t):
        tb = t.reshape(B, nb, ATTN_BLOCK, N_KV_HEADS, Dh)
        prev = jnp.pad(tb, ((0, 0), (1, 0), (0, 0), (0, 0), (0, 0)))[:, :nb]
        return jnp.concatenate([prev, tb], axis=2)

    kb, vb = band(k), band(v)
    s = jnp.einsum('bnqkgd,bnskd->bnkgqs', qb, kb) * (Dh ** -0.5)
    qi = jnp.arange(ATTN_BLOCK)[:, None]
    si = jnp.arange(2 * ATTN_BLOCK)[None, :]
    rel = ATTN_BLOCK + qi - si
    in_window = (rel >= 0) & (rel < WINDOW)
    first = (jnp.arange(nb) == 0)[:, None, None] & (si < ATTN_BLOCK)[None]
    mask = (in_window[None] & ~first)[None, :, None, None]
    sink = sinks.astype(jnp.float32).reshape(1, 1, N_KV_HEADS, G, 1, 1)
    s = jnp.where(mask, s, -jnp.inf)
    m = jnp.maximum(jnp.max(s, axis=-1, keepdims=True), sink)
    p = jnp.exp(s - m)
    denom = jnp.sum(p, axis=-1, keepdims=True) + jnp.exp(sink - m)
    o = jnp.einsum('bnkgqs,bnskd->bnqkgd', p / denom, vb)
    return o.reshape(B, L, H * Dh)


def moe_ffn(h, router_w, w_gate, w_up, w_down):
    B, L, D = h.shape
    T = B * L
    hf = h.reshape(T, D)
    logits = (hf @ router_w).astype(jnp.float32)
    top_v, top_i = lax.top_k(logits, TOP_K)
    gates = jax.nn.softmax(top_v, axis=-1)
    A = T * TOP_K
    flat_e = top_i.reshape(A).astype(jnp.int32)
    flat_tok = jnp.arange(A, dtype=jnp.int32) // TOP_K
    flat_g = gates.reshape(A)
    order = jnp.argsort(flat_e)
    sorted_e = flat_e[order]
    counts = jnp.bincount(flat_e, length=N_EXPERTS).astype(jnp.int32)
    padded = ((counts + MOE_BLOCK - 1) // MOE_BLOCK) * MOE_BLOCK
    start = jnp.cumsum(counts) - counts
    pend = jnp.cumsum(padded)
    pstart = pend - padded
    dest = pstart[sorted_e] + jnp.arange(A, dtype=jnp.int32) - start[sorted_e]
    P = ((A + N_EXPERTS * (MOE_BLOCK - 1) + MOE_BLOCK - 1) // MOE_BLOCK) * MOE_BLOCK
    nb = P // MOE_BLOCK
    buf_tok = jnp.full((P,), T, jnp.int32).at[dest].set(flat_tok[order])
    buf_gate = jnp.zeros((P,), jnp.float32).at[dest].set(flat_g[order])
    blk_e = jnp.minimum(
        jnp.searchsorted(pend, jnp.arange(nb, dtype=jnp.int32) * MOE_BLOCK, side='right'),
        N_EXPERTS - 1)
    x_pad = jnp.concatenate([hf, jnp.zeros((1, D), hf.dtype)], axis=0)
    xb = x_pad[buf_tok].reshape(nb, MOE_BLOCK, D)

    def expert_block(args):
        xblk, e = args
        return swiglu(xblk, w_gate[e], w_up[e], w_down[e])

    yb = lax.map(expert_block, (xb, blk_e)).reshape(P, D)
    out = jnp.zeros((T + 1, D), jnp.float32).at[buf_tok].add(
        yb.astype(jnp.float32) * buf_gate[:, None])
    return out[:T].reshape(B, L, D).astype(h.dtype)


def even_layer(x, c, w_mod, b_mod, w_in, pool_w, pool_scale, conv_w, conv_b,
               conv_ln_g, conv_ln_b, w_out, ffn_w_gate, ffn_w_up, ffn_w_down):
    sh1, sc1, g1, sh2, sc2, g2 = adaln_params(c, w_mod, b_mod)
    h = modulate(x, sh1, sc1)
    z = h @ w_in
    u = z[..., :D_POOL]
    val = z[..., D_POOL:D_POOL + D_CONV]
    gate = z[..., D_POOL + D_CONV:]
    mixed = jnp.concatenate(
        [pool_mixer(u, pool_w, pool_scale),
         conv_module(val, gate, conv_w, conv_b, conv_ln_g, conv_ln_b)], axis=-1)
    x = x + g1[:, None, :] * (mixed @ w_out)
    h = modulate(x, sh2, sc2)
    x = x + g2[:, None, :] * swiglu(h, ffn_w_gate, ffn_w_up, ffn_w_down)
    return x


def odd_layer(x, c, positions, w_mod, b_mod, w_qkv, q_norm, k_norm, sinks, w_o,
              router_w, exp_w_gate, exp_w_up, exp_w_down):
    sh1, sc1, g1, sh2, sc2, g2 = adaln_params(c, w_mod, b_mod)
    B, L, _ = x.shape
    h = modulate(x, sh1, sc1)
    qkv = h @ w_qkv
    nq = N_HEADS * HEAD_DIM
    nk = N_KV_HEADS * HEAD_DIM
    q = qkv[..., :nq].reshape(B, L, N_HEADS, HEAD_DIM)
    k = qkv[..., nq:nq + nk].reshape(B, L, N_KV_HEADS, HEAD_DIM)
    v = qkv[..., nq + nk:].reshape(B, L, N_KV_HEADS, HEAD_DIM).astype(jnp.float32)
    q = rope(rms_norm(q) * q_norm, positions)
    k = rope(rms_norm(k) * k_norm, positions)
    attn = sliding_window_attention(q, k, v, sinks).astype(x.dtype)
    x = x + g1[:, None, :] * (attn @ w_o)
    h = modulate(x, sh2, sc2)
    x = x + g2[:, None, :] * moe_ffn(h, router_w, exp_w_gate, exp_w_up, exp_w_down)
    return x


def setup_inputs(seed: int = 0) -> dict:
    key = jax.random.key(seed)
    ks = jax.random.split(key, 32)

    def nrm(k, shape, scale):
        return jax.random.normal(k, shape, jnp.float32) * scale

    D = D_MODEL
    mod_scale = 0.5 * D ** -0.5
    inp = {}
    inp['x'] = nrm(ks[0], (BATCH, SEQ, D), 1.0)
    inp['c'] = nrm(ks[1], (BATCH, D), 1.0)
    offs = jax.random.randint(ks[2], (BATCH, 1), 0, 1024, dtype=jnp.int32)
    inp['positions'] = (offs + jnp.arange(SEQ, dtype=jnp.int32)[None, :]).astype(jnp.int32)
    inp['l0_w_mod'] = nrm(ks[3], (D, 6 * D), mod_scale)
    inp['l0_b_mod'] = nrm(ks[4], (6 * D,), 0.02)
    inp['l0_w_in'] = nrm(ks[5], (D, D_POOL + 2 * D_CONV), D ** -0.5)
    inp['l0_pool_w'] = nrm(ks[6], (N_POOL_GROUPS, POOL_GROUP_DIM, POOL_GROUP_DIM), POOL_GROUP_DIM ** -0.5)
    inp['l0_pool_scale'] = 1.0 + nrm(ks[7], (D_POOL,), 0.02)
    inp['l0_conv_w'] = nrm(ks[8], (CONV_WIDTH, 1, D_CONV), CONV_WIDTH ** -0.5)
    inp['l0_conv_b'] = nrm(ks[9], (D_CONV,), 0.02)
    inp['l0_conv_ln_g'] = 1.0 + nrm(ks[10], (D_CONV,), 0.02)
    inp['l0_conv_ln_b'] = nrm(ks[11], (D_CONV,), 0.02)
    inp['l0_w_out'] = nrm(ks[12], (D_POOL + D_CONV, D), (D_POOL + D_CONV) ** -0.5)
    inp['l0_ffn_w_gate'] = nrm(ks[13], (D, D_FF), D ** -0.5)
    inp['l0_ffn_w_up'] = nrm(ks[14], (D, D_FF), D ** -0.5)
    inp['l0_ffn_w_down'] = nrm(ks[15], (D_FF, D), D_FF ** -0.5)
    inp['l1_w_mod'] = nrm(ks[16], (D, 6 * D), mod_scale)
    inp['l1_b_mod'] = nrm(ks[17], (6 * D,), 0.02)
    inp['l1_w_qkv'] = nrm(ks[18], (D, (N_HEADS + 2 * N_KV_HEADS) * HEAD_DIM), D ** -0.5)
    inp['l1_q_norm'] = 1.0 + nrm(ks[19], (HEAD_DIM,), 0.02)
    inp['l1_k_norm'] = 1.0 + nrm(ks[20], (HEAD_DIM,), 0.02)
    inp['l1_sinks'] = nrm(ks[21], (N_HEADS,), 0.5)
    inp['l1_w_o'] = nrm(ks[22], (N_HEADS * HEAD_DIM, D), (N_HEADS * HEAD_DIM) ** -0.5)
    inp['l1_router_w'] = nrm(ks[23], (D, N_EXPERTS), D ** -0.5)
    inp['l1_exp_w_gate'] = nrm(ks[24], (N_EXPERTS, D, D_FF_EXPERT), D ** -0.5)
    inp['l1_exp_w_up'] = nrm(ks[25], (N_EXPERTS, D, D_FF_EXPERT), D ** -0.5)
    inp['l1_exp_w_down'] = nrm(ks[26], (N_EXPERTS, D_FF_EXPERT, D), D_FF_EXPERT ** -0.5)
    return inp


def reference(x, c, positions,
              l0_w_mod, l0_b_mod, l0_w_in, l0_pool_w, l0_pool_scale, l0_conv_w,
              l0_conv_b, l0_conv_ln_g, l0_conv_ln_b, l0_w_out, l0_ffn_w_gate,
              l0_ffn_w_up, l0_ffn_w_down,
              l1_w_mod, l1_b_mod, l1_w_qkv, l1_q_norm, l1_k_norm, l1_sinks, l1_w_o,
              l1_router_w, l1_exp_w_gate, l1_exp_w_up, l1_exp_w_down):
    layer_params = [
        (l0_w_mod, l0_b_mod, l0_w_in, l0_pool_w, l0_pool_scale, l0_conv_w, l0_conv_b,
         l0_conv_ln_g, l0_conv_ln_b, l0_w_out, l0_ffn_w_gate, l0_ffn_w_up, l0_ffn_w_down),
        (l1_w_mod, l1_b_mod, l1_w_qkv, l1_q_norm, l1_k_norm, l1_sinks, l1_w_o,
         l1_router_w, l1_exp_w_gate, l1_exp_w_up, l1_exp_w_down),
    ]
    for i in range(DEPTH):
        if i % 2 == 0:
            x = even_layer(x, c, *layer_params[i])
        else:
            x = odd_layer(x, c, positions, *layer_params[i])
    return x
```

```python
import functools

import jax
import jax.numpy as jnp
from jax import lax
from jax.experimental import pallas as pl
from jax.experimental.pallas import tpu as pltpu

F32 = jnp.float32
BF16 = jnp.bfloat16

HEAD_DIM = 64
Q_PER_KV = 8
POOL_WINDOWS = (2, 4, 8, 16)
CONV_WIDTH = 31
WINDOW = 128
ROPE_THETA = 10000.0
N_EXPERTS = 8
TOP_K = 2
NORM_EPS = 1e-6
LN_EPS = 1e-5

LANES = 128
SUBLANES = 8
HALO = 32
MOE_BLOCK = 512
VMEM_LIMIT = 56 * 1024 * 1024
NEG_INF = float("-inf")


def _params(sem, vmem=VMEM_LIMIT):
    return pltpu.CompilerParams(dimension_semantics=sem, vmem_limit_bytes=vmem)


def _rms_modulate(x, shift, scale):
    ms = jnp.mean(x * x, axis=-1, keepdims=True)
    return x * lax.rsqrt(ms + NORM_EPS) * (1.0 + scale) + shift


def _dot(a, b):
    return jnp.dot(a, b, preferred_element_type=F32)


def _dot_nt(a, b):
    return lax.dot_general(a, b, (((1,), (1,)), ((), ())), preferred_element_type=F32)


def _adaln_kernel(c_ref, w_ref, b_ref, o_ref):
    sc = jax.nn.silu(c_ref[...]).astype(BF16)
    o_ref[...] = _dot(sc, w_ref[...].astype(BF16)) + b_ref[...]


def _adaln(c_pad, w_mod, b_mod):
    d, n = w_mod.shape
    tn = 1024
    return pl.pallas_call(
        _adaln_kernel,
        grid=(n // tn,),
        in_specs=[
            pl.BlockSpec((SUBLANES, d), lambda j: (0, 0)),
            pl.BlockSpec((d, tn), lambda j: (0, j)),
            pl.BlockSpec((1, tn), lambda j: (0, j)),
        ],
        out_specs=pl.BlockSpec((SUBLANES, tn), lambda j: (0, j)),
        out_shape=jax.ShapeDtypeStruct((SUBLANES, n), F32),
        compiler_params=_params(("arbitrary",)),
        name="adaln",
    )(c_pad, w_mod, b_mod.reshape(1, n))


def _l0_mixer_kernel(x_ref, sh_ref, sc_ref, g_ref, w_in_ref, pool_w_ref, pool_scale_ref,
                     conv_w_ref, conv_b_ref, ln_g_ref, ln_b_ref, w_out_ref, o_ref,
                     u_ext, glu_ext, y_ref, mixed_ref, *, tm, d_pool, d_conv, gd):
    l = pl.program_id(1)

    @pl.when(l == 0)
    def _():
        u_ext[0:HALO, :] = jnp.zeros((HALO, d_pool), F32)
        glu_ext[0:HALO, :] = jnp.zeros((HALO, d_conv), F32)

    @pl.when(l > 0)
    def _():
        u_ext[0:HALO, :] = u_ext[tm:tm + HALO, :]
        glu_ext[0:HALO, :] = glu_ext[tm:tm + HALO, :]

    x = x_ref[0]
    h = _rms_modulate(x, sh_ref[0], sc_ref[0]).astype(BF16)
    z = _dot(h, w_in_ref[...])
    u_ext[HALO:HALO + tm, :] = z[:, :d_pool]
    glu_ext[HALO:HALO + tm, :] = z[:, d_pool:d_pool + d_conv] * jax.nn.sigmoid(z[:, d_pool + d_conv:])

    t1 = l * tm + lax.broadcasted_iota(jnp.int32, (tm, 1), 0) + 1
    for g, w in enumerate(POOL_WINDOWS):
        c0 = g * gd
        tok = u_ext[HALO:HALO + tm, c0:c0 + gd]
        s = tok
        for k in range(1, w):
            s = s + u_ext[HALO - k:HALO - k + tm, c0:c0 + gd]
        inv_cnt = 1.0 / jnp.minimum(t1, w).astype(F32)
        pooled = s * inv_cnt - tok
        mixed = _dot(pooled.astype(BF16), pool_w_ref[g]) * pool_scale_ref[:, c0:c0 + gd]
        mixed_ref[:, c0:c0 + gd] = mixed.astype(BF16)

    rows = 64
    base = HALO - (CONV_WIDTH - 1)
    for r0 in range(0, tm, rows):
        for c0 in range(0, d_conv, LANES):
            acc = jnp.broadcast_to(conv_b_ref[:, c0:c0 + LANES], (rows, LANES))
            for j in range(CONV_WIDTH):
                acc = acc + conv_w_ref[j:j + 1, c0:c0 + LANES] * \
                    glu_ext[base + j + r0:base + j + r0 + rows, c0:c0 + LANES]
            y_ref[r0:r0 + rows, c0:c0 + LANES] = acc

    y = y_ref[...]
    mu = jnp.mean(y, axis=-1, keepdims=True)
    yc = y - mu
    var = jnp.mean(yc * yc, axis=-1, keepdims=True)
    ln = yc * lax.rsqrt(var + LN_EPS) * ln_g_ref[...] + ln_b_ref[...]
    mixed_ref[:, d_pool:] = jax.nn.silu(ln).astype(BF16)

    out = _dot(mixed_ref[...], w_out_ref[...])
    o_ref[0] = x + g_ref[0] * out


def _l0_mixer(x, sh, sc, g, w_in, pool_w, pool_scale, conv_w, conv_b, ln_g, ln_b, w_out, *, tm):
    B, L, D = x.shape
    d_pool = pool_scale.shape[0]
    d_conv = conv_b.shape[0]
    gd = d_pool // len(POOL_WINDOWS)
    const2 = lambda b, l: (0, 0)
    const3 = lambda b, l: (0, 0, 0)
    per_b = pl.BlockSpec((1, 1, D), lambda b, l: (b, 0, 0))
    kern = functools.partial(_l0_mixer_kernel, tm=tm, d_pool=d_pool, d_conv=d_conv, gd=gd)
    return pl.pallas_call(
        kern,
        grid=(B, L // tm),
        in_specs=[
            pl.BlockSpec((1, tm, D), lambda b, l: (b, l, 0)),
            per_b, per_b, per_b,
            pl.BlockSpec(w_in.shape, const2, pipeline_mode=pl.Buffered(1)),
            pl.BlockSpec(pool_w.shape, const3, pipeline_mode=pl.Buffered(1)),
            pl.BlockSpec((1, d_pool), const2),
            pl.BlockSpec((CONV_WIDTH, d_conv), const2),
            pl.BlockSpec((1, d_conv), const2),
            pl.BlockSpec((1, d_conv), const2),
            pl.BlockSpec((1, d_conv), const2),
            pl.BlockSpec(w_out.shape, const2, pipeline_mode=pl.Buffered(1)),
        ],
        out_specs=pl.BlockSpec((1, tm, D), lambda b, l: (b, l, 0)),
        out_shape=jax.ShapeDtypeStruct((B, L, D), F32),
        scratch_shapes=[
            pltpu.VMEM((HALO + tm, d_pool), F32),
            pltpu.VMEM((HALO + tm, d_conv), F32),
            pltpu.VMEM((tm, d_conv), F32),
            pltpu.VMEM((tm, d_pool + d_conv), BF16),
        ],
        compiler_params=_params(("arbitrary", "arbitrary")),
        name="l0_mixer",
    )(x, sh, sc, g, w_in, pool_w, pool_scale.reshape(1, d_pool), conv_w.reshape(CONV_WIDTH, d_conv),
      conv_b.reshape(1, d_conv), ln_g.reshape(1, d_conv), ln_b.reshape(1, d_conv), w_out)


def _l0_ffn_kernel(x_ref, sh_ref, sc_ref, g_ref, wg_ref, wu_ref, wd_ref, o_ref, h_ref):
    j = pl.program_id(1)

    @pl.when(j == 0)
    def _():
        h_ref[...] = _rms_modulate(x_ref[...], sh_ref[0], sc_ref[0]).astype(BF16)

    h = h_ref[...]
    a = (jax.nn.silu(_dot(h, wg_ref[...])) * _dot(h, wu_ref[...])).astype(BF16)
    part = _dot(a, wd_ref[...])

    @pl.when(j == 0)
    def _():
        o_ref[...] = part

    @pl.when(j > 0)
    def _():
        o_ref[...] += part

    @pl.when(j == pl.num_programs(1) - 1)
    def _():
        o_ref[...] = x_ref[...] + g_ref[0] * o_ref[...]


def _l0_ffn(x2d, sh, sc, g, wg, wu, wd, *, seq, tm, tf):
    T, D = x2d.shape
    dff = wg.shape[1]
    per_b = pl.BlockSpec((1, 1, D), lambda i, j: (i // (seq // tm), 0, 0))
    return pl.pallas_call(
        _l0_ffn_kernel,
        grid=(T // tm, dff // tf),
        in_specs=[
            pl.BlockSpec((tm, D), lambda i, j: (i, 0)),
            per_b, per_b, per_b,
            pl.BlockSpec((D, tf), lambda i, j: (0, j)),
            pl.BlockSpec((D, tf), lambda i, j: (0, j)),
            pl.BlockSpec((tf, D), lambda i, j: (j, 0)),
        ],
        out_specs=pl.BlockSpec((tm, D), lambda i, j: (i, 0)),
        out_shape=jax.ShapeDtypeStruct((T, D), F32),
        scratch_shapes=[pltpu.VMEM((tm, D), BF16)],
        compiler_params=_params(("arbitrary", "arbitrary")),
        name="l0_ffn",
    )(x2d, sh, sc, g, wg, wu, wd)


def _l1_qkv_kernel(x_ref, sh_ref, sc_ref, pos_ref, inv_ref, qn_ref, kn_ref, bd_ref, w_ref,
                   q_ref, k_ref, v_ref, *, n_q, n_kv):
    x = x_ref[0]
    h = _rms_modulate(x, sh_ref[0], sc_ref[0]).astype(BF16)
    qkv = _dot(h, w_ref[...])

    ang = pos_ref[0].astype(F32) * inv_ref[...]
    cos = jnp.cos(ang)
    sin = jnp.sin(ang)
    lane = lax.broadcasted_iota(jnp.int32, (1, LANES), 1)
    first_half = (lane % HEAD_DIM) < (HEAD_DIM // 2)
    sin_signed = jnp.where(first_half, -sin, sin)
    low_head = lane < HEAD_DIM
    bd = bd_ref[...]

    def norm_rope(blk, nw):
        sq = blk * blk
        hi = sq.astype(BF16)
        lo = (sq - hi.astype(F32)).astype(BF16)
        ss = _dot(hi, bd) + _dot(lo, bd)
        n = blk * lax.rsqrt(ss * (1.0 / HEAD_DIM) + NORM_EPS) * nw
        partner = jnp.where(first_half, pltpu.roll(n, LANES - HEAD_DIM // 2, 1),
                            pltpu.roll(n, HEAD_DIM // 2, 1))
        return n * cos + partner * sin_signed

    def split_heads(blk):
        a_lo = jnp.where(low_head, blk, 0.0)
        b_hi = jnp.where(low_head, 0.0, blk)
        return (a_lo, pltpu.roll(a_lo, HEAD_DIM, 1), pltpu.roll(b_hi, HEAD_DIM, 1), b_hi)

    scale = HEAD_DIM ** -0.5
    for cb in range(n_q):
        blk = qkv[:, cb * LANES:(cb + 1) * LANES]
        q_ref[0, :, cb * LANES:(cb + 1) * LANES] = (norm_rope(blk, qn_ref[...]) * scale).astype(BF16)
    k0 = n_q * LANES
    v0 = k0 + n_kv * LANES
    for cb in range(n_kv):
        kr = norm_rope(qkv[:, k0 + cb * LANES:k0 + (cb + 1) * LANES], kn_ref[...])
        for i, part in enumerate(split_heads(kr)):
            k_ref[0, :, (4 * cb + i) * LANES:(4 * cb + i + 1) * LANES] = part.astype(BF16)
        vr = qkv[:, v0 + cb * LANES:v0 + (cb + 1) * LANES]
        for i, part in enumerate(split_heads(vr)):
            v_ref[0, :, (4 * cb + i) * LANES:(4 * cb + i + 1) * LANES] = part.astype(BF16)


def _l1_qkv(x, sh, sc, pos3, inv_t, qn_t, kn_t, bd, w_qkv, *, tm):
    B, L, D = x.shape
    n_heads = D // HEAD_DIM
    n_kvh = n_heads // Q_PER_KV
    n_q = n_heads * HEAD_DIM // LANES
    n_kv = n_kvh * HEAD_DIM // LANES
    kw = n_kvh * 2 * LANES
    const2 = lambda b, l: (0, 0)
    per_b = pl.BlockSpec((1, 1, D), lambda b, l: (b, 0, 0))
    row = lambda w: pl.BlockSpec((1, tm, w), lambda b, l: (b, l, 0))
    kern = functools.partial(_l1_qkv_kernel, n_q=n_q, n_kv=n_kv)
    return pl.pallas_call(
        kern,
        grid=(B, L // tm),
        in_specs=[
            row(D), per_b, per_b, row(1),
            pl.BlockSpec((1, LANES), const2),
            pl.BlockSpec((1, LANES), const2),
            pl.BlockSpec((1, LANES), const2),
            pl.BlockSpec((LANES, LANES), const2),
            pl.BlockSpec(w_qkv.shape, const2, pipeline_mode=pl.Buffered(1)),
        ],
        out_specs=[row(D), row(kw), row(kw)],
        out_shape=[jax.ShapeDtypeStruct((B, L, D), BF16),
                   jax.ShapeDtypeStruct((B, L, kw), BF16),
                   jax.ShapeDtypeStruct((B, L, kw), BF16)],
        compiler_params=_params(("arbitrary", "arbitrary")),
        name="l1_qkv",
    )(x, sh, sc, pos3, inv_t, qn_t, kn_t, bd, w_qkv)


def _l1_attn_kernel(sinks_ref, q_ref, kc_ref, kp_ref, vc_ref, vp_ref, x_ref, g_ref, wo_ref,
                    o_ref, attn_ref, *, tq, n_pairs):
    i = pl.program_id(1)
    qi = lax.broadcasted_iota(jnp.int32, (WINDOW, WINDOW), 0)
    kj = lax.broadcasted_iota(jnp.int32, (WINDOW, WINDOW), 1)
    mask_cur = kj <= qi
    mask_prev_band = kj > qi

    for n in range(tq // WINDOW):
        r0 = n * WINDOW
        if n == 0:
            k_prev, v_prev = kp_ref[0], vp_ref[0]
            mask_prev = kj > qi + jnp.where(i > 0, 0, WINDOW)
        else:
            k_prev, v_prev = kc_ref[0, r0 - WINDOW:r0, :], vc_ref[0, r0 - WINDOW:r0, :]
            mask_prev = mask_prev_band
        k_cur, v_cur = kc_ref[0, r0:r0 + WINDOW, :], vc_ref[0, r0:r0 + WINDOW, :]
        for p in range(n_pairs):
            q2 = q_ref[0, r0:r0 + WINDOW, p * LANES:(p + 1) * LANES]
            kv = (2 * p) // Q_PER_KV
            acc = jnp.zeros((WINDOW, LANES), F32)
            for half in range(2):
                c0 = (2 * kv + half) * LANES
                s_p = jnp.where(mask_prev, _dot_nt(q2, k_prev[:, c0:c0 + LANES]), NEG_INF)
                s_c = jnp.where(mask_cur, _dot_nt(q2, k_cur[:, c0:c0 + LANES]), NEG_INF)
                sink = sinks_ref[2 * p + half]
                m = jnp.maximum(jnp.maximum(jnp.max(s_p, axis=1, keepdims=True),
                                            jnp.max(s_c, axis=1, keepdims=True)), sink)
                e_p = jnp.exp(s_p - m)
                e_c = jnp.exp(s_c - m)
                denom = (jnp.sum(e_p, axis=1, keepdims=True) + jnp.sum(e_c, axis=1, keepdims=True)
                         + jnp.exp(sink - m))
                o_h = _dot(e_p.astype(BF16), v_prev[:, c0:c0 + LANES]) + \
                    _dot(e_c.astype(BF16), v_cur[:, c0:c0 + LANES])
                acc = acc + o_h * (1.0 / denom)
            attn_ref[r0:r0 + WINDOW, p * LANES:(p + 1) * LANES] = acc.astype(BF16)

    out = _dot(attn_ref[...], wo_ref[...])
    o_ref[0] = x_ref[0] + g_ref[0] * out


def _l1_attn(sinks, q, k, v, x, g, w_o, *, tq):
    B, L, D = x.shape
    kw = k.shape[2]
    nb = tq // WINDOW
    cur = lambda w: pl.BlockSpec((1, tq, w), lambda b, i: (b, i, 0))
    prev = pl.BlockSpec((1, WINDOW, kw), lambda b, i: (b, jnp.maximum(i * nb - 1, 0), 0))
    kern = functools.partial(_l1_attn_kernel, tq=tq, n_pairs=D // LANES)
    return pl.pallas_call(
        kern,
        grid=(B, L // tq),
        in_specs=[
            pl.BlockSpec(memory_space=pltpu.SMEM),
            cur(D), cur(kw), prev, cur(kw), prev, cur(D),
            pl.BlockSpec((1, 1, D), lambda b, i: (b, 0, 0)),
            pl.BlockSpec(w_o.shape, lambda b, i: (0, 0), pipeline_mode=pl.Buffered(1)),
        ],
        out_specs=cur(D),
        out_shape=jax.ShapeDtypeStruct((B, L, D), F32),
        scratch_shapes=[pltpu.VMEM((tq, D), BF16)],
        compiler_params=_params(("arbitrary", "arbitrary")),
        name="l1_attn",
    )(sinks, q, k, k, v, v, x, g, w_o)


def _l1_router_kernel(x_ref, sh_ref, sc_ref, rw_ref, h_ref, meta_ref, gate_ref, cnt_ref, carry,
                      *, tm):
    i = pl.program_id(0)

    @pl.when(i == 0)
    def _():
        carry[...] = jnp.zeros_like(carry)

    h = _rms_modulate(x_ref[...], sh_ref[0], sc_ref[0])
    h_ref[...] = h

    rw = rw_ref[...]
    h_hi = h.astype(BF16)
    h_lo = (h - h_hi.astype(F32)).astype(BF16)
    w_hi = rw.astype(BF16)
    w_lo = (rw - w_hi.astype(F32)).astype(BF16)
    logits = _dot_nt(w_hi, h_hi) + (_dot_nt(w_hi, h_lo) + _dot_nt(w_lo, h_hi))

    eidx = lax.broadcasted_iota(jnp.int32, (N_EXPERTS, tm), 0)
    m1 = jnp.max(logits, axis=0, keepdims=True)
    i1 = jnp.min(jnp.where(logits == m1, eidx, N_EXPERTS), axis=0, keepdims=True)
    rest = jnp.where(eidx == i1, NEG_INF, logits)
    m2 = jnp.max(rest, axis=0, keepdims=True)
    i2 = jnp.min(jnp.where(rest == m2, eidx, N_EXPERTS), axis=0, keepdims=True)
    e2 = jnp.exp(m2 - m1)
    gate1 = 1.0 / (1.0 + e2)
    gate2 = e2 / (1.0 + e2)

    sel1 = eidx == i1
    sel2 = eidx == i2
    ind = (sel1 | sel2).astype(F32)
    before = lax.broadcasted_iota(jnp.int32, (tm, tm), 0) < lax.broadcasted_iota(jnp.int32, (tm, tm), 1)
    excl = _dot(ind.astype(BF16), before.astype(BF16)) + carry[...]
    r1 = jnp.sum(jnp.where(sel1, excl, 0.0), axis=0, keepdims=True).astype(jnp.int32)
    r2 = jnp.sum(jnp.where(sel2, excl, 0.0), axis=0, keepdims=True).astype(jnp.int32)
    carry[...] = carry[...] + jnp.sum(ind, axis=1, keepdims=True)
    cnt_ref[...] = jnp.broadcast_to(carry[...], cnt_ref.shape).astype(jnp.int32)

    meta_ref[...] = jnp.where(eidx == 0, i1, jnp.where(eidx == 1, i2, jnp.where(eidx == 2, r1,
                              jnp.where(eidx == 3, r2, 0))))
    gate_ref[...] = jnp.where(eidx == 0, gate1, jnp.where(eidx == 1, gate2, 0.0))


def _l1_router(x2d, sh, sc, rw_t, *, seq, tm):
    T, D = x2d.shape
    per_b = pl.BlockSpec((1, 1, D), lambda i: (i // (seq // tm), 0, 0))
    kern = functools.partial(_l1_router_kernel, tm=tm)
    return pl.pallas_call(
        kern,
        grid=(T // tm,),
        in_specs=[
            pl.BlockSpec((tm, D), lambda i: (i, 0)),
            per_b, per_b,
            pl.BlockSpec((N_EXPERTS, D), lambda i: (0, 0)),
        ],
        out_specs=[
            pl.BlockSpec((tm, D), lambda i: (i, 0)),
            pl.BlockSpec((N_EXPERTS, tm), lambda i: (0, i)),
            pl.BlockSpec((N_EXPERTS, tm), lambda i: (0, i)),
            pl.BlockSpec((N_EXPERTS, LANES), lambda i: (0, 0)),
        ],
        out_shape=[
            jax.ShapeDtypeStruct((T, D), F32),
            jax.ShapeDtypeStruct((N_EXPERTS, T), jnp.int32),
            jax.ShapeDtypeStruct((N_EXPERTS, T), F32),
            jax.ShapeDtypeStruct((N_EXPERTS, LANES), jnp.int32),
        ],
        scratch_shapes=[pltpu.VMEM((N_EXPERTS, 1), F32)],
        compiler_params=_params(("arbitrary",)),
        name="l1_router",
    )(x2d, sh, sc, rw_t)


def _dispatch_kernel(pstart_ref, fill_ref, meta_ref, h_ref, zeros_ref, xb_ref, sem, zsem, *, td):
    i = pl.program_id(0)

    @pl.when(i == 0)
    def _():
        for e in range(N_EXPERTS):
            start = pl.multiple_of((fill_ref[e] // SUBLANES) * SUBLANES, SUBLANES)
            cp = pltpu.make_async_copy(zeros_ref, xb_ref.at[pl.ds(start, MOE_BLOCK + SUBLANES)], zsem)
            cp.start()
            cp.wait()

        def fill_tail(b, carry):
            start = pl.multiple_of(b * MOE_BLOCK, MOE_BLOCK)
            cp = pltpu.make_async_copy(zeros_ref.at[pl.ds(0, MOE_BLOCK)],
                                       xb_ref.at[pl.ds(start, MOE_BLOCK)], zsem)
            cp.start()
            cp.wait()
            return carry

        lax.fori_loop(fill_ref[N_EXPERTS] // MOE_BLOCK, xb_ref.shape[0] // MOE_BLOCK, fill_tail, 0)

    def row_copy(tok, dst):
        return pltpu.make_async_copy(h_ref.at[pl.ds(tok, 1)], xb_ref.at[pl.ds(dst, 1)], sem)

    def issue(t, carry):
        for k in range(TOP_K):
            dst = pstart_ref[meta_ref[k, t]] + meta_ref[TOP_K + k, t]
            row_copy(i * td + t, dst).start()
        return carry

    lax.fori_loop(0, td, issue, 0)

    def drain(t, carry):
        for k in range(TOP_K):
            row_copy(0, 0).wait()
        return carry

    lax.fori_loop(0, td, drain, 0)


def _dispatch(pstart, fill, meta, h, zeros_blk, *, n_rows, td):
    T, D = h.shape
    kern = functools.partial(_dispatch_kernel, td=td)
    return pl.pallas_call(
        kern,
        grid_spec=pltpu.PrefetchScalarGridSpec(
            num_scalar_prefetch=2,
            grid=(T // td,),
            in_specs=[
                pl.BlockSpec((N_EXPERTS, td), lambda i, *_: (0, i), memory_space=pltpu.SMEM),
                pl.BlockSpec(memory_space=pl.ANY),
                pl.BlockSpec(memory_space=pl.ANY),
            ],
            out_specs=pl.BlockSpec(memory_space=pl.ANY),
            scratch_shapes=[pltpu.SemaphoreType.DMA(()), pltpu.SemaphoreType.DMA(())],
        ),
        out_shape=jax.ShapeDtypeStruct((n_rows, D), F32),
        compiler_params=_params(("arbitrary",)),
        name="dispatch",
    )(pstart, fill, meta, h, zeros_blk)


def _experts_kernel(blk_e_ref, nused_ref, x_ref, wg_ref, wu_ref, wd_ref, o_ref, xs_ref):
    b = pl.program_id(0)
    j = pl.program_id(1)

    @pl.when(b < nused_ref[0])
    def _():
        @pl.when(j == 0)
        def _():
            xs_ref[...] = x_ref[...].astype(BF16)

        xs = xs_ref[...]
        a = (jax.nn.silu(_dot(xs, wg_ref[...])) * _dot(xs, wu_ref[...])).astype(BF16)
        part = _dot(a, wd_ref[...])

        @pl.when(j == 0)
        def _():
            o_ref[...] = part

        @pl.when(j > 0)
        def _():
            o_ref[...] += part

    @pl.when((b >= nused_ref[0]) & (j == 0))
    def _():
        o_ref[...] = jnp.zeros_like(o_ref)


def _experts(blk_e, nused, xb, wg, wu, wd, *, n_blocks, tf):
    D = xb.shape[1]
    dff = wg.shape[2]
    nj = dff // tf

    def row_map(b, j, be, nu):
        return (jnp.minimum(b, nu[0] - 1), 0)

    def jj(b, j, nu):
        return jnp.where(b < nu[0], j, nj - 1)

    return pl.pallas_call(
        _experts_kernel,
        grid_spec=pltpu.PrefetchScalarGridSpec(
            num_scalar_prefetch=2,
            grid=(n_blocks, nj),
            in_specs=[
                pl.BlockSpec((MOE_BLOCK, D), row_map),
                pl.BlockSpec((None, D, tf), lambda b, j, be, nu: (be[b], 0, jj(b, j, nu))),
                pl.BlockSpec((None, D, tf), lambda b, j, be, nu: (be[b], 0, jj(b, j, nu))),
                pl.BlockSpec((None, tf, D), lambda b, j, be, nu: (be[b], jj(b, j, nu), 0)),
            ],
            out_specs=pl.BlockSpec((MOE_BLOCK, D), lambda b, j, be, nu: (b, 0)),
            scratch_shapes=[pltpu.VMEM((MOE_BLOCK, D), BF16)],
        ),
        out_shape=jax.ShapeDtypeStruct((n_blocks * MOE_BLOCK, D), F32),
        compiler_params=_params(("arbitrary", "arbitrary")),
        name="experts",
    )(blk_e, nused, xb, wg, wu, wd)


def _combine_kernel(pstart_ref, meta_ref, gate_ref, x_ref, g_ref, yb_ref, o_ref, buf, sem, *, tc):
    def row_copy(k, t, src):
        return pltpu.make_async_copy(yb_ref.at[pl.ds(src, 1)], buf.at[k, pl.ds(t, 1)], sem)

    def issue(t, carry):
        for k in range(TOP_K):
            src = pstart_ref[meta_ref[k, t]] + meta_ref[TOP_K + k, t]
            row_copy(k, t, src).start()
        return carry

    lax.fori_loop(0, tc, issue, 0)

    def drain(t, carry):
        for k in range(TOP_K):
            row_copy(k, 0, 0).wait()
        return carry

    lax.fori_loop(0, tc, drain, 0)

    gates = gate_ref[...]
    moe = gates[:, 0:1] * buf[0] + gates[:, 1:2] * buf[1]
    o_ref[...] = x_ref[...] + g_ref[0] * moe


def _combine(pstart, meta, gates_t, x2d, g, yb, *, seq, tc):
    T, D = x2d.shape
    kern = functools.partial(_combine_kernel, tc=tc)
    return pl.pallas_call(
        kern,
        grid_spec=pltpu.PrefetchScalarGridSpec(
            num_scalar_prefetch=1,
            grid=(T // tc,),
            in_specs=[
                pl.BlockSpec((N_EXPERTS, tc), lambda i, *_: (0, i), memory_space=pltpu.SMEM),
                pl.BlockSpec((tc, N_EXPERTS), lambda i, *_: (i, 0)),
                pl.BlockSpec((tc, D), lambda i, *_: (i, 0)),
                pl.BlockSpec((1, 1, D), lambda i, *_: (i // (seq // tc), 0, 0)),
                pl.BlockSpec(memory_space=pl.ANY),
            ],
            out_specs=pl.BlockSpec((tc, D), lambda i, *_: (i, 0)),
            scratch_shapes=[pltpu.VMEM((TOP_K, tc, D), F32), pltpu.SemaphoreType.DMA(())],
        ),
        out_shape=jax.ShapeDtypeStruct((T, D), F32),
        compiler_params=_params(("arbitrary",)),
        name="combine",
    )(pstart, meta, gates_t, x2d, g, yb)


def _mod_params(c, w_mod, b_mod):
    B, D = c.shape
    c_pad = jnp.zeros((SUBLANES, D), F32).at[:B].set(c)
    mod = _adaln(c_pad, w_mod, b_mod)[:B]
    return [m.reshape(B, 1, D) for m in jnp.split(mod, 6, axis=-1)]


def kernel(x, c, positions, l0_w_mod, l0_b_mod, l0_w_in, l0_pool_w, l0_pool_scale, l0_conv_w, l0_conv_b, l0_conv_ln_g, l0_conv_ln_b, l0_w_out, l0_ffn_w_gate, l0_ffn_w_up, l0_ffn_w_down, l1_w_mod, l1_b_mod, l1_w_qkv, l1_q_norm, l1_k_norm, l1_sinks, l1_w_o, l1_router_w, l1_exp_w_gate, l1_exp_w_up, l1_exp_w_down):
    B, L, D = x.shape
    T = B * L
    bf = lambda w: w.astype(BF16)

    sh1, sc1, g1, sh2, sc2, g2 = _mod_params(c, l0_w_mod, l0_b_mod)
    x = _l0_mixer(x, sh1, sc1, g1, bf(l0_w_in), bf(l0_pool_w), l0_pool_scale, l0_conv_w, l0_conv_b,
                  l0_conv_ln_g, l0_conv_ln_b, bf(l0_w_out), tm=min(256, L))
    x = _l0_ffn(x.reshape(T, D), sh2, sc2, g2, bf(l0_ffn_w_gate), bf(l0_ffn_w_up), bf(l0_ffn_w_down),
                seq=L, tm=min(512, L), tf=512 if l0_ffn_w_gate.shape[1] % 512 == 0 else 256)
    x = x.reshape(B, L, D)

    sh1, sc1, g1, sh2, sc2, g2 = _mod_params(c, l1_w_mod, l1_b_mod)
    half = HEAD_DIM // 2
    inv = ROPE_THETA ** (-jnp.arange(half, dtype=F32) / half)
    inv_t = jnp.tile(inv, LANES // half).reshape(1, LANES)
    qn_t = jnp.tile(l1_q_norm, LANES // HEAD_DIM).reshape(1, LANES)
    kn_t = jnp.tile(l1_k_norm, LANES // HEAD_DIM).reshape(1, LANES)
    lane = jnp.arange(LANES)
    bd = (lane[:, None] // HEAD_DIM == lane[None, :] // HEAD_DIM).astype(BF16)
    q, k, v = _l1_qkv(x, sh1, sc1, positions.reshape(B, L, 1), inv_t, qn_t, kn_t, bd, bf(l1_w_qkv),
                      tm=min(256, L))
    x = _l1_attn(l1_sinks, q, k, v, x, g1, bf(l1_w_o), tq=min(256, L))

    x2d = x.reshape(T, D)
    h, meta, gates, cnt = _l1_router(x2d, sh2, sc2, l1_router_w.T, seq=L, tm=min(512, L))
    counts = cnt[:, 0]
    padded = ((counts + MOE_BLOCK - 1) // MOE_BLOCK) * MOE_BLOCK
    pend = jnp.cumsum(padded)
    pstart = pend - padded
    n_blocks = (T * TOP_K + N_EXPERTS * (MOE_BLOCK - 1) + MOE_BLOCK - 1) // MOE_BLOCK
    nused = (pend[-1] // MOE_BLOCK).astype(jnp.int32).reshape(1)
    blk_start = jnp.minimum(jnp.arange(n_blocks, dtype=jnp.int32), nused[0] - 1) * MOE_BLOCK
    blk_e = jnp.minimum(jnp.searchsorted(pend, blk_start, side='right'), N_EXPERTS - 1).astype(jnp.int32)
    fill = jnp.concatenate([pstart + counts, pend[-1:]]).astype(jnp.int32)
    xb = _dispatch(pstart.astype(jnp.int32), fill, meta, h,
                   jnp.zeros((MOE_BLOCK + SUBLANES, D), F32), n_rows=(n_blocks + 2) * MOE_BLOCK,
                   td=min(1024, T))
    yb = _experts(blk_e, nused, xb, bf(l1_exp_w_gate), bf(l1_exp_w_up), bf(l1_exp_w_down),
                  n_blocks=n_blocks, tf=512)
    out = _combine(pstart.astype(jnp.int32), meta, gates.T, x2d, g2, yb, seq=L, tc=min(256, L))
    return out.reshape(B, L, D)
```

```python
import functools

import jax
import jax.numpy as jnp
from jax import lax
from jax.experimental import pallas as pl
from jax.experimental.pallas import tpu as pltpu

F32 = jnp.float32
BF16 = jnp.bfloat16

HEAD_DIM = 64
Q_PER_KV = 8
POOL_WINDOWS = (2, 4, 8, 16)
CONV_WIDTH = 31
WINDOW = 128
ROPE_THETA = 10000.0
N_EXPERTS = 8
TOP_K = 2
NORM_EPS = 1e-6
LN_EPS = 1e-5

LANES = 128
SUBLANES = 8
HALO = 32
MOE_BLOCK = 512
VMEM_LIMIT = 56 * 1024 * 1024
NEG_INF = float("-inf")


def _params(sem, vmem=VMEM_LIMIT):
    return pltpu.CompilerParams(dimension_semantics=sem, vmem_limit_bytes=vmem)


def _rms_modulate(x, shift, scale):
    ms = jnp.mean(x * x, axis=-1, keepdims=True)
    return x * lax.rsqrt(ms + NORM_EPS) * (1.0 + scale) + shift


def _dot(a, b):
    return jnp.dot(a, b, preferred_element_type=F32)


def _dot_nt(a, b):
    return lax.dot_general(a, b, (((1,), (1,)), ((), ())), preferred_element_type=F32)


def _adaln_kernel(c_ref, w_ref, b_ref, o_ref):
    sc = jax.nn.silu(c_ref[...]).astype(BF16)
    o_ref[...] = _dot(sc, w_ref[...].astype(BF16)) + b_ref[...]


def _adaln(c_pad, w_mod, b_mod):
    d, n = w_mod.shape
    tn = 1024
    return pl.pallas_call(
        _adaln_kernel,
        grid=(n // tn,),
        in_specs=[
            pl.BlockSpec((SUBLANES, d), lambda j: (0, 0)),
            pl.BlockSpec((d, tn), lambda j: (0, j)),
            pl.BlockSpec((1, tn), lambda j: (0, j)),
        ],
        out_specs=pl.BlockSpec((SUBLANES, tn), lambda j: (0, j)),
        out_shape=jax.ShapeDtypeStruct((SUBLANES, n), F32),
        compiler_params=_params(("arbitrary",)),
        name="adaln",
    )(c_pad, w_mod, b_mod.reshape(1, n))


def _l0_mixer_kernel(x_ref, sh_ref, sc_ref, g_ref, w_in_ref, pool_w_ref, pool_scale_ref,
                     conv_w_ref, conv_b_ref, ln_g_ref, ln_b_ref, w_out_ref, o_ref,
                     u_ext, glu_ext, y_ref, mixed_ref, *, tm, d_pool, d_conv, gd):
    l = pl.program_id(1)

    @pl.when(l == 0)
    def _():
        u_ext[0:HALO, :] = jnp.zeros((HALO, d_pool), F32)
        glu_ext[0:HALO, :] = jnp.zeros((HALO, d_conv), F32)

    @pl.when(l > 0)
    def _():
        u_ext[0:HALO, :] = u_ext[tm:tm + HALO, :]
        glu_ext[0:HALO, :] = glu_ext[tm:tm + HALO, :]

    x = x_ref[0]
    h = _rms_modulate(x, sh_ref[0], sc_ref[0]).astype(BF16)
    z = _dot(h, w_in_ref[...])
    u_ext[HALO:HALO + tm, :] = z[:, :d_pool]
    glu_ext[HALO:HALO + tm, :] = z[:, d_pool:d_pool + d_conv] * jax.nn.sigmoid(z[:, d_pool + d_conv:])

    t1 = l * tm + lax.broadcasted_iota(jnp.int32, (tm, 1), 0) + 1
    for g, w in enumerate(POOL_WINDOWS):
        c0 = g * gd
        tok = u_ext[HALO:HALO + tm, c0:c0 + gd]
        s = tok
        for k in range(1, w):
            s = s + u_ext[HALO - k:HALO - k + tm, c0:c0 + gd]
        inv_cnt = 1.0 / jnp.minimum(t1, w).astype(F32)
        pooled = s * inv_cnt - tok
        mixed = _dot(pooled.astype(BF16), pool_w_ref[g]) * pool_scale_ref[:, c0:c0 + gd]
        mixed_ref[:, c0:c0 + gd] = mixed.astype(BF16)

    rows = 64
    base = HALO - (CONV_WIDTH - 1)
    for r0 in range(0, tm, rows):
        for c0 in range(0, d_conv, LANES):
            acc = jnp.broadcast_to(conv_b_ref[:, c0:c0 + LANES], (rows, LANES))
            for j in range(CONV_WIDTH):
                acc = acc + conv_w_ref[j:j + 1, c0:c0 + LANES] * \
                    glu_ext[base + j + r0:base + j + r0 + rows, c0:c0 + LANES]
            y_ref[r0:r0 + rows, c0:c0 + LANES] = acc

    y = y_ref[...]
    mu = jnp.mean(y, axis=-1, keepdims=True)
    yc = y - mu
    var = jnp.mean(yc * yc, axis=-1, keepdims=True)
    ln = yc * lax.rsqrt(var + LN_EPS) * ln_g_ref[...] + ln_b_ref[...]
    mixed_ref[:, d_pool:] = jax.nn.silu(ln).astype(BF16)

    out = _dot(mixed_ref[...], w_out_ref[...])
    o_ref[0] = x + g_ref[0] * out


def _l0_mixer(x, sh, sc, g, w_in, pool_w, pool_scale, conv_w, conv_b, ln_g, ln_b, w_out, *, tm):
    B, L, D = x.shape
    d_pool = pool_scale.shape[0]
    d_conv = conv_b.shape[0]
    gd = d_pool // len(POOL_WINDOWS)
    const2 = lambda b, l: (0, 0)
    const3 = lambda b, l: (0, 0, 0)
    per_b = pl.BlockSpec((1, 1, D), lambda b, l: (b, 0, 0))
    kern = functools.partial(_l0_mixer_kernel, tm=tm, d_pool=d_pool, d_conv=d_conv, gd=gd)
    return pl.pallas_call(
        kern,
        grid=(B, L // tm),
        in_specs=[
            pl.BlockSpec((1, tm, D), lambda b, l: (b, l, 0)),
            per_b, per_b, per_b,
            pl.BlockSpec(w_in.shape, const2, pipeline_mode=pl.Buffered(1)),
            pl.BlockSpec(pool_w.shape, const3, pipeline_mode=pl.Buffered(1)),
            pl.BlockSpec((1, d_pool), const2),
            pl.BlockSpec((CONV_WIDTH, d_conv), const2),
            pl.BlockSpec((1, d_conv), const2),
            pl.BlockSpec((1, d_conv), const2),
            pl.BlockSpec((1, d_conv), const2),
            pl.BlockSpec(w_out.shape, const2, pipeline_mode=pl.Buffered(1)),
        ],
        out_specs=pl.BlockSpec((1, tm, D), lambda b, l: (b, l, 0)),
        out_shape=jax.ShapeDtypeStruct((B, L, D), F32),
        scratch_shapes=[
            pltpu.VMEM((HALO + tm, d_pool), F32),
            pltpu.VMEM((HALO + tm, d_conv), F32),
            pltpu.VMEM((tm, d_conv), F32),
            pltpu.VMEM((tm, d_pool + d_conv), BF16),
        ],
        compiler_params=_params(("arbitrary", "arbitrary")),
        name="l0_mixer",
    )(x, sh, sc, g, w_in, pool_w, pool_scale.reshape(1, d_pool), conv_w.reshape(CONV_WIDTH, d_conv),
      conv_b.reshape(1, d_conv), ln_g.reshape(1, d_conv), ln_b.reshape(1, d_conv), w_out)


def _l0_ffn_kernel(x_ref, sh_ref, sc_ref, g_ref, wg_ref, wu_ref, wd_ref, o_ref, h_ref):
    j = pl.program_id(1)

    @pl.when(j == 0)
    def _():
        h_ref[...] = _rms_modulate(x_ref[...], sh_ref[0], sc_ref[0]).astype(BF16)
        o_ref[...] = jnp.zeros_like(o_ref)

    h = h_ref[...]
    a = (jax.nn.silu(_dot(h, wg_ref[...])) * _dot(h, wu_ref[...])).astype(BF16)
    o_ref[...] += _dot(a, wd_ref[...])

    @pl.when(j == pl.num_programs(1) - 1)
    def _():
        o_ref[...] = x_ref[...] + g_ref[0] * o_ref[...]


def _l0_ffn(x2d, sh, sc, g, wg, wu, wd, *, seq, tm, tf):
    T, D = x2d.shape
    dff = wg.shape[1]
    per_b = pl.BlockSpec((1, 1, D), lambda i, j: (i // (seq // tm), 0, 0))
    return pl.pallas_call(
        _l0_ffn_kernel,
        grid=(T // tm, dff // tf),
        in_specs=[
            pl.BlockSpec((tm, D), lambda i, j: (i, 0)),
            per_b, per_b, per_b,
            pl.BlockSpec((D, tf), lambda i, j: (0, j)),
            pl.BlockSpec((D, tf), lambda i, j: (0, j)),
            pl.BlockSpec((tf, D), lambda i, j: (j, 0)),
        ],
        out_specs=pl.BlockSpec((tm, D), lambda i, j: (i, 0)),
        out_shape=jax.ShapeDtypeStruct((T, D), F32),
        scratch_shapes=[pltpu.VMEM((tm, D), BF16)],
        compiler_params=_params(("arbitrary", "arbitrary")),
        name="l0_ffn",
    )(x2d, sh, sc, g, wg, wu, wd)


def _l1_qkv_kernel(x_ref, sh_ref, sc_ref, pos_ref, inv_ref, qn_ref, kn_ref, bd_ref, w_ref,
                   q_ref, k_ref, v_ref, *, n_q, n_kv):
    x = x_ref[0]
    h = _rms_modulate(x, sh_ref[0], sc_ref[0]).astype(BF16)
    qkv = _dot(h, w_ref[...])

    ang = pos_ref[0].astype(F32) * inv_ref[...]
    cos = jnp.cos(ang)
    sin = jnp.sin(ang)
    lane = lax.broadcasted_iota(jnp.int32, (1, LANES), 1)
    first_half = (lane % HEAD_DIM) < (HEAD_DIM // 2)
    sin_signed = jnp.where(first_half, -sin, sin)
    low_head = lane < HEAD_DIM
    bd = bd_ref[...]

    def norm_rope(blk, nw):
        sq = blk * blk
        hi = sq.astype(BF16)
        lo = (sq - hi.astype(F32)).astype(BF16)
        ss = _dot(hi, bd) + _dot(lo, bd)
        n = blk * lax.rsqrt(ss * (1.0 / HEAD_DIM) + NORM_EPS) * nw
        partner = jnp.where(first_half, pltpu.roll(n, LANES - HEAD_DIM // 2, 1),
                            pltpu.roll(n, HEAD_DIM // 2, 1))
        return n * cos + partner * sin_signed

    def split_heads(blk):
        a_lo = jnp.where(low_head, blk, 0.0)
        b_hi = jnp.where(low_head, 0.0, blk)
        return (a_lo, pltpu.roll(a_lo, HEAD_DIM, 1), pltpu.roll(b_hi, HEAD_DIM, 1), b_hi)

    scale = HEAD_DIM ** -0.5
    for cb in range(n_q):
        blk = qkv[:, cb * LANES:(cb + 1) * LANES]
        q_ref[0, :, cb * LANES:(cb + 1) * LANES] = (norm_rope(blk, qn_ref[...]) * scale).astype(BF16)
    k0 = n_q * LANES
    v0 = k0 + n_kv * LANES
    for cb in range(n_kv):
        kr = norm_rope(qkv[:, k0 + cb * LANES:k0 + (cb + 1) * LANES], kn_ref[...])
        for i, part in enumerate(split_heads(kr)):
            k_ref[0, :, (4 * cb + i) * LANES:(4 * cb + i + 1) * LANES] = part.astype(BF16)
        vr = qkv[:, v0 + cb * LANES:v0 + (cb + 1) * LANES]
        for i, part in enumerate(split_heads(vr)):
            v_ref[0, :, (4 * cb + i) * LANES:(4 * cb + i + 1) * LANES] = part.astype(BF16)


def _l1_qkv(x, sh, sc, pos3, inv_t, qn_t, kn_t, bd, w_qkv, *, tm):
    B, L, D = x.shape
    n_heads = D // HEAD_DIM
    n_kvh = n_heads // Q_PER_KV
    n_q = n_heads * HEAD_DIM // LANES
    n_kv = n_kvh * HEAD_DIM // LANES
    kw = n_kvh * 2 * LANES
    const2 = lambda b, l: (0, 0)
    per_b = pl.BlockSpec((1, 1, D), lambda b, l: (b, 0, 0))
    row = lambda w: pl.BlockSpec((1, tm, w), lambda b, l: (b, l, 0))
    kern = functools.partial(_l1_qkv_kernel, n_q=n_q, n_kv=n_kv)
    return pl.pallas_call(
        kern,
        grid=(B, L // tm),
        in_specs=[
            row(D), per_b, per_b, row(1),
            pl.BlockSpec((1, LANES), const2),
            pl.BlockSpec((1, LANES), const2),
            pl.BlockSpec((1, LANES), const2),
            pl.BlockSpec((LANES, LANES), const2),
            pl.BlockSpec(w_qkv.shape, const2, pipeline_mode=pl.Buffered(1)),
        ],
        out_specs=[row(D), row(kw), row(kw)],
        out_shape=[jax.ShapeDtypeStruct((B, L, D), BF16),
                   jax.ShapeDtypeStruct((B, L, kw), BF16),
                   jax.ShapeDtypeStruct((B, L, kw), BF16)],
        compiler_params=_params(("arbitrary", "arbitrary")),
        name="l1_qkv",
    )(x, sh, sc, pos3, inv_t, qn_t, kn_t, bd, w_qkv)


def _l1_attn_kernel(sinks_ref, q_ref, kc_ref, kp_ref, vc_ref, vp_ref, x_ref, g_ref, wo_ref,
                    o_ref, attn_ref, *, tq, n_pairs):
    i = pl.program_id(1)
    gp = Q_PER_KV // 2
    rows = gp * WINDOW
    qi = lax.broadcasted_iota(jnp.int32, (rows, WINDOW), 0) % WINDOW
    kj = lax.broadcasted_iota(jnp.int32, (rows, WINDOW), 1)
    pair_of_row = lax.broadcasted_iota(jnp.int32, (rows, 1), 0) // WINDOW
    mask_cur = kj <= qi
    mask_prev_band = kj > qi

    for n in range(tq // WINDOW):
        r0 = n * WINDOW
        if n == 0:
            k_prev, v_prev = kp_ref[0], vp_ref[0]
            mask_prev = kj > qi + jnp.where(i > 0, 0, WINDOW)
        else:
            k_prev, v_prev = kc_ref[0, r0 - WINDOW:r0, :], vc_ref[0, r0 - WINDOW:r0, :]
            mask_prev = mask_prev_band
        k_cur, v_cur = kc_ref[0, r0:r0 + WINDOW, :], vc_ref[0, r0:r0 + WINDOW, :]
        for kv in range(n_pairs // gp):
            p0 = kv * gp
            qs = jnp.concatenate([q_ref[0, r0:r0 + WINDOW, (p0 + j) * LANES:(p0 + j + 1) * LANES]
                                  for j in range(gp)], axis=0)
            acc = jnp.zeros((rows, LANES), F32)
            for half in range(2):
                c0 = (2 * kv + half) * LANES
                s_p = jnp.where(mask_prev, _dot_nt(qs, k_prev[:, c0:c0 + LANES]), NEG_INF)
                s_c = jnp.where(mask_cur, _dot_nt(qs, k_cur[:, c0:c0 + LANES]), NEG_INF)
                sink = jnp.full((rows, 1), sinks_ref[2 * p0 + half], F32)
                for j in range(1, gp):
                    sink = jnp.where(pair_of_row == j, sinks_ref[2 * (p0 + j) + half], sink)
                m = jnp.maximum(jnp.maximum(jnp.max(s_p, axis=1, keepdims=True),
                                            jnp.max(s_c, axis=1, keepdims=True)), sink)
                e_p = jnp.exp(s_p - m)
                e_c = jnp.exp(s_c - m)
                denom = (jnp.sum(e_p, axis=1, keepdims=True) + jnp.sum(e_c, axis=1, keepdims=True)
                         + jnp.exp(sink - m))
                o_h = _dot(e_p.astype(BF16), v_prev[:, c0:c0 + LANES]) + \
                    _dot(e_c.astype(BF16), v_cur[:, c0:c0 + LANES])
                acc = acc + o_h * (1.0 / denom)
            for j in range(gp):
                attn_ref[r0:r0 + WINDOW, (p0 + j) * LANES:(p0 + j + 1) * LANES] = \
                    acc[j * WINDOW:(j + 1) * WINDOW].astype(BF16)

    out = _dot(attn_ref[...], wo_ref[...])
    o_ref[0] = x_ref[0] + g_ref[0] * out


def _l1_attn(sinks, q, k, v, x, g, w_o, *, tq):
    B, L, D = x.shape
    kw = k.shape[2]
    nb = tq // WINDOW
    cur = lambda w: pl.BlockSpec((1, tq, w), lambda b, i: (b, i, 0))
    prev = pl.BlockSpec((1, WINDOW, kw), lambda b, i: (b, jnp.maximum(i * nb - 1, 0), 0))
    kern = functools.partial(_l1_attn_kernel, tq=tq, n_pairs=D // LANES)
    return pl.pallas_call(
        kern,
        grid=(B, L // tq),
        in_specs=[
            pl.BlockSpec(memory_space=pltpu.SMEM),
            cur(D), cur(kw), prev, cur(kw), prev, cur(D),
            pl.BlockSpec((1, 1, D), lambda b, i: (b, 0, 0)),
            pl.BlockSpec(w_o.shape, lambda b, i: (0, 0), pipeline_mode=pl.Buffered(1)),
        ],
        out_specs=cur(D),
        out_shape=jax.ShapeDtypeStruct((B, L, D), F32),
        scratch_shapes=[pltpu.VMEM((tq, D), BF16)],
        compiler_params=_params(("arbitrary", "arbitrary")),
        name="l1_attn",
    )(sinks, q, k, k, v, v, x, g, w_o)


def _l1_router_kernel(x_ref, sh_ref, sc_ref, rw_ref, h_ref, meta_ref, gate_ref, cnt_ref, carry,
                      *, tm):
    i = pl.program_id(0)

    @pl.when(i == 0)
    def _():
        carry[...] = jnp.zeros_like(carry)

    h = _rms_modulate(x_ref[...], sh_ref[0], sc_ref[0])
    h_ref[...] = h

    rw = rw_ref[...]
    h_hi = h.astype(BF16)
    h_lo = (h - h_hi.astype(F32)).astype(BF16)
    w_hi = rw.astype(BF16)
    w_lo = (rw - w_hi.astype(F32)).astype(BF16)
    logits = _dot_nt(w_hi, h_hi) + (_dot_nt(w_hi, h_lo) + _dot_nt(w_lo, h_hi))

    eidx = lax.broadcasted_iota(jnp.int32, (N_EXPERTS, tm), 0)
    m1 = jnp.max(logits, axis=0, keepdims=True)
    i1 = jnp.min(jnp.where(logits == m1, eidx, N_EXPERTS), axis=0, keepdims=True)
    rest = jnp.where(eidx == i1, NEG_INF, logits)
    m2 = jnp.max(rest, axis=0, keepdims=True)
    i2 = jnp.min(jnp.where(rest == m2, eidx, N_EXPERTS), axis=0, keepdims=True)
    e2 = jnp.exp(m2 - m1)
    gate1 = 1.0 / (1.0 + e2)
    gate2 = e2 / (1.0 + e2)

    sel1 = eidx == i1
    sel2 = eidx == i2
    ind = (sel1 | sel2).astype(F32)
    before = lax.broadcasted_iota(jnp.int32, (tm, tm), 0) < lax.broadcasted_iota(jnp.int32, (tm, tm), 1)
    excl = _dot(ind.astype(BF16), before.astype(BF16)) + carry[...]
    r1 = jnp.sum(jnp.where(sel1, excl, 0.0), axis=0, keepdims=True).astype(jnp.int32)
    r2 = jnp.sum(jnp.where(sel2, excl, 0.0), axis=0, keepdims=True).astype(jnp.int32)
    carry[...] = carry[...] + jnp.sum(ind, axis=1, keepdims=True)
    cnt_ref[...] = jnp.broadcast_to(carry[...], cnt_ref.shape).astype(jnp.int32)

    meta_ref[...] = jnp.where(eidx == 0, i1, jnp.where(eidx == 1, i2, jnp.where(eidx == 2, r1,
                              jnp.where(eidx == 3, r2, 0))))
    gate_ref[...] = jnp.where(eidx == 0, gate1, jnp.where(eidx == 1, gate2, 0.0))


def _l1_router(x2d, sh, sc, rw_t, *, seq, tm):
    T, D = x2d.shape
    per_b = pl.BlockSpec((1, 1, D), lambda i: (i // (seq // tm), 0, 0))
    kern = functools.partial(_l1_router_kernel, tm=tm)
    return pl.pallas_call(
        kern,
        grid=(T // tm,),
        in_specs=[
            pl.BlockSpec((tm, D), lambda i: (i, 0)),
            per_b, per_b,
            pl.BlockSpec((N_EXPERTS, D), lambda i: (0, 0)),
        ],
        out_specs=[
            pl.BlockSpec((tm, D), lambda i: (i, 0)),
            pl.BlockSpec((N_EXPERTS, tm), lambda i: (0, i)),
            pl.BlockSpec((N_EXPERTS, tm), lambda i: (0, i)),
            pl.BlockSpec((N_EXPERTS, LANES), lambda i: (0, 0)),
        ],
        out_shape=[
            jax.ShapeDtypeStruct((T, D), F32),
            jax.ShapeDtypeStruct((N_EXPERTS, T), jnp.int32),
            jax.ShapeDtypeStruct((N_EXPERTS, T), F32),
            jax.ShapeDtypeStruct((N_EXPERTS, LANES), jnp.int32),
        ],
        scratch_shapes=[pltpu.VMEM((N_EXPERTS, 1), F32)],
        compiler_params=_params(("arbitrary",)),
        name="l1_router",
    )(x2d, sh, sc, rw_t)


def _dispatch_kernel(pstart_ref, fill_ref, meta_ref, h_ref, zeros_ref, xb_ref, sem, zsem, *, td):
    i = pl.program_id(0)

    @pl.when(i == 0)
    def _():
        for e in range(N_EXPERTS):
            start = pl.multiple_of((fill_ref[e] // SUBLANES) * SUBLANES, SUBLANES)
            cp = pltpu.make_async_copy(zeros_ref, xb_ref.at[pl.ds(start, MOE_BLOCK + SUBLANES)], zsem)
            cp.start()
            cp.wait()

        def fill_tail(b, carry):
            start = pl.multiple_of(b * MOE_BLOCK, MOE_BLOCK)
            cp = pltpu.make_async_copy(zeros_ref.at[pl.ds(0, MOE_BLOCK)],
                                       xb_ref.at[pl.ds(start, MOE_BLOCK)], zsem)
            cp.start()
            cp.wait()
            return carry

        lax.fori_loop(fill_ref[N_EXPERTS] // MOE_BLOCK, xb_ref.shape[0] // MOE_BLOCK, fill_tail, 0)

    def row_copy(t, dst):
        return pltpu.make_async_copy(h_ref.at[pl.ds(t, 1)], xb_ref.at[pl.ds(dst, 1)], sem)

    def issue(t, carry):
        for k in range(TOP_K):
            dst = pstart_ref[meta_ref[k, t]] + meta_ref[TOP_K + k, t]
            row_copy(t, dst).start()
        return carry

    lax.fori_loop(0, td, issue, 0)

    def drain(t, carry):
        for k in range(TOP_K):
            row_copy(0, 0).wait()
        return carry

    lax.fori_loop(0, td, drain, 0)


def _dispatch(pstart, fill, meta, h, zeros_blk, *, n_rows, td):
    T, D = h.shape
    kern = functools.partial(_dispatch_kernel, td=td)
    return pl.pallas_call(
        kern,
        grid_spec=pltpu.PrefetchScalarGridSpec(
            num_scalar_prefetch=2,
            grid=(T // td,),
            in_specs=[
                pl.BlockSpec((N_EXPERTS, td), lambda i, *_: (0, i), memory_space=pltpu.SMEM),
                pl.BlockSpec((td, D), lambda i, *_: (i, 0)),
                pl.BlockSpec(memory_space=pl.ANY),
            ],
            out_specs=pl.BlockSpec(memory_space=pl.ANY),
            scratch_shapes=[pltpu.SemaphoreType.DMA(()), pltpu.SemaphoreType.DMA(())],
        ),
        out_shape=jax.ShapeDtypeStruct((n_rows, D), F32),
        compiler_params=_params(("arbitrary",)),
        name="dispatch",
    )(pstart, fill, meta, h, zeros_blk)


def _experts_kernel(blk_e_ref, nused_ref, x_ref, wg_ref, wu_ref, wd_ref, o_ref, xs_ref):
    b = pl.program_id(0)
    j = pl.program_id(1)

    @pl.when(b < nused_ref[0])
    def _():
        @pl.when(j == 0)
        def _():
            xs_ref[...] = x_ref[...].astype(BF16)
            o_ref[...] = jnp.zeros_like(o_ref)

        xs = xs_ref[...]
        a = (jax.nn.silu(_dot(xs, wg_ref[...])) * _dot(xs, wu_ref[...])).astype(BF16)
        o_ref[...] += _dot(a, wd_ref[...])

    @pl.when((b >= nused_ref[0]) & (j == 0))
    def _():
        o_ref[...] = jnp.zeros_like(o_ref)


def _experts(blk_e, nused, xb, wg, wu, wd, *, n_blocks, tf):
    D = xb.shape[1]
    dff = wg.shape[2]
    nj = dff // tf

    def row_map(b, j, be, nu):
        return (jnp.minimum(b, nu[0] - 1), 0)

    def jj(b, j, nu):
        return jnp.where(b < nu[0], j, nj - 1)

    return pl.pallas_call(
        _experts_kernel,
        grid_spec=pltpu.PrefetchScalarGridSpec(
            num_scalar_prefetch=2,
            grid=(n_blocks, nj),
            in_specs=[
                pl.BlockSpec((MOE_BLOCK, D), row_map),
                pl.BlockSpec((None, D, tf), lambda b, j, be, nu: (be[b], 0, jj(b, j, nu))),
                pl.BlockSpec((None, D, tf), lambda b, j, be, nu: (be[b], 0, jj(b, j, nu))),
                pl.BlockSpec((None, tf, D), lambda b, j, be, nu: (be[b], jj(b, j, nu), 0)),
            ],
            out_specs=pl.BlockSpec((MOE_BLOCK, D), lambda b, j, be, nu: (b, 0)),
            scratch_shapes=[pltpu.VMEM((MOE_BLOCK, D), BF16)],
        ),
        out_shape=jax.ShapeDtypeStruct((n_blocks * MOE_BLOCK, D), F32),
        compiler_params=_params(("arbitrary", "arbitrary")),
        name="experts",
    )(blk_e, nused, xb, wg, wu, wd)


def _combine_kernel(pstart_ref, meta_ref, gate_ref, x_ref, g_ref, yb_ref, o_ref, buf, sem, *, tc):
    def row_copy(k, t, src):
        return pltpu.make_async_copy(yb_ref.at[pl.ds(src, 1)], buf.at[k, pl.ds(t, 1)], sem)

    def issue(t, carry):
        for k in range(TOP_K):
            src = pstart_ref[meta_ref[k, t]] + meta_ref[TOP_K + k, t]
            row_copy(k, t, src).start()
        return carry

    lax.fori_loop(0, tc, issue, 0)

    def drain(t, carry):
        for k in range(TOP_K):
            row_copy(k, 0, 0).wait()
        return carry

    lax.fori_loop(0, tc, drain, 0)

    gates = gate_ref[...]
    moe = gates[:, 0:1] * buf[0] + gates[:, 1:2] * buf[1]
    o_ref[...] = x_ref[...] + g_ref[0] * moe


def _combine(pstart, meta, gates_t, x2d, g, yb, *, seq, tc):
    T, D = x2d.shape
    kern = functools.partial(_combine_kernel, tc=tc)
    return pl.pallas_call(
        kern,
        grid_spec=pltpu.PrefetchScalarGridSpec(
            num_scalar_prefetch=1,
            grid=(T // tc,),
            in_specs=[
                pl.BlockSpec((N_EXPERTS, tc), lambda i, *_: (0, i), memory_space=pltpu.SMEM),
                pl.BlockSpec((tc, N_EXPERTS), lambda i, *_: (i, 0)),
                pl.BlockSpec((tc, D), lambda i, *_: (i, 0)),
                pl.BlockSpec((1, 1, D), lambda i, *_: (i // (seq // tc), 0, 0)),
                pl.BlockSpec(memory_space=pl.ANY),
            ],
            out_specs=pl.BlockSpec((tc, D), lambda i, *_: (i, 0)),
            scratch_shapes=[pltpu.VMEM((TOP_K, tc, D), F32), pltpu.SemaphoreType.DMA(())],
        ),
        out_shape=jax.ShapeDtypeStruct((T, D), F32),
        compiler_params=_params(("arbitrary",)),
        name="combine",
    )(pstart, meta, gates_t, x2d, g, yb)


def _mod_params(c, w_mod, b_mod):
    B, D = c.shape
    c_pad = jnp.zeros((SUBLANES, D), F32).at[:B].set(c)
    mod = _adaln(c_pad, w_mod, b_mod)[:B]
    return [m.reshape(B, 1, D) for m in jnp.split(mod, 6, axis=-1)]


def kernel(x, c, positions, l0_w_mod, l0_b_mod, l0_w_in, l0_pool_w, l0_pool_scale, l0_conv_w, l0_conv_b, l0_conv_ln_g, l0_conv_ln_b, l0_w_out, l0_ffn_w_gate, l0_ffn_w_up, l0_ffn_w_down, l1_w_mod, l1_b_mod, l1_w_qkv, l1_q_norm, l1_k_norm, l1_sinks, l1_w_o, l1_router_w, l1_exp_w_gate, l1_exp_w_up, l1_exp_w_down):
    B, L, D = x.shape
    T = B * L
    bf = lambda w: w.astype(BF16)

    sh1, sc1, g1, sh2, sc2, g2 = _mod_params(c, l0_w_mod, l0_b_mod)
    x = _l0_mixer(x, sh1, sc1, g1, bf(l0_w_in), bf(l0_pool_w), l0_pool_scale, l0_conv_w, l0_conv_b,
                  l0_conv_ln_g, l0_conv_ln_b, bf(l0_w_out), tm=min(256, L))
    x = _l0_ffn(x.reshape(T, D), sh2, sc2, g2, bf(l0_ffn_w_gate), bf(l0_ffn_w_up), bf(l0_ffn_w_down),
                seq=L, tm=min(512, L), tf=512 if l0_ffn_w_gate.shape[1] % 512 == 0 else 256)
    x = x.reshape(B, L, D)

    sh1, sc1, g1, sh2, sc2, g2 = _mod_params(c, l1_w_mod, l1_b_mod)
    half = HEAD_DIM // 2
    inv = ROPE_THETA ** (-jnp.arange(half, dtype=F32) / half)
    inv_t = jnp.tile(inv, LANES // half).reshape(1, LANES)
    qn_t = jnp.tile(l1_q_norm, LANES // HEAD_DIM).reshape(1, LANES)
    kn_t = jnp.tile(l1_k_norm, LANES // HEAD_DIM).reshape(1, LANES)
    lane = jnp.arange(LANES)
    bd = (lane[:, None] // HEAD_DIM == lane[None, :] // HEAD_DIM).astype(BF16)
    q, k, v = _l1_qkv(x, sh1, sc1, positions.reshape(B, L, 1), inv_t, qn_t, kn_t, bd, bf(l1_w_qkv),
                      tm=min(256, L))
    x = _l1_attn(l1_sinks, q, k, v, x, g1, bf(l1_w_o), tq=min(256, L))

    x2d = x.reshape(T, D)
    h, meta, gates, cnt = _l1_router(x2d, sh2, sc2, l1_router_w.T, seq=L, tm=min(512, L))
    counts = cnt[:, 0]
    padded = ((counts + MOE_BLOCK - 1) // MOE_BLOCK) * MOE_BLOCK
    pend = jnp.cumsum(padded)
    pstart = pend - padded
    n_blocks = (T * TOP_K + N_EXPERTS * (MOE_BLOCK - 1) + MOE_BLOCK - 1) // MOE_BLOCK
    nused = (pend[-1] // MOE_BLOCK).astype(jnp.int32).reshape(1)
    blk_start = jnp.minimum(jnp.arange(n_blocks, dtype=jnp.int32), nused[0] - 1) * MOE_BLOCK
    blk_e = jnp.minimum(jnp.searchsorted(pend, blk_start, side='right'), N_EXPERTS - 1).astype(jnp.int32)
    fill = jnp.concatenate([pstart + counts, pend[-1:]]).astype(jnp.int32)
    xb = _dispatch(pstart.astype(jnp.int32), fill, meta, h,
                   jnp.zeros((MOE_BLOCK + SUBLANES, D), F32), n_rows=(n_blocks + 2) * MOE_BLOCK,
                   td=min(512, T))
    yb = _experts(blk_e, nused, xb, bf(l1_exp_w_gate), bf(l1_exp_w_up), bf(l1_exp_w_down),
                  n_blocks=n_blocks, tf=512)
    out = _combine(pstart.astype(jnp.int32), meta, gates.T, x2d, g2, yb, seq=L, tc=min(256, L))
    return out.reshape(B, L, D)
```

```python
import functools

import jax
import jax.numpy as jnp
from jax import lax
from jax.experimental import pallas as pl
from jax.experimental.pallas import tpu as pltpu

F32 = jnp.float32
BF16 = jnp.bfloat16

HEAD_DIM = 64
Q_PER_KV = 8
POOL_WINDOWS = (2, 4, 8, 16)
CONV_WIDTH = 31
WINDOW = 128
ROPE_THETA = 10000.0
N_EXPERTS = 8
TOP_K = 2
NORM_EPS = 1e-6
LN_EPS = 1e-5

LANES = 128
SUBLANES = 8
HALO = 32
MOE_BLOCK = 512
VMEM_LIMIT = 56 * 1024 * 1024
NEG_INF = float("-inf")


def _params(sem, vmem=VMEM_LIMIT):
    return pltpu.CompilerParams(dimension_semantics=sem, vmem_limit_bytes=vmem)


def _rms_modulate(x, shift, scale):
    ms = jnp.mean(x * x, axis=-1, keepdims=True)
    return x * lax.rsqrt(ms + NORM_EPS) * (1.0 + scale) + shift


def _dot(a, b):
    return jnp.dot(a, b, preferred_element_type=F32)


def _dot_nt(a, b):
    return lax.dot_general(a, b, (((1,), (1,)), ((), ())), preferred_element_type=F32)


def _adaln_kernel(c_ref, w_ref, b_ref, o_ref):
    sc = jax.nn.silu(c_ref[...]).astype(BF16)
    o_ref[...] = _dot(sc, w_ref[...].astype(BF16)) + b_ref[...]


def _adaln(c_pad, w_mod, b_mod):
    d, n = w_mod.shape
    tn = 1024
    return pl.pallas_call(
        _adaln_kernel,
        grid=(n // tn,),
        in_specs=[
            pl.BlockSpec((SUBLANES, d), lambda j: (0, 0)),
            pl.BlockSpec((d, tn), lambda j: (0, j)),
            pl.BlockSpec((1, tn), lambda j: (0, j)),
        ],
        out_specs=pl.BlockSpec((SUBLANES, tn), lambda j: (0, j)),
        out_shape=jax.ShapeDtypeStruct((SUBLANES, n), F32),
        compiler_params=_params(("arbitrary",)),
        name="adaln",
    )(c_pad, w_mod, b_mod.reshape(1, n))


def _l0_mixer_kernel(x_ref, sh_ref, sc_ref, g_ref, w_in_ref, pool_w_ref, pool_scale_ref,
                     conv_w_ref, conv_b_ref, ln_g_ref, ln_b_ref, w_out_ref, o_ref,
                     u_ext, glu_ext, y_ref, mixed_ref, *, tm, d_pool, d_conv, gd):
    l = pl.program_id(1)

    @pl.when(l == 0)
    def _():
        u_ext[0:HALO, :] = jnp.zeros((HALO, d_pool), F32)
        glu_ext[0:HALO, :] = jnp.zeros((HALO, d_conv), F32)

    @pl.when(l > 0)
    def _():
        u_ext[0:HALO, :] = u_ext[tm:tm + HALO, :]
        glu_ext[0:HALO, :] = glu_ext[tm:tm + HALO, :]

    x = x_ref[0]
    h = _rms_modulate(x, sh_ref[0], sc_ref[0]).astype(BF16)
    z = _dot(h, w_in_ref[...])
    u_ext[HALO:HALO + tm, :] = z[:, :d_pool]
    glu_ext[HALO:HALO + tm, :] = z[:, d_pool:d_pool + d_conv] * jax.nn.sigmoid(z[:, d_pool + d_conv:])

    t1 = l * tm + lax.broadcasted_iota(jnp.int32, (tm, 1), 0) + 1
    for g, w in enumerate(POOL_WINDOWS):
        c0 = g * gd
        tok = u_ext[HALO:HALO + tm, c0:c0 + gd]
        s = tok
        for k in range(1, w):
            s = s + u_ext[HALO - k:HALO - k + tm, c0:c0 + gd]
        inv_cnt = 1.0 / jnp.minimum(t1, w).astype(F32)
        pooled = s * inv_cnt - tok
        mixed = _dot(pooled.astype(BF16), pool_w_ref[g]) * pool_scale_ref[:, c0:c0 + gd]
        mixed_ref[:, c0:c0 + gd] = mixed.astype(BF16)

    rows = 64
    base = HALO - (CONV_WIDTH - 1)
    for r0 in range(0, tm, rows):
        for c0 in range(0, d_conv, LANES):
            acc = jnp.broadcast_to(conv_b_ref[:, c0:c0 + LANES], (rows, LANES))
            for j in range(CONV_WIDTH):
                acc = acc + conv_w_ref[j:j + 1, c0:c0 + LANES] * \
                    glu_ext[base + j + r0:base + j + r0 + rows, c0:c0 + LANES]
            y_ref[r0:r0 + rows, c0:c0 + LANES] = acc

    y = y_ref[...]
    mu = jnp.mean(y, axis=-1, keepdims=True)
    yc = y - mu
    var = jnp.mean(yc * yc, axis=-1, keepdims=True)
    ln = yc * lax.rsqrt(var + LN_EPS) * ln_g_ref[...] + ln_b_ref[...]
    mixed_ref[:, d_pool:] = jax.nn.silu(ln).astype(BF16)

    out = _dot(mixed_ref[...], w_out_ref[...])
    o_ref[0] = x + g_ref[0] * out


def _l0_mixer(x, sh, sc, g, w_in, pool_w, pool_scale, conv_w, conv_b, ln_g, ln_b, w_out, *, tm):
    B, L, D = x.shape
    d_pool = pool_scale.shape[0]
    d_conv = conv_b.shape[0]
    gd = d_pool // len(POOL_WINDOWS)
    const2 = lambda b, l: (0, 0)
    const3 = lambda b, l: (0, 0, 0)
    per_b = pl.BlockSpec((1, 1, D), lambda b, l: (b, 0, 0))
    kern = functools.partial(_l0_mixer_kernel, tm=tm, d_pool=d_pool, d_conv=d_conv, gd=gd)
    return pl.pallas_call(
        kern,
        grid=(B, L // tm),
        in_specs=[
            pl.BlockSpec((1, tm, D), lambda b, l: (b, l, 0)),
            per_b, per_b, per_b,
            pl.BlockSpec(w_in.shape, const2, pipeline_mode=pl.Buffered(1)),
            pl.BlockSpec(pool_w.shape, const3, pipeline_mode=pl.Buffered(1)),
            pl.BlockSpec((1, d_pool), const2),
            pl.BlockSpec((CONV_WIDTH, d_conv), const2),
            pl.BlockSpec((1, d_conv), const2),
            pl.BlockSpec((1, d_conv), const2),
            pl.BlockSpec((1, d_conv), const2),
            pl.BlockSpec(w_out.shape, const2, pipeline_mode=pl.Buffered(1)),
        ],
        out_specs=pl.BlockSpec((1, tm, D), lambda b, l: (b, l, 0)),
        out_shape=jax.ShapeDtypeStruct((B, L, D), F32),
        scratch_shapes=[
            pltpu.VMEM((HALO + tm, d_pool), F32),
            pltpu.VMEM((HALO + tm, d_conv), F32),
            pltpu.VMEM((tm, d_conv), F32),
            pltpu.VMEM((tm, d_pool + d_conv), BF16),
        ],
        compiler_params=_params(("arbitrary", "arbitrary")),
        name="l0_mixer",
    )(x, sh, sc, g, w_in, pool_w, pool_scale.reshape(1, d_pool), conv_w.reshape(CONV_WIDTH, d_conv),
      conv_b.reshape(1, d_conv), ln_g.reshape(1, d_conv), ln_b.reshape(1, d_conv), w_out)


def _l0_ffn_kernel(x_ref, sh_ref, sc_ref, g_ref, wg_ref, wu_ref, wd_ref, o_ref, h_ref):
    j = pl.program_id(1)

    @pl.when(j == 0)
    def _():
        h_ref[...] = _rms_modulate(x_ref[...], sh_ref[0], sc_ref[0]).astype(BF16)
        o_ref[...] = jnp.zeros_like(o_ref)

    h = h_ref[...]
    a = (jax.nn.silu(_dot(h, wg_ref[...])) * _dot(h, wu_ref[...])).astype(BF16)
    o_ref[...] += _dot(a, wd_ref[...])

    @pl.when(j == pl.num_programs(1) - 1)
    def _():
        o_ref[...] = x_ref[...] + g_ref[0] * o_ref[...]


def _l0_ffn(x2d, sh, sc, g, wg, wu, wd, *, seq, tm, tf):
    T, D = x2d.shape
    dff = wg.shape[1]
    per_b = pl.BlockSpec((1, 1, D), lambda i, j: (i // (seq // tm), 0, 0))
    return pl.pallas_call(
        _l0_ffn_kernel,
        grid=(T // tm, dff // tf),
        in_specs=[
            pl.BlockSpec((tm, D), lambda i, j: (i, 0)),
            per_b, per_b, per_b,
            pl.BlockSpec((D, tf), lambda i, j: (0, j)),
            pl.BlockSpec((D, tf), lambda i, j: (0, j)),
            pl.BlockSpec((tf, D), lambda i, j: (j, 0)),
        ],
        out_specs=pl.BlockSpec((tm, D), lambda i, j: (i, 0)),
        out_shape=jax.ShapeDtypeStruct((T, D), F32),
        scratch_shapes=[pltpu.VMEM((tm, D), BF16)],
        compiler_params=_params(("arbitrary", "arbitrary")),
        name="l0_ffn",
    )(x2d, sh, sc, g, wg, wu, wd)


def _l1_qkv_kernel(x_ref, sh_ref, sc_ref, pos_ref, inv_ref, qn_ref, kn_ref, bd_ref, w_ref,
                   q_ref, k_ref, v_ref, *, n_q, n_kv):
    x = x_ref[0]
    h = _rms_modulate(x, sh_ref[0], sc_ref[0]).astype(BF16)
    qkv = _dot(h, w_ref[...])

    ang = pos_ref[0].astype(F32) * inv_ref[...]
    cos = jnp.cos(ang)
    sin = jnp.sin(ang)
    lane = lax.broadcasted_iota(jnp.int32, (1, LANES), 1)
    first_half = (lane % HEAD_DIM) < (HEAD_DIM // 2)
    sin_signed = jnp.where(first_half, -sin, sin)
    low_head = lane < HEAD_DIM
    bd = bd_ref[...]

    def norm_rope(blk, nw):
        sq = blk * blk
        hi = sq.astype(BF16)
        lo = (sq - hi.astype(F32)).astype(BF16)
        ss = _dot(hi, bd) + _dot(lo, bd)
        n = blk * lax.rsqrt(ss * (1.0 / HEAD_DIM) + NORM_EPS) * nw
        partner = jnp.where(first_half, pltpu.roll(n, LANES - HEAD_DIM // 2, 1),
                            pltpu.roll(n, HEAD_DIM // 2, 1))
        return n * cos + partner * sin_signed

    def split_heads(blk):
        a_lo = jnp.where(low_head, blk, 0.0)
        b_hi = jnp.where(low_head, 0.0, blk)
        return (a_lo, pltpu.roll(a_lo, HEAD_DIM, 1), pltpu.roll(b_hi, HEAD_DIM, 1), b_hi)

    scale = HEAD_DIM ** -0.5
    for cb in range(n_q):
        blk = qkv[:, cb * LANES:(cb + 1) * LANES]
        q_ref[0, :, cb * LANES:(cb + 1) * LANES] = (norm_rope(blk, qn_ref[...]) * scale).astype(BF16)
    k0 = n_q * LANES
    v0 = k0 + n_kv * LANES
    for cb in range(n_kv):
        kr = norm_rope(qkv[:, k0 + cb * LANES:k0 + (cb + 1) * LANES], kn_ref[...])
        for i, part in enumerate(split_heads(kr)):
            k_ref[0, :, (4 * cb + i) * LANES:(4 * cb + i + 1) * LANES] = part.astype(BF16)
        vr = qkv[:, v0 + cb * LANES:v0 + (cb + 1) * LANES]
        for i, part in enumerate(split_heads(vr)):
            v_ref[0, :, (4 * cb + i) * LANES:(4 * cb + i + 1) * LANES] = part.astype(BF16)


def _l1_qkv(x, sh, sc, pos3, inv_t, qn_t, kn_t, bd, w_qkv, *, tm):
    B, L, D = x.shape
    n_heads = D // HEAD_DIM
    n_kvh = n_heads // Q_PER_KV
    n_q = n_heads * HEAD_DIM // LANES
    n_kv = n_kvh * HEAD_DIM // LANES
    kw = n_kvh * 2 * LANES
    const2 = lambda b, l: (0, 0)
    per_b = pl.BlockSpec((1, 1, D), lambda b, l: (b, 0, 0))
    row = lambda w: pl.BlockSpec((1, tm, w), lambda b, l: (b, l, 0))
    kern = functools.partial(_l1_qkv_kernel, n_q=n_q, n_kv=n_kv)
    return pl.pallas_call(
        kern,
        grid=(B, L // tm),
        in_specs=[
            row(D), per_b, per_b, row(1),
            pl.BlockSpec((1, LANES), const2),
            pl.BlockSpec((1, LANES), const2),
            pl.BlockSpec((1, LANES), const2),
            pl.BlockSpec((LANES, LANES), const2),
            pl.BlockSpec(w_qkv.shape, const2, pipeline_mode=pl.Buffered(1)),
        ],
        out_specs=[row(D), row(kw), row(kw)],
        out_shape=[jax.ShapeDtypeStruct((B, L, D), BF16),
                   jax.ShapeDtypeStruct((B, L, kw), BF16),
                   jax.ShapeDtypeStruct((B, L, kw), BF16)],
        compiler_params=_params(("arbitrary", "arbitrary")),
        name="l1_qkv",
    )(x, sh, sc, pos3, inv_t, qn_t, kn_t, bd, w_qkv)


def _l1_attn_kernel(sinks_ref, q_ref, kc_ref, kp_ref, vc_ref, vp_ref, x_ref, g_ref, wo_ref,
                    o_ref, attn_ref, *, tq, n_pairs):
    i = pl.program_id(1)
    gp = Q_PER_KV // 2
    rows = gp * WINDOW
    qi = lax.broadcasted_iota(jnp.int32, (rows, WINDOW), 0) % WINDOW
    kj = lax.broadcasted_iota(jnp.int32, (rows, WINDOW), 1)
    pair_of_row = lax.broadcasted_iota(jnp.int32, (rows, 1), 0) // WINDOW
    mask_cur = kj <= qi
    mask_prev_band = kj > qi

    for n in range(tq // WINDOW):
        r0 = n * WINDOW
        if n == 0:
            k_prev, v_prev = kp_ref[0], vp_ref[0]
            mask_prev = kj > qi + jnp.where(i > 0, 0, WINDOW)
        else:
            k_prev, v_prev = kc_ref[0, r0 - WINDOW:r0, :], vc_ref[0, r0 - WINDOW:r0, :]
            mask_prev = mask_prev_band
        k_cur, v_cur = kc_ref[0, r0:r0 + WINDOW, :], vc_ref[0, r0:r0 + WINDOW, :]
        for kv in range(n_pairs // gp):
            p0 = kv * gp
            qs = jnp.concatenate([q_ref[0, r0:r0 + WINDOW, (p0 + j) * LANES:(p0 + j + 1) * LANES]
                                  for j in range(gp)], axis=0)
            acc = jnp.zeros((rows, LANES), F32)
            for half in range(2):
                c0 = (2 * kv + half) * LANES
                s_p = jnp.where(mask_prev, _dot_nt(qs, k_prev[:, c0:c0 + LANES]), NEG_INF)
                s_c = jnp.where(mask_cur, _dot_nt(qs, k_cur[:, c0:c0 + LANES]), NEG_INF)
                sink = jnp.full((rows, 1), sinks_ref[2 * p0 + half], F32)
                for j in range(1, gp):
                    sink = jnp.where(pair_of_row == j, sinks_ref[2 * (p0 + j) + half], sink)
                m = jnp.maximum(jnp.maximum(jnp.max(s_p, axis=1, keepdims=True),
                                            jnp.max(s_c, axis=1, keepdims=True)), sink)
                e_p = jnp.exp(s_p - m)
                e_c = jnp.exp(s_c - m)
                denom = (jnp.sum(e_p, axis=1, keepdims=True) + jnp.sum(e_c, axis=1, keepdims=True)
                         + jnp.exp(sink - m))
                o_h = _dot(e_p.astype(BF16), v_prev[:, c0:c0 + LANES]) + \
                    _dot(e_c.astype(BF16), v_cur[:, c0:c0 + LANES])
                acc = acc + o_h * (1.0 / denom)
            for j in range(gp):
                attn_ref[r0:r0 + WINDOW, (p0 + j) * LANES:(p0 + j + 1) * LANES] = \
                    acc[j * WINDOW:(j + 1) * WINDOW].astype(BF16)

    out = _dot(attn_ref[...], wo_ref[...])
    o_ref[0] = x_ref[0] + g_ref[0] * out


def _l1_attn(sinks, q, k, v, x, g, w_o, *, tq):
    B, L, D = x.shape
    kw = k.shape[2]
    nb = tq // WINDOW
    cur = lambda w: pl.BlockSpec((1, tq, w), lambda b, i: (b, i, 0))
    prev = pl.BlockSpec((1, WINDOW, kw), lambda b, i: (b, jnp.maximum(i * nb - 1, 0), 0))
    kern = functools.partial(_l1_attn_kernel, tq=tq, n_pairs=D // LANES)
    return pl.pallas_call(
        kern,
        grid=(B, L // tq),
        in_specs=[
            pl.BlockSpec(memory_space=pltpu.SMEM),
            cur(D), cur(kw), prev, cur(kw), prev, cur(D),
            pl.BlockSpec((1, 1, D), lambda b, i: (b, 0, 0)),
            pl.BlockSpec(w_o.shape, lambda b, i: (0, 0), pipeline_mode=pl.Buffered(1)),
        ],
        out_specs=cur(D),
        out_shape=jax.ShapeDtypeStruct((B, L, D), F32),
        scratch_shapes=[pltpu.VMEM((tq, D), BF16)],
        compiler_params=_params(("arbitrary", "arbitrary")),
        name="l1_attn",
    )(sinks, q, k, k, v, v, x, g, w_o)


def _l1_router_kernel(x_ref, sh_ref, sc_ref, rw_ref, h_ref, meta_ref, gate_ref, cnt_ref, carry,
                      *, tm):
    i = pl.program_id(0)

    @pl.when(i == 0)
    def _():
        carry[...] = jnp.zeros_like(carry)

    h = _rms_modulate(x_ref[...], sh_ref[0], sc_ref[0])
    h_ref[...] = h

    rw = rw_ref[...]
    h_hi = h.astype(BF16)
    h_lo = (h - h_hi.astype(F32)).astype(BF16)
    w_hi = rw.astype(BF16)
    w_lo = (rw - w_hi.astype(F32)).astype(BF16)
    logits = _dot_nt(w_hi, h_hi) + (_dot_nt(w_hi, h_lo) + _dot_nt(w_lo, h_hi))

    eidx = lax.broadcasted_iota(jnp.int32, (N_EXPERTS, tm), 0)
    m1 = jnp.max(logits, axis=0, keepdims=True)
    i1 = jnp.min(jnp.where(logits == m1, eidx, N_EXPERTS), axis=0, keepdims=True)
    rest = jnp.where(eidx == i1, NEG_INF, logits)
    m2 = jnp.max(rest, axis=0, keepdims=True)
    i2 = jnp.min(jnp.where(rest == m2, eidx, N_EXPERTS), axis=0, keepdims=True)
    e2 = jnp.exp(m2 - m1)
    gate1 = 1.0 / (1.0 + e2)
    gate2 = e2 / (1.0 + e2)

    sel1 = eidx == i1
    sel2 = eidx == i2
    ind = (sel1 | sel2).astype(F32)
    before = lax.broadcasted_iota(jnp.int32, (tm, tm), 0) < lax.broadcasted_iota(jnp.int32, (tm, tm), 1)
    excl = _dot(ind.astype(BF16), before.astype(BF16)) + carry[...]
    r1 = jnp.sum(jnp.where(sel1, excl, 0.0), axis=0, keepdims=True).astype(jnp.int32)
    r2 = jnp.sum(jnp.where(sel2, excl, 0.0), axis=0, keepdims=True).astype(jnp.int32)
    carry[...] = carry[...] + jnp.sum(ind, axis=1, keepdims=True)
    cnt_ref[...] = jnp.broadcast_to(carry[...], cnt_ref.shape).astype(jnp.int32)

    meta_ref[...] = jnp.where(eidx == 0, i1, jnp.where(eidx == 1, i2, jnp.where(eidx == 2, r1,
                              jnp.where(eidx == 3, r2, 0))))
    gate_ref[...] = jnp.where(eidx == 0, gate1, jnp.where(eidx == 1, gate2, 0.0))


def _l1_router(x2d, sh, sc, rw_t, *, seq, tm):
    T, D = x2d.shape
    per_b = pl.BlockSpec((1, 1, D), lambda i: (i // (seq // tm), 0, 0))
    kern = functools.partial(_l1_router_kernel, tm=tm)
    return pl.pallas_call(
        kern,
        grid=(T // tm,),
        in_specs=[
            pl.BlockSpec((tm, D), lambda i: (i, 0)),
            per_b, per_b,
            pl.BlockSpec((N_EXPERTS, D), lambda i: (0, 0)),
        ],
        out_specs=[
            pl.BlockSpec((tm, D), lambda i: (i, 0)),
            pl.BlockSpec((N_EXPERTS, tm), lambda i: (0, i)),
            pl.BlockSpec((N_EXPERTS, tm), lambda i: (0, i)),
            pl.BlockSpec((N_EXPERTS, LANES), lambda i: (0, 0)),
        ],
        out_shape=[
            jax.ShapeDtypeStruct((T, D), F32),
            jax.ShapeDtypeStruct((N_EXPERTS, T), jnp.int32),
            jax.ShapeDtypeStruct((N_EXPERTS, T), F32),
            jax.ShapeDtypeStruct((N_EXPERTS, LANES), jnp.int32),
        ],
        scratch_shapes=[pltpu.VMEM((N_EXPERTS, 1), F32)],
        compiler_params=_params(("arbitrary",)),
        name="l1_router",
    )(x2d, sh, sc, rw_t)


def _dispatch_kernel(pstart_ref, meta_ref, tok_ref, *, td, n_slots):
    i = pl.program_id(0)

    @pl.when(i == 0)
    def _():
        def clear(p, carry):
            tok_ref[p] = 0
            return carry

        lax.fori_loop(0, n_slots, clear, 0, unroll=8)

    def place(t, carry):
        for k in range(TOP_K):
            tok_ref[pstart_ref[meta_ref[k, t]] + meta_ref[TOP_K + k, t]] = i * td + t
        return carry

    lax.fori_loop(0, td, place, 0, unroll=4)


def _dispatch(pstart, meta, *, n_slots, td):
    T = meta.shape[1]
    kern = functools.partial(_dispatch_kernel, td=td, n_slots=n_slots)
    return pl.pallas_call(
        kern,
        grid_spec=pltpu.PrefetchScalarGridSpec(
            num_scalar_prefetch=1,
            grid=(T // td,),
            in_specs=[pl.BlockSpec((N_EXPERTS, td), lambda i, *_: (0, i), memory_space=pltpu.SMEM)],
            out_specs=pl.BlockSpec(memory_space=pltpu.SMEM),
        ),
        out_shape=jax.ShapeDtypeStruct((n_slots,), jnp.int32),
        compiler_params=_params(("arbitrary",)),
        name="dispatch",
    )(pstart, meta)


def _experts_kernel(blk_e_ref, nused_ref, tok_cur_ref, tok_next_ref, h_ref, wg_ref, wu_ref, wd_ref,
                    o_ref, xbuf, xs_ref, sem):
    b = pl.program_id(0)
    j = pl.program_id(1)
    nused = nused_ref[0]
    slot = b % 2

    def gather(tok_ref, s):
        def issue(r, carry):
            pltpu.make_async_copy(h_ref.at[pl.ds(tok_ref[0, r], 1)], xbuf.at[s, pl.ds(r, 1)],
                                  sem.at[s]).start()
            return carry

        lax.fori_loop(0, MOE_BLOCK, issue, 0, unroll=8)

    @pl.when(b < nused)
    def _():
        @pl.when(j == 0)
        def _():
            @pl.when(b == 0)
            def _():
                gather(tok_cur_ref, 0)

            pltpu.make_async_copy(h_ref.at[pl.ds(0, MOE_BLOCK)], xbuf.at[slot], sem.at[slot]).wait()
            xs_ref[...] = xbuf[slot].astype(BF16)
            o_ref[...] = jnp.zeros_like(o_ref)

            @pl.when(b + 1 < nused)
            def _():
                gather(tok_next_ref, 1 - slot)

        xs = xs_ref[...]
        a = (jax.nn.silu(_dot(xs, wg_ref[...])) * _dot(xs, wu_ref[...])).astype(BF16)
        o_ref[...] += _dot(a, wd_ref[...])

    @pl.when((b >= nused_ref[0]) & (j == 0))
    def _():
        o_ref[...] = jnp.zeros_like(o_ref)


def _experts(blk_e, nused, slot_tok, h, wg, wu, wd, *, n_blocks, tf):
    D = h.shape[1]
    dff = wg.shape[2]
    nj = dff // tf
    tok3 = slot_tok.reshape(n_blocks, 1, MOE_BLOCK)

    def jj(b, j, nu):
        return jnp.where(b < nu[0], j, nj - 1)

    def tok_spec(shift):
        return pl.BlockSpec((None, 1, MOE_BLOCK),
                            lambda b, j, be, nu: (jnp.minimum(b + shift, n_blocks - 1), 0, 0),
                            memory_space=pltpu.SMEM)

    return pl.pallas_call(
        _experts_kernel,
        grid_spec=pltpu.PrefetchScalarGridSpec(
            num_scalar_prefetch=2,
            grid=(n_blocks, nj),
            in_specs=[
                tok_spec(0), tok_spec(1),
                pl.BlockSpec(memory_space=pl.ANY),
                pl.BlockSpec((None, D, tf), lambda b, j, be, nu: (be[b], 0, jj(b, j, nu))),
                pl.BlockSpec((None, D, tf), lambda b, j, be, nu: (be[b], 0, jj(b, j, nu))),
                pl.BlockSpec((None, tf, D), lambda b, j, be, nu: (be[b], jj(b, j, nu), 0)),
            ],
            out_specs=pl.BlockSpec((MOE_BLOCK, D), lambda b, j, be, nu: (b, 0)),
            scratch_shapes=[pltpu.VMEM((2, MOE_BLOCK, D), F32), pltpu.VMEM((MOE_BLOCK, D), BF16),
                            pltpu.SemaphoreType.DMA((2,))],
        ),
        out_shape=jax.ShapeDtypeStruct((n_blocks * MOE_BLOCK, D), F32),
        compiler_params=_params(("arbitrary", "arbitrary")),
        name="experts",
    )(blk_e, nused, tok3, tok3, h, wg, wu, wd)


def _combine_kernel(pstart_ref, meta_cur_ref, meta_next_ref, gate_ref, x_ref, g_ref, yb_ref, o_ref,
                    buf, sem, *, tc):
    i = pl.program_id(0)
    slot = i % 2

    def gather(meta_ref, s):
        def issue(t, carry):
            for k in range(TOP_K):
                src = pstart_ref[meta_ref[k, t]] + meta_ref[TOP_K + k, t]
                pltpu.make_async_copy(yb_ref.at[pl.ds(src, 1)], buf.at[s, k, pl.ds(t, 1)],
                                      sem.at[s]).start()
            return carry

        lax.fori_loop(0, tc, issue, 0, unroll=4)

    @pl.when(i == 0)
    def _():
        gather(meta_cur_ref, 0)

    @pl.when(i + 1 < pl.num_programs(0))
    def _():
        gather(meta_next_ref, 1 - slot)

    for k in range(TOP_K):
        pltpu.make_async_copy(yb_ref.at[pl.ds(0, tc)], buf.at[slot, k], sem.at[slot]).wait()

    gates = gate_ref[...]
    moe = gates[:, 0:1] * buf[slot, 0] + gates[:, 1:2] * buf[slot, 1]
    o_ref[...] = x_ref[...] + g_ref[0] * moe


def _combine(pstart, meta, gates_t, x2d, g, yb, *, seq, tc):
    T, D = x2d.shape
    kern = functools.partial(_combine_kernel, tc=tc)
    return pl.pallas_call(
        kern,
        grid_spec=pltpu.PrefetchScalarGridSpec(
            num_scalar_prefetch=1,
            grid=(T // tc,),
            in_specs=[
                pl.BlockSpec((N_EXPERTS, tc), lambda i, *_: (0, i), memory_space=pltpu.SMEM),
                pl.BlockSpec((N_EXPERTS, tc), lambda i, *_: (0, jnp.minimum(i + 1, T // tc - 1)),
                             memory_space=pltpu.SMEM),
                pl.BlockSpec((tc, N_EXPERTS), lambda i, *_: (i, 0)),
                pl.BlockSpec((tc, D), lambda i, *_: (i, 0)),
                pl.BlockSpec((1, 1, D), lambda i, *_: (i // (seq // tc), 0, 0)),
                pl.BlockSpec(memory_space=pl.ANY),
            ],
            out_specs=pl.BlockSpec((tc, D), lambda i, *_: (i, 0)),
            scratch_shapes=[pltpu.VMEM((2, TOP_K, tc, D), F32), pltpu.SemaphoreType.DMA((2,))],
        ),
        out_shape=jax.ShapeDtypeStruct((T, D), F32),
        compiler_params=_params(("arbitrary",)),
        name="combine",
    )(pstart, meta, meta, gates_t, x2d, g, yb)


def _mod_params(c, w_mod, b_mod):
    B, D = c.shape
    c_pad = jnp.zeros((SUBLANES, D), F32).at[:B].set(c)
    mod = _adaln(c_pad, w_mod, b_mod)[:B]
    return [m.reshape(B, 1, D) for m in jnp.split(mod, 6, axis=-1)]


def kernel(x, c, positions, l0_w_mod, l0_b_mod, l0_w_in, l0_pool_w, l0_pool_scale, l0_conv_w, l0_conv_b, l0_conv_ln_g, l0_conv_ln_b, l0_w_out, l0_ffn_w_gate, l0_ffn_w_up, l0_ffn_w_down, l1_w_mod, l1_b_mod, l1_w_qkv, l1_q_norm, l1_k_norm, l1_sinks, l1_w_o, l1_router_w, l1_exp_w_gate, l1_exp_w_up, l1_exp_w_down):
    B, L, D = x.shape
    T = B * L
    bf = lambda w: w.astype(BF16)

    sh1, sc1, g1, sh2, sc2, g2 = _mod_params(c, l0_w_mod, l0_b_mod)
    x = _l0_mixer(x, sh1, sc1, g1, bf(l0_w_in), bf(l0_pool_w), l0_pool_scale, l0_conv_w, l0_conv_b,
                  l0_conv_ln_g, l0_conv_ln_b, bf(l0_w_out), tm=min(256, L))
    x = _l0_ffn(x.reshape(T, D), sh2, sc2, g2, bf(l0_ffn_w_gate), bf(l0_ffn_w_up), bf(l0_ffn_w_down),
                seq=L, tm=min(512, L), tf=512 if l0_ffn_w_gate.shape[1] % 512 == 0 else 256)
    x = x.reshape(B, L, D)

    sh1, sc1, g1, sh2, sc2, g2 = _mod_params(c, l1_w_mod, l1_b_mod)
    half = HEAD_DIM // 2
    inv = ROPE_THETA ** (-jnp.arange(half, dtype=F32) / half)
    inv_t = jnp.tile(inv, LANES // half).reshape(1, LANES)
    qn_t = jnp.tile(l1_q_norm, LANES // HEAD_DIM).reshape(1, LANES)
    kn_t = jnp.tile(l1_k_norm, LANES // HEAD_DIM).reshape(1, LANES)
    lane = jnp.arange(LANES)
    bd = (lane[:, None] // HEAD_DIM == lane[None, :] // HEAD_DIM).astype(BF16)
    q, k, v = _l1_qkv(x, sh1, sc1, positions.reshape(B, L, 1), inv_t, qn_t, kn_t, bd, bf(l1_w_qkv),
                      tm=min(256, L))
    x = _l1_attn(l1_sinks, q, k, v, x, g1, bf(l1_w_o), tq=min(256, L))

    x2d = x.reshape(T, D)
    h, meta, gates, cnt = _l1_router(x2d, sh2, sc2, l1_router_w.T, seq=L, tm=min(512, L))
    counts = cnt[:, 0]
    padded = ((counts + MOE_BLOCK - 1) // MOE_BLOCK) * MOE_BLOCK
    pend = jnp.cumsum(padded)
    pstart = pend - padded
    n_blocks = (T * TOP_K + N_EXPERTS * (MOE_BLOCK - 1) + MOE_BLOCK - 1) // MOE_BLOCK
    nused = (pend[-1] // MOE_BLOCK).astype(jnp.int32).reshape(1)
    blk_start = jnp.minimum(jnp.arange(n_blocks, dtype=jnp.int32), nused[0] - 1) * MOE_BLOCK
    blk_e = jnp.minimum(jnp.searchsorted(pend, blk_start, side='right'), N_EXPERTS - 1).astype(jnp.int32)
    slot_tok = _dispatch(pstart.astype(jnp.int32), meta, n_slots=n_blocks * MOE_BLOCK, td=min(2048, T))
    yb = _experts(blk_e, nused, slot_tok, h, bf(l1_exp_w_gate), bf(l1_exp_w_up), bf(l1_exp_w_down),
                  n_blocks=n_blocks, tf=512)
    out = _combine(pstart.astype(jnp.int32), meta, gates.T, x2d, g2, yb, seq=L, tc=min(256, L))
    return out.reshape(B, L, D)
```

```python
import functools

import jax
import jax.numpy as jnp
from jax import lax
from jax.experimental import pallas as pl
from jax.experimental.pallas import tpu as pltpu

F32 = jnp.float32
BF16 = jnp.bfloat16

HEAD_DIM = 64
Q_PER_KV = 8
POOL_WINDOWS = (2, 4, 8, 16)
CONV_WIDTH = 31
WINDOW = 128
ROPE_THETA = 10000.0
N_EXPERTS = 8
TOP_K = 2
NORM_EPS = 1e-6
LN_EPS = 1e-5

LANES = 128
SUBLANES = 8
HALO = 32
MOE_BLOCK = 512
VMEM_LIMIT = 56 * 1024 * 1024
NEG_INF = float("-inf")


def _params(sem, vmem=VMEM_LIMIT):
    return pltpu.CompilerParams(dimension_semantics=sem, vmem_limit_bytes=vmem)


def _rms_modulate(x, shift, scale):
    ms = jnp.mean(x * x, axis=-1, keepdims=True)
    return x * lax.rsqrt(ms + NORM_EPS) * (1.0 + scale) + shift


def _dot(a, b):
    return jnp.dot(a, b, preferred_element_type=F32)


def _dot_nt(a, b):
    return lax.dot_general(a, b, (((1,), (1,)), ((), ())), preferred_element_type=F32)


def _adaln_kernel(c_ref, w_ref, b_ref, o_ref):
    sc = jax.nn.silu(c_ref[...]).astype(BF16)
    o_ref[...] = _dot(sc, w_ref[...].astype(BF16)) + b_ref[...]


def _adaln(c_pad, w_mod, b_mod):
    d, n = w_mod.shape
    tn = 1024
    return pl.pallas_call(
        _adaln_kernel,
        grid=(n // tn,),
        in_specs=[
            pl.BlockSpec((SUBLANES, d), lambda j: (0, 0)),
            pl.BlockSpec((d, tn), lambda j: (0, j)),
            pl.BlockSpec((1, tn), lambda j: (0, j)),
        ],
        out_specs=pl.BlockSpec((SUBLANES, tn), lambda j: (0, j)),
        out_shape=jax.ShapeDtypeStruct((SUBLANES, n), F32),
        compiler_params=_params(("arbitrary",)),
        name="adaln",
    )(c_pad, w_mod, b_mod.reshape(1, n))


def _l0_mixer_kernel(x_ref, sh_ref, sc_ref, g_ref, w_in_ref, pool_w_ref, pool_scale_ref,
                     conv_w_ref, conv_b_ref, ln_g_ref, ln_b_ref, w_out_ref, o_ref,
                     u_ext, glu_ext, shift_ref, y_ref, mixed_ref, *, tm, d_pool, d_conv, gd):
    l = pl.program_id(1)

    @pl.when(l == 0)
    def _():
        u_ext[0:HALO, :] = jnp.zeros((HALO, d_pool), F32)
        glu_ext[0:HALO, :] = jnp.zeros((HALO, d_conv), F32)

    @pl.when(l > 0)
    def _():
        u_ext[0:HALO, :] = u_ext[tm:tm + HALO, :]
        glu_ext[0:HALO, :] = glu_ext[tm:tm + HALO, :]

    x = x_ref[0]
    h = _rms_modulate(x, sh_ref[0], sc_ref[0]).astype(BF16)
    z = _dot(h, w_in_ref[...])
    u_ext[HALO:HALO + tm, :] = z[:, :d_pool]
    glu_ext[HALO:HALO + tm, :] = z[:, d_pool:d_pool + d_conv] * jax.nn.sigmoid(z[:, d_pool + d_conv:])

    t1 = l * tm + lax.broadcasted_iota(jnp.int32, (tm, 1), 0) + 1
    for g, w in enumerate(POOL_WINDOWS):
        c0 = g * gd
        tok = u_ext[HALO:HALO + tm, c0:c0 + gd]
        s = tok
        for k in range(1, w):
            s = s + u_ext[HALO - k:HALO - k + tm, c0:c0 + gd]
        inv_cnt = 1.0 / jnp.minimum(t1, w).astype(F32)
        pooled = s * inv_cnt - tok
        mixed = _dot(pooled.astype(BF16), pool_w_ref[g]) * pool_scale_ref[:, c0:c0 + gd]
        mixed_ref[:, c0:c0 + gd] = mixed.astype(BF16)

    span = HALO + tm - SUBLANES
    for s in range(1, SUBLANES):
        shift_ref[s - 1, 0:span, :] = glu_ext[s:s + span, :]

    rows = 64
    base = HALO - (CONV_WIDTH - 1)
    for r0 in range(0, tm, rows):
        for c0 in range(0, d_conv, LANES):
            acc = jnp.broadcast_to(conv_b_ref[:, c0:c0 + LANES], (rows, LANES))
            for j in range(CONV_WIDTH):
                s = (base + j) % SUBLANES
                a0 = base + j - s + r0
                if s == 0:
                    tap = glu_ext[a0:a0 + rows, c0:c0 + LANES]
                else:
                    tap = shift_ref[s - 1, a0:a0 + rows, c0:c0 + LANES]
                acc = acc + conv_w_ref[j:j + 1, c0:c0 + LANES] * tap
            y_ref[r0:r0 + rows, c0:c0 + LANES] = acc

    y = y_ref[...]
    mu = jnp.mean(y, axis=-1, keepdims=True)
    yc = y - mu
    var = jnp.mean(yc * yc, axis=-1, keepdims=True)
    ln = yc * lax.rsqrt(var + LN_EPS) * ln_g_ref[...] + ln_b_ref[...]
    mixed_ref[:, d_pool:] = jax.nn.silu(ln).astype(BF16)

    out = _dot(mixed_ref[...], w_out_ref[...])
    o_ref[0] = x + g_ref[0] * out


def _l0_mixer(x, sh, sc, g, w_in, pool_w, pool_scale, conv_w, conv_b, ln_g, ln_b, w_out, *, tm):
    B, L, D = x.shape
    d_pool = pool_scale.shape[0]
    d_conv = conv_b.shape[0]
    gd = d_pool // len(POOL_WINDOWS)
    const2 = lambda b, l: (0, 0)
    const3 = lambda b, l: (0, 0, 0)
    per_b = pl.BlockSpec((1, 1, D), lambda b, l: (b, 0, 0))
    kern = functools.partial(_l0_mixer_kernel, tm=tm, d_pool=d_pool, d_conv=d_conv, gd=gd)
    return pl.pallas_call(
        kern,
        grid=(B, L // tm),
        in_specs=[
            pl.BlockSpec((1, tm, D), lambda b, l: (b, l, 0)),
            per_b, per_b, per_b,
            pl.BlockSpec(w_in.shape, const2, pipeline_mode=pl.Buffered(1)),
            pl.BlockSpec(pool_w.shape, const3, pipeline_mode=pl.Buffered(1)),
            pl.BlockSpec((1, d_pool), const2),
            pl.BlockSpec((CONV_WIDTH, d_conv), const2),
            pl.BlockSpec((1, d_conv), const2),
            pl.BlockSpec((1, d_conv), const2),
            pl.BlockSpec((1, d_conv), const2),
            pl.BlockSpec(w_out.shape, const2, pipeline_mode=pl.Buffered(1)),
        ],
        out_specs=pl.BlockSpec((1, tm, D), lambda b, l: (b, l, 0)),
        out_shape=jax.ShapeDtypeStruct((B, L, D), F32),
        scratch_shapes=[
            pltpu.VMEM((HALO + tm, d_pool), F32),
            pltpu.VMEM((HALO + tm, d_conv), F32),
            pltpu.VMEM((SUBLANES - 1, HALO + tm - SUBLANES, d_conv), F32),
            pltpu.VMEM((tm, d_conv), F32),
            pltpu.VMEM((tm, d_pool + d_conv), BF16),
        ],
        compiler_params=_params(("arbitrary", "arbitrary")),
        name="l0_mixer",
    )(x, sh, sc, g, w_in, pool_w, pool_scale.reshape(1, d_pool), conv_w.reshape(CONV_WIDTH, d_conv),
      conv_b.reshape(1, d_conv), ln_g.reshape(1, d_conv), ln_b.reshape(1, d_conv), w_out)


def _l0_ffn_kernel(x_ref, sh_ref, sc_ref, g_ref, wg_ref, wu_ref, wd_ref, o_ref, h_ref):
    j = pl.program_id(1)

    @pl.when(j == 0)
    def _():
        h_ref[...] = _rms_modulate(x_ref[...], sh_ref[0], sc_ref[0]).astype(BF16)
        o_ref[...] = jnp.zeros_like(o_ref)

    h = h_ref[...]
    a = (jax.nn.silu(_dot(h, wg_ref[...])) * _dot(h, wu_ref[...])).astype(BF16)
    o_ref[...] += _dot(a, wd_ref[...])

    @pl.when(j == pl.num_programs(1) - 1)
    def _():
        o_ref[...] = x_ref[...] + g_ref[0] * o_ref[...]


def _l0_ffn(x2d, sh, sc, g, wg, wu, wd, *, seq, tm, tf):
    T, D = x2d.shape
    dff = wg.shape[1]
    per_b = pl.BlockSpec((1, 1, D), lambda i, j: (i // (seq // tm), 0, 0))
    return pl.pallas_call(
        _l0_ffn_kernel,
        grid=(T // tm, dff // tf),
        in_specs=[
            pl.BlockSpec((tm, D), lambda i, j: (i, 0)),
            per_b, per_b, per_b,
            pl.BlockSpec((D, tf), lambda i, j: (0, j)),
            pl.BlockSpec((D, tf), lambda i, j: (0, j)),
            pl.BlockSpec((tf, D), lambda i, j: (j, 0)),
        ],
        out_specs=pl.BlockSpec((tm, D), lambda i, j: (i, 0)),
        out_shape=jax.ShapeDtypeStruct((T, D), F32),
        scratch_shapes=[pltpu.VMEM((tm, D), BF16)],
        compiler_params=_params(("arbitrary", "arbitrary")),
        name="l0_ffn",
    )(x2d, sh, sc, g, wg, wu, wd)


def _l1_qkv_kernel(x_ref, sh_ref, sc_ref, pos_ref, inv_ref, qn_ref, kn_ref, bd_ref, w_ref,
                   q_ref, k_ref, v_ref, *, n_q, n_kv):
    x = x_ref[0]
    h = _rms_modulate(x, sh_ref[0], sc_ref[0]).astype(BF16)
    qkv = _dot(h, w_ref[...])

    ang = pos_ref[0].astype(F32) * inv_ref[...]
    cos = jnp.cos(ang)
    sin = jnp.sin(ang)
    lane = lax.broadcasted_iota(jnp.int32, (1, LANES), 1)
    first_half = (lane % HEAD_DIM) < (HEAD_DIM // 2)
    sin_signed = jnp.where(first_half, -sin, sin)
    low_head = lane < HEAD_DIM
    bd = bd_ref[...]

    def norm_rope(blk, nw):
        sq = blk * blk
        hi = sq.astype(BF16)
        lo = (sq - hi.astype(F32)).astype(BF16)
        ss = _dot(hi, bd) + _dot(lo, bd)
        n = blk * lax.rsqrt(ss * (1.0 / HEAD_DIM) + NORM_EPS) * nw
        partner = jnp.where(first_half, pltpu.roll(n, LANES - HEAD_DIM // 2, 1),
                            pltpu.roll(n, HEAD_DIM // 2, 1))
        return n * cos + partner * sin_signed

    def split_heads(blk):
        a_lo = jnp.where(low_head, blk, 0.0)
        b_hi = jnp.where(low_head, 0.0, blk)
        return (a_lo, pltpu.roll(a_lo, HEAD_DIM, 1), pltpu.roll(b_hi, HEAD_DIM, 1), b_hi)

    scale = HEAD_DIM ** -0.5
    for cb in range(n_q):
        blk = qkv[:, cb * LANES:(cb + 1) * LANES]
        q_ref[0, :, cb * LANES:(cb + 1) * LANES] = (norm_rope(blk, qn_ref[...]) * scale).astype(BF16)
    k0 = n_q * LANES
    v0 = k0 + n_kv * LANES
    for cb in range(n_kv):
        kr = norm_rope(qkv[:, k0 + cb * LANES:k0 + (cb + 1) * LANES], kn_ref[...])
        for i, part in enumerate(split_heads(kr)):
            k_ref[0, :, (4 * cb + i) * LANES:(4 * cb + i + 1) * LANES] = part.astype(BF16)
        vr = qkv[:, v0 + cb * LANES:v0 + (cb + 1) * LANES]
        for i, part in enumerate(split_heads(vr)):
            v_ref[0, :, (4 * cb + i) * LANES:(4 * cb + i + 1) * LANES] = part.astype(BF16)


def _l1_qkv(x, sh, sc, pos3, inv_t, qn_t, kn_t, bd, w_qkv, *, tm):
    B, L, D = x.shape
    n_heads = D // HEAD_DIM
    n_kvh = n_heads // Q_PER_KV
    n_q = n_heads * HEAD_DIM // LANES
    n_kv = n_kvh * HEAD_DIM // LANES
    kw = n_kvh * 2 * LANES
    const2 = lambda b, l: (0, 0)
    per_b = pl.BlockSpec((1, 1, D), lambda b, l: (b, 0, 0))
    row = lambda w: pl.BlockSpec((1, tm, w), lambda b, l: (b, l, 0))
    kern = functools.partial(_l1_qkv_kernel, n_q=n_q, n_kv=n_kv)
    return pl.pallas_call(
        kern,
        grid=(B, L // tm),
        in_specs=[
            row(D), per_b, per_b, row(1),
            pl.BlockSpec((1, LANES), const2),
            pl.BlockSpec((1, LANES), const2),
            pl.BlockSpec((1, LANES), const2),
            pl.BlockSpec((LANES, LANES), const2),
            pl.BlockSpec(w_qkv.shape, const2, pipeline_mode=pl.Buffered(1)),
        ],
        out_specs=[row(D), row(kw), row(kw)],
        out_shape=[jax.ShapeDtypeStruct((B, L, D), BF16),
                   jax.ShapeDtypeStruct((B, L, kw), BF16),
                   jax.ShapeDtypeStruct((B, L, kw), BF16)],
        compiler_params=_params(("arbitrary", "arbitrary")),
        name="l1_qkv",
    )(x, sh, sc, pos3, inv_t, qn_t, kn_t, bd, w_qkv)


def _l1_attn_kernel(sinks_ref, q_ref, kc_ref, kp_ref, vc_ref, vp_ref, x_ref, g_ref, wo_ref,
                    o_ref, attn_ref, *, tq, n_pairs):
    i = pl.program_id(1)
    gp = Q_PER_KV // 2
    rows = gp * WINDOW
    qi = lax.broadcasted_iota(jnp.int32, (rows, WINDOW), 0) % WINDOW
    kj = lax.broadcasted_iota(jnp.int32, (rows, WINDOW), 1)
    pair_of_row = lax.broadcasted_iota(jnp.int32, (rows, 1), 0) // WINDOW
    mask_cur = kj <= qi
    mask_prev_band = kj > qi

    for n in range(tq // WINDOW):
        r0 = n * WINDOW
        if n == 0:
            k_prev, v_prev = kp_ref[0], vp_ref[0]
            mask_prev = kj > qi + jnp.where(i > 0, 0, WINDOW)
        else:
            k_prev, v_prev = kc_ref[0, r0 - WINDOW:r0, :], vc_ref[0, r0 - WINDOW:r0, :]
            mask_prev = mask_prev_band
        k_cur, v_cur = kc_ref[0, r0:r0 + WINDOW, :], vc_ref[0, r0:r0 + WINDOW, :]
        for kv in range(n_pairs // gp):
            p0 = kv * gp
            qs = jnp.concatenate([q_ref[0, r0:r0 + WINDOW, (p0 + j) * LANES:(p0 + j + 1) * LANES]
                                  for j in range(gp)], axis=0)
            acc = jnp.zeros((rows, LANES), F32)
            for half in range(2):
                c0 = (2 * kv + half) * LANES
                s_p = jnp.where(mask_prev, _dot_nt(qs, k_prev[:, c0:c0 + LANES]), NEG_INF)
                s_c = jnp.where(mask_cur, _dot_nt(qs, k_cur[:, c0:c0 + LANES]), NEG_INF)
                sink = jnp.full((rows, 1), sinks_ref[2 * p0 + half], F32)
                for j in range(1, gp):
                    sink = jnp.where(pair_of_row == j, sinks_ref[2 * (p0 + j) + half], sink)
                m = jnp.maximum(jnp.maximum(jnp.max(s_p, axis=1, keepdims=True),
                                            jnp.max(s_c, axis=1, keepdims=True)), sink)
                e_p = jnp.exp(s_p - m)
                e_c = jnp.exp(s_c - m)
                denom = (jnp.sum(e_p, axis=1, keepdims=True) + jnp.sum(e_c, axis=1, keepdims=True)
                         + jnp.exp(sink - m))
                o_h = _dot(e_p.astype(BF16), v_prev[:, c0:c0 + LANES]) + \
                    _dot(e_c.astype(BF16), v_cur[:, c0:c0 + LANES])
                acc = acc + o_h * (1.0 / denom)
            for j in range(gp):
                attn_ref[r0:r0 + WINDOW, (p0 + j) * LANES:(p0 + j + 1) * LANES] = \
                    acc[j * WINDOW:(j + 1) * WINDOW].astype(BF16)

    out = _dot(attn_ref[...], wo_ref[...])
    o_ref[0] = x_ref[0] + g_ref[0] * out


def _l1_attn(sinks, q, k, v, x, g, w_o, *, tq):
    B, L, D = x.shape
    kw = k.shape[2]
    nb = tq // WINDOW
    cur = lambda w: pl.BlockSpec((1, tq, w), lambda b, i: (b, i, 0))
    prev = pl.BlockSpec((1, WINDOW, kw), lambda b, i: (b, jnp.maximum(i * nb - 1, 0), 0))
    kern = functools.partial(_l1_attn_kernel, tq=tq, n_pairs=D // LANES)
    return pl.pallas_call(
        kern,
        grid=(B, L // tq),
        in_specs=[
            pl.BlockSpec(memory_space=pltpu.SMEM),
            cur(D), cur(kw), prev, cur(kw), prev, cur(D),
            pl.BlockSpec((1, 1, D), lambda b, i: (b, 0, 0)),
            pl.BlockSpec(w_o.shape, lambda b, i: (0, 0), pipeline_mode=pl.Buffered(1)),
        ],
        out_specs=cur(D),
        out_shape=jax.ShapeDtypeStruct((B, L, D), F32),
        scratch_shapes=[pltpu.VMEM((tq, D), BF16)],
        compiler_params=_params(("arbitrary", "arbitrary")),
        name="l1_attn",
    )(sinks, q, k, k, v, v, x, g, w_o)


def _l1_router_kernel(x_ref, sh_ref, sc_ref, rw_ref, h_ref, meta_ref, gate_ref, cnt_ref, carry,
                      *, tm):
    i = pl.program_id(0)

    @pl.when(i == 0)
    def _():
        carry[...] = jnp.zeros_like(carry)

    h = _rms_modulate(x_ref[...], sh_ref[0], sc_ref[0])
    h_ref[...] = h

    rw = rw_ref[...]
    h_hi = h.astype(BF16)
    h_lo = (h - h_hi.astype(F32)).astype(BF16)
    w_hi = rw.astype(BF16)
    w_lo = (rw - w_hi.astype(F32)).astype(BF16)
    logits = _dot_nt(w_hi, h_hi) + (_dot_nt(w_hi, h_lo) + _dot_nt(w_lo, h_hi))

    eidx = lax.broadcasted_iota(jnp.int32, (N_EXPERTS, tm), 0)
    m1 = jnp.max(logits, axis=0, keepdims=True)
    i1 = jnp.min(jnp.where(logits == m1, eidx, N_EXPERTS), axis=0, keepdims=True)
    rest = jnp.where(eidx == i1, NEG_INF, logits)
    m2 = jnp.max(rest, axis=0, keepdims=True)
    i2 = jnp.min(jnp.where(rest == m2, eidx, N_EXPERTS), axis=0, keepdims=True)
    e2 = jnp.exp(m2 - m1)
    gate1 = 1.0 / (1.0 + e2)
    gate2 = e2 / (1.0 + e2)

    sel1 = eidx == i1
    sel2 = eidx == i2
    ind = (sel1 | sel2).astype(F32)
    before = lax.broadcasted_iota(jnp.int32, (tm, tm), 0) < lax.broadcasted_iota(jnp.int32, (tm, tm), 1)
    excl = _dot(ind.astype(BF16), before.astype(BF16)) + carry[...]
    r1 = jnp.sum(jnp.where(sel1, excl, 0.0), axis=0, keepdims=True).astype(jnp.int32)
    r2 = jnp.sum(jnp.where(sel2, excl, 0.0), axis=0, keepdims=True).astype(jnp.int32)
    carry[...] = carry[...] + jnp.sum(ind, axis=1, keepdims=True)
    cnt_ref[...] = jnp.broadcast_to(carry[...], cnt_ref.shape).astype(jnp.int32)

    meta_ref[...] = jnp.where(eidx == 0, i1, jnp.where(eidx == 1, i2, jnp.where(eidx == 2, r1,
                              jnp.where(eidx == 3, r2, 0))))
    gate_ref[...] = jnp.where(eidx == 0, gate1, jnp.where(eidx == 1, gate2, 0.0))


def _l1_router(x2d, sh, sc, rw_t, *, seq, tm):
    T, D = x2d.shape
    per_b = pl.BlockSpec((1, 1, D), lambda i: (i // (seq // tm), 0, 0))
    kern = functools.partial(_l1_router_kernel, tm=tm)
    return pl.pallas_call(
        kern,
        grid=(T // tm,),
        in_specs=[
            pl.BlockSpec((tm, D), lambda i: (i, 0)),
            per_b, per_b,
            pl.BlockSpec((N_EXPERTS, D), lambda i: (0, 0)),
        ],
        out_specs=[
            pl.BlockSpec((tm, D), lambda i: (i, 0)),
            pl.BlockSpec((N_EXPERTS, tm), lambda i: (0, i)),
            pl.BlockSpec((N_EXPERTS, tm), lambda i: (0, i)),
            pl.BlockSpec((N_EXPERTS, LANES), lambda i: (0, 0)),
        ],
        out_shape=[
            jax.ShapeDtypeStruct((T, D), F32),
            jax.ShapeDtypeStruct((N_EXPERTS, T), jnp.int32),
            jax.ShapeDtypeStruct((N_EXPERTS, T), F32),
            jax.ShapeDtypeStruct((N_EXPERTS, LANES), jnp.int32),
        ],
        scratch_shapes=[pltpu.VMEM((N_EXPERTS, 1), F32)],
        compiler_params=_params(("arbitrary",)),
        name="l1_router",
    )(x2d, sh, sc, rw_t)


def _dispatch_kernel(pstart_ref, meta_ref, tok_ref, *, td, n_slots):
    i = pl.program_id(0)

    @pl.when(i == 0)
    def _():
        def clear(p, carry):
            tok_ref[p] = 0
            return carry

        lax.fori_loop(0, n_slots, clear, 0, unroll=8)

    def place(t, carry):
        for k in range(TOP_K):
            tok_ref[pstart_ref[meta_ref[k, t]] + meta_ref[TOP_K + k, t]] = i * td + t
        return carry

    lax.fori_loop(0, td, place, 0, unroll=4)


def _dispatch(pstart, meta, *, n_slots, td):
    T = meta.shape[1]
    kern = functools.partial(_dispatch_kernel, td=td, n_slots=n_slots)
    return pl.pallas_call(
        kern,
        grid_spec=pltpu.PrefetchScalarGridSpec(
            num_scalar_prefetch=1,
            grid=(T // td,),
            in_specs=[pl.BlockSpec((N_EXPERTS, td), lambda i, *_: (0, i), memory_space=pltpu.SMEM)],
            out_specs=pl.BlockSpec(memory_space=pltpu.SMEM),
        ),
        out_shape=jax.ShapeDtypeStruct((n_slots,), jnp.int32),
        compiler_params=_params(("arbitrary",)),
        name="dispatch",
    )(pstart, meta)


def _experts_kernel(blk_e_ref, nused_ref, tok_cur_ref, tok_next_ref, h_ref, wg_ref, wu_ref, wd_ref,
                    o_ref, xbuf, xs_ref, sem, *, n_blocks, nj):
    b = pl.program_id(0)
    j = pl.program_id(1)
    nused = nused_ref[0]
    slot = b % 2

    per_step = -(-MOE_BLOCK // nj)
    total = per_step * nj

    def row_copy(tok_ref, r, s):
        tok = tok_ref[0, jnp.minimum(r, MOE_BLOCK - 1)]
        return pltpu.make_async_copy(h_ref.at[pl.ds(tok, 1)], xbuf.at[s, pl.ds(r, 1)], sem.at[s])

    def wait_block(s):
        pltpu.make_async_copy(h_ref.at[pl.ds(0, MOE_BLOCK)], xbuf.at[s, pl.ds(0, MOE_BLOCK)],
                              sem.at[s]).wait()
        for r in range(MOE_BLOCK, total):
            pltpu.make_async_copy(h_ref.at[pl.ds(0, 1)], xbuf.at[s, pl.ds(r, 1)], sem.at[s]).wait()

    @pl.when(b < nused)
    def _():
        @pl.when(j == 0)
        def _():
            @pl.when(b == 0)
            def _():
                def issue(r, carry):
                    row_copy(tok_cur_ref, r, 0).start()
                    return carry

                lax.fori_loop(0, total, issue, 0)

            wait_block(slot)
            xs_ref[...] = xbuf[slot, 0:MOE_BLOCK].astype(BF16)
            o_ref[...] = jnp.zeros_like(o_ref)

        for u in range(per_step):
            row_copy(tok_next_ref, j * per_step + u, 1 - slot).start()

        xs = xs_ref[...]
        a = (jax.nn.silu(_dot(xs, wg_ref[...])) * _dot(xs, wu_ref[...])).astype(BF16)
        o_ref[...] += _dot(a, wd_ref[...])

    @pl.when((b == nused) & (j == 0))
    def _():
        wait_block(slot)

    @pl.when((b == n_blocks - 1) & (j == nj - 1) & (nused == n_blocks))
    def _():
        wait_block(1 - slot)

    @pl.when((b >= nused) & (j == 0))
    def _():
        o_ref[...] = jnp.zeros_like(o_ref)


def _experts(blk_e, nused, slot_tok, h, wg, wu, wd, *, n_blocks, tf):
    D = h.shape[1]
    dff = wg.shape[2]
    nj = dff // tf
    tok3 = slot_tok.reshape(n_blocks, 1, MOE_BLOCK)
    spare = -(-MOE_BLOCK // nj) * nj - MOE_BLOCK
    xrows = MOE_BLOCK + -(-spare // SUBLANES) * SUBLANES

    def jj(b, j, nu):
        return jnp.where(b < nu[0], j, nj - 1)

    def tok_spec(shift):
        return pl.BlockSpec((None, 1, MOE_BLOCK),
                            lambda b, j, be, nu: (jnp.minimum(b + shift, n_blocks - 1), 0, 0),
                            memory_space=pltpu.SMEM)

    return pl.pallas_call(
        functools.partial(_experts_kernel, n_blocks=n_blocks, nj=nj),
        grid_spec=pltpu.PrefetchScalarGridSpec(
            num_scalar_prefetch=2,
            grid=(n_blocks, nj),
            in_specs=[
                tok_spec(0), tok_spec(1),
                pl.BlockSpec(memory_space=pl.ANY),
                pl.BlockSpec((None, D, tf), lambda b, j, be, nu: (be[b], 0, jj(b, j, nu))),
                pl.BlockSpec((None, D, tf), lambda b, j, be, nu: (be[b], 0, jj(b, j, nu))),
                pl.BlockSpec((None, tf, D), lambda b, j, be, nu: (be[b], jj(b, j, nu), 0)),
            ],
            out_specs=pl.BlockSpec((MOE_BLOCK, D), lambda b, j, be, nu: (b, 0)),
            scratch_shapes=[pltpu.VMEM((2, xrows, D), F32), pltpu.VMEM((MOE_BLOCK, D), BF16),
                            pltpu.SemaphoreType.DMA((2,))],
        ),
        out_shape=jax.ShapeDtypeStruct((n_blocks * MOE_BLOCK, D), F32),
        compiler_params=_params(("arbitrary", "arbitrary")),
        name="experts",
    )(blk_e, nused, tok3, tok3, h, wg, wu, wd)


def _combine_kernel(pstart_ref, meta_cur_ref, meta_next_ref, gate_ref, x_ref, g_ref, yb_ref, o_ref,
                    buf, sem, *, tc):
    i = pl.program_id(0)
    slot = i % 2

    def gather(meta_ref, s):
        def issue(t, carry):
            for k in range(TOP_K):
                src = pstart_ref[meta_ref[k, t]] + meta_ref[TOP_K + k, t]
                pltpu.make_async_copy(yb_ref.at[pl.ds(src, 1)], buf.at[s, k, pl.ds(t, 1)],
                                      sem.at[s]).start()
            return carry

        lax.fori_loop(0, tc, issue, 0, unroll=4)

    @pl.when(i == 0)
    def _():
        gather(meta_cur_ref, 0)

    @pl.when(i + 1 < pl.num_programs(0))
    def _():
        gather(meta_next_ref, 1 - slot)

    for k in range(TOP_K):
        pltpu.make_async_copy(yb_ref.at[pl.ds(0, tc)], buf.at[slot, k], sem.at[slot]).wait()

    gates = gate_ref[...]
    moe = gates[:, 0:1] * buf[slot, 0] + gates[:, 1:2] * buf[slot, 1]
    o_ref[...] = x_ref[...] + g_ref[0] * moe


def _combine(pstart, meta, gates_t, x2d, g, yb, *, seq, tc):
    T, D = x2d.shape
    kern = functools.partial(_combine_kernel, tc=tc)
    return pl.pallas_call(
        kern,
        grid_spec=pltpu.PrefetchScalarGridSpec(
            num_scalar_prefetch=1,
            grid=(T // tc,),
            in_specs=[
                pl.BlockSpec((N_EXPERTS, tc), lambda i, *_: (0, i), memory_space=pltpu.SMEM),
                pl.BlockSpec((N_EXPERTS, tc), lambda i, *_: (0, jnp.minimum(i + 1, T // tc - 1)),
                             memory_space=pltpu.SMEM),
                pl.BlockSpec((tc, N_EXPERTS), lambda i, *_: (i, 0)),
                pl.BlockSpec((tc, D), lambda i, *_: (i, 0)),
                pl.BlockSpec((1, 1, D), lambda i, *_: (i // (seq // tc), 0, 0)),
                pl.BlockSpec(memory_space=pl.ANY),
            ],
            out_specs=pl.BlockSpec((tc, D), lambda i, *_: (i, 0)),
            scratch_shapes=[pltpu.VMEM((2, TOP_K, tc, D), F32), pltpu.SemaphoreType.DMA((2,))],
        ),
        out_shape=jax.ShapeDtypeStruct((T, D), F32),
        compiler_params=_params(("arbitrary",)),
        name="combine",
    )(pstart, meta, meta, gates_t, x2d, g, yb)


def _mod_params(c, w_mod, b_mod):
    B, D = c.shape
    c_pad = jnp.zeros((SUBLANES, D), F32).at[:B].set(c)
    mod = _adaln(c_pad, w_mod, b_mod)[:B]
    return [m.reshape(B, 1, D) for m in jnp.split(mod, 6, axis=-1)]


def kernel(x, c, positions, l0_w_mod, l0_b_mod, l0_w_in, l0_pool_w, l0_pool_scale, l0_conv_w, l0_conv_b, l0_conv_ln_g, l0_conv_ln_b, l0_w_out, l0_ffn_w_gate, l0_ffn_w_up, l0_ffn_w_down, l1_w_mod, l1_b_mod, l1_w_qkv, l1_q_norm, l1_k_norm, l1_sinks, l1_w_o, l1_router_w, l1_exp_w_gate, l1_exp_w_up, l1_exp_w_down):
    B, L, D = x.shape
    T = B * L
    bf = lambda w: w.astype(BF16)

    sh1, sc1, g1, sh2, sc2, g2 = _mod_params(c, l0_w_mod, l0_b_mod)
    x = _l0_mixer(x, sh1, sc1, g1, bf(l0_w_in), bf(l0_pool_w), l0_pool_scale, l0_conv_w, l0_conv_b,
                  l0_conv_ln_g, l0_conv_ln_b, bf(l0_w_out), tm=min(256, L))
    x = _l0_ffn(x.reshape(T, D), sh2, sc2, g2, bf(l0_ffn_w_gate), bf(l0_ffn_w_up), bf(l0_ffn_w_down),
                seq=L, tm=min(512, L), tf=512 if l0_ffn_w_gate.shape[1] % 512 == 0 else 256)
    x = x.reshape(B, L, D)

    sh1, sc1, g1, sh2, sc2, g2 = _mod_params(c, l1_w_mod, l1_b_mod)
    half = HEAD_DIM // 2
    inv = ROPE_THETA ** (-jnp.arange(half, dtype=F32) / half)
    inv_t = jnp.tile(inv, LANES // half).reshape(1, LANES)
    qn_t = jnp.tile(l1_q_norm, LANES // HEAD_DIM).reshape(1, LANES)
    kn_t = jnp.tile(l1_k_norm, LANES // HEAD_DIM).reshape(1, LANES)
    lane = jnp.arange(LANES)
    bd = (lane[:, None] // HEAD_DIM == lane[None, :] // HEAD_DIM).astype(BF16)
    q, k, v = _l1_qkv(x, sh1, sc1, positions.reshape(B, L, 1), inv_t, qn_t, kn_t, bd, bf(l1_w_qkv),
                      tm=min(256, L))
    x = _l1_attn(l1_sinks, q, k, v, x, g1, bf(l1_w_o), tq=min(256, L))

    x2d = x.reshape(T, D)
    h, meta, gates, cnt = _l1_router(x2d, sh2, sc2, l1_router_w.T, seq=L, tm=min(512, L))
    counts = cnt[:, 0]
    padded = ((counts + MOE_BLOCK - 1) // MOE_BLOCK) * MOE_BLOCK
    pend = jnp.cumsum(padded)
    pstart = pend - padded
    n_blocks = (T * TOP_K + N_EXPERTS * (MOE_BLOCK - 1) + MOE_BLOCK - 1) // MOE_BLOCK
    nused = (pend[-1] // MOE_BLOCK).astype(jnp.int32).reshape(1)
    blk_start = jnp.minimum(jnp.arange(n_blocks, dtype=jnp.int32), nused[0] - 1) * MOE_BLOCK
    blk_e = jnp.minimum(jnp.searchsorted(pend, blk_start, side='right'), N_EXPERTS - 1).astype(jnp.int32)
    slot_tok = _dispatch(pstart.astype(jnp.int32), meta, n_slots=n_blocks * MOE_BLOCK, td=min(2048, T))
    yb = _experts(blk_e, nused, slot_tok, h, bf(l1_exp_w_gate), bf(l1_exp_w_up), bf(l1_exp_w_down),
                  n_blocks=n_blocks, tf=512)
    out = _combine(pstart.astype(jnp.int32), meta, gates.T, x2d, g2, yb, seq=L, tc=min(256, L))
    return out.reshape(B, L, D)
```

```python
import functools

import jax
import jax.numpy as jnp
from jax import lax
from jax.experimental import pallas as pl
from jax.experimental.pallas import tpu as pltpu

F32 = jnp.float32
BF16 = jnp.bfloat16

HEAD_DIM = 64
Q_PER_KV = 8
POOL_WINDOWS = (2, 4, 8, 16)
CONV_WIDTH = 31
WINDOW = 128
ROPE_THETA = 10000.0
N_EXPERTS = 8
TOP_K = 2
NORM_EPS = 1e-6
LN_EPS = 1e-5

LANES = 128
SUBLANES = 8
HALO = 32
MOE_BLOCK = 512
VMEM_LIMIT = 56 * 1024 * 1024
VMEM_LIMIT_ATTN = 60 * 1024 * 1024
NEG_INF = float("-inf")


def _params(sem, vmem=VMEM_LIMIT):
    return pltpu.CompilerParams(dimension_semantics=sem, vmem_limit_bytes=vmem)


def _rms_modulate(x, shift, scale):
    ms = jnp.mean(x * x, axis=-1, keepdims=True)
    return x * lax.rsqrt(ms + NORM_EPS) * (1.0 + scale) + shift


def _dot(a, b):
    return jnp.dot(a, b, preferred_element_type=F32)


def _dot_nt(a, b):
    return lax.dot_general(a, b, (((1,), (1,)), ((), ())), preferred_element_type=F32)


def _adaln_kernel(c_ref, w_ref, b_ref, o_ref):
    sc = jax.nn.silu(c_ref[...]).astype(BF16)
    o_ref[...] = _dot(sc, w_ref[...].astype(BF16)) + b_ref[...]


def _adaln(c_pad, w_mod, b_mod):
    d, n = w_mod.shape
    tn = 1024
    return pl.pallas_call(
        _adaln_kernel,
        grid=(n // tn,),
        in_specs=[
            pl.BlockSpec((SUBLANES, d), lambda j: (0, 0)),
            pl.BlockSpec((d, tn), lambda j: (0, j)),
            pl.BlockSpec((1, tn), lambda j: (0, j)),
        ],
        out_specs=pl.BlockSpec((SUBLANES, tn), lambda j: (0, j)),
        out_shape=jax.ShapeDtypeStruct((SUBLANES, n), F32),
        compiler_params=_params(("arbitrary",)),
        name="adaln",
    )(c_pad, w_mod, b_mod.reshape(1, n))


def _l0_mixer_kernel(x_ref, sh_ref, sc_ref, g_ref, w_in_ref, pool_w_ref, pool_scale_ref,
                     conv_w_ref, conv_b_ref, ln_g_ref, ln_b_ref, w_out_ref, o_ref,
                     u_ext, glu_ext, shift_ref, y_ref, mixed_ref, *, tm, d_pool, d_conv, gd):
    l = pl.program_id(1)

    @pl.when(l == 0)
    def _():
        u_ext[0:HALO, :] = jnp.zeros((HALO, d_pool), F32)
        glu_ext[0:HALO, :] = jnp.zeros((HALO, d_conv), F32)

    @pl.when(l > 0)
    def _():
        u_ext[0:HALO, :] = u_ext[tm:tm + HALO, :]
        glu_ext[0:HALO, :] = glu_ext[tm:tm + HALO, :]

    x = x_ref[0]
    h = _rms_modulate(x, sh_ref[0], sc_ref[0]).astype(BF16)
    z = _dot(h, w_in_ref[...])
    u_ext[HALO:HALO + tm, :] = z[:, :d_pool]
    glu_ext[HALO:HALO + tm, :] = z[:, d_pool:d_pool + d_conv] * jax.nn.sigmoid(z[:, d_pool + d_conv:])

    t1 = l * tm + lax.broadcasted_iota(jnp.int32, (tm, 1), 0) + 1
    for g, w in enumerate(POOL_WINDOWS):
        c0 = g * gd
        tok = u_ext[HALO:HALO + tm, c0:c0 + gd]
        s = tok
        for k in range(1, w):
            s = s + u_ext[HALO - k:HALO - k + tm, c0:c0 + gd]
        inv_cnt = 1.0 / jnp.minimum(t1, w).astype(F32)
        pooled = s * inv_cnt - tok
        mixed = _dot(pooled.astype(BF16), pool_w_ref[g]) * pool_scale_ref[:, c0:c0 + gd]
        mixed_ref[:, c0:c0 + gd] = mixed.astype(BF16)

    span = HALO + tm - SUBLANES
    for s in range(1, SUBLANES):
        shift_ref[s - 1, 0:span, :] = glu_ext[s:s + span, :]

    rows = 64
    base = HALO - (CONV_WIDTH - 1)
    for r0 in range(0, tm, rows):
        for c0 in range(0, d_conv, LANES):
            acc = jnp.broadcast_to(conv_b_ref[:, c0:c0 + LANES], (rows, LANES))
            for j in range(CONV_WIDTH):
                s = (base + j) % SUBLANES
                a0 = base + j - s + r0
                if s == 0:
                    tap = glu_ext[a0:a0 + rows, c0:c0 + LANES]
                else:
                    tap = shift_ref[s - 1, a0:a0 + rows, c0:c0 + LANES]
                acc = acc + conv_w_ref[j:j + 1, c0:c0 + LANES] * tap
            y_ref[r0:r0 + rows, c0:c0 + LANES] = acc

    y = y_ref[...]
    mu = jnp.mean(y, axis=-1, keepdims=True)
    yc = y - mu
    var = jnp.mean(yc * yc, axis=-1, keepdims=True)
    ln = yc * lax.rsqrt(var + LN_EPS) * ln_g_ref[...] + ln_b_ref[...]
    mixed_ref[:, d_pool:] = jax.nn.silu(ln).astype(BF16)

    out = _dot(mixed_ref[...], w_out_ref[...])
    o_ref[0] = x + g_ref[0] * out


def _l0_mixer(x, sh, sc, g, w_in, pool_w, pool_scale, conv_w, conv_b, ln_g, ln_b, w_out, *, tm):
    B, L, D = x.shape
    d_pool = pool_scale.shape[0]
    d_conv = conv_b.shape[0]
    gd = d_pool // len(POOL_WINDOWS)
    const2 = lambda b, l: (0, 0)
    const3 = lambda b, l: (0, 0, 0)
    per_b = pl.BlockSpec((1, 1, D), lambda b, l: (b, 0, 0))
    kern = functools.partial(_l0_mixer_kernel, tm=tm, d_pool=d_pool, d_conv=d_conv, gd=gd)
    return pl.pallas_call(
        kern,
        grid=(B, L // tm),
        in_specs=[
            pl.BlockSpec((1, tm, D), lambda b, l: (b, l, 0)),
            per_b, per_b, per_b,
            pl.BlockSpec(w_in.shape, const2, pipeline_mode=pl.Buffered(1)),
            pl.BlockSpec(pool_w.shape, const3, pipeline_mode=pl.Buffered(1)),
            pl.BlockSpec((1, d_pool), const2),
            pl.BlockSpec((CONV_WIDTH, d_conv), const2),
            pl.BlockSpec((1, d_conv), const2),
            pl.BlockSpec((1, d_conv), const2),
            pl.BlockSpec((1, d_conv), const2),
            pl.BlockSpec(w_out.shape, const2, pipeline_mode=pl.Buffered(1)),
        ],
        out_specs=pl.BlockSpec((1, tm, D), lambda b, l: (b, l, 0)),
        out_shape=jax.ShapeDtypeStruct((B, L, D), F32),
        scratch_shapes=[
            pltpu.VMEM((HALO + tm, d_pool), F32),
            pltpu.VMEM((HALO + tm, d_conv), F32),
            pltpu.VMEM((SUBLANES - 1, HALO + tm - SUBLANES, d_conv), F32),
            pltpu.VMEM((tm, d_conv), F32),
            pltpu.VMEM((tm, d_pool + d_conv), BF16),
        ],
        compiler_params=_params(("arbitrary", "arbitrary")),
        name="l0_mixer",
    )(x, sh, sc, g, w_in, pool_w, pool_scale.reshape(1, d_pool), conv_w.reshape(CONV_WIDTH, d_conv),
      conv_b.reshape(1, d_conv), ln_g.reshape(1, d_conv), ln_b.reshape(1, d_conv), w_out)


def _cast_rider(w2d, n_chunks, chunk_of):
    rows, cols = w2d.shape
    assert rows % n_chunks == 0 and (rows // n_chunks) % (2 * SUBLANES) == 0
    spec = lambda: pl.BlockSpec((rows // n_chunks, cols), lambda *g: (chunk_of(*g), 0))
    return spec(), spec(), jax.ShapeDtypeStruct(w2d.shape, BF16)


def _l0_ffn_kernel(x_ref, sh_ref, sc_ref, g_ref, wg_ref, wu_ref, wd_ref, cw_ref, o_ref, cwo_ref, h_ref):
    j = pl.program_id(1)

    @pl.when(j == 0)
    def _():
        h_ref[...] = _rms_modulate(x_ref[...], sh_ref[0], sc_ref[0]).astype(BF16)
        o_ref[...] = jnp.zeros_like(o_ref)

    h = h_ref[...]
    a = (jax.nn.silu(_dot(h, wg_ref[...])) * _dot(h, wu_ref[...])).astype(BF16)
    o_ref[...] += _dot(a, wd_ref[...])
    cwo_ref[...] = cw_ref[...].astype(BF16)

    @pl.when(j == pl.num_programs(1) - 1)
    def _():
        o_ref[...] = x_ref[...] + g_ref[0] * o_ref[...]


def _l0_ffn(x2d, sh, sc, g, wg, wu, wd, cast_w, *, seq, tm, tf):
    T, D = x2d.shape
    dff = wg.shape[1]
    nj = dff // tf
    per_b = pl.BlockSpec((1, 1, D), lambda i, j: (i // (seq // tm), 0, 0))
    jc = min(8, nj)
    cw_in, cw_out, cw_shape = _cast_rider(cast_w, (T // tm) * jc, lambda i, j: i * jc + jnp.minimum(j, jc - 1))
    return pl.pallas_call(
        _l0_ffn_kernel,
        grid=(T // tm, nj),
        in_specs=[
            pl.BlockSpec((tm, D), lambda i, j: (i, 0)),
            per_b, per_b, per_b,
            pl.BlockSpec((D, tf), lambda i, j: (0, j)),
            pl.BlockSpec((D, tf), lambda i, j: (0, j)),
            pl.BlockSpec((tf, D), lambda i, j: (j, 0)),
            cw_in,
        ],
        out_specs=[pl.BlockSpec((tm, D), lambda i, j: (i, 0)), cw_out],
        out_shape=[jax.ShapeDtypeStruct((T, D), F32), cw_shape],
        scratch_shapes=[pltpu.VMEM((tm, D), BF16)],
        compiler_params=_params(("arbitrary", "arbitrary")),
        name="l0_ffn",
    )(x2d, sh, sc, g, wg, wu, wd, cast_w)


def _l1_qkv_kernel(x_ref, sh_ref, sc_ref, pos_ref, inv_ref, qn_ref, kn_ref, bd_ref, w_ref, cw_ref,
                   q_ref, k_ref, v_ref, cwo_ref, *, n_q, n_kv):
    cwo_ref[...] = cw_ref[...].astype(BF16)
    x = x_ref[0]
    h = _rms_modulate(x, sh_ref[0], sc_ref[0]).astype(BF16)
    qkv = _dot(h, w_ref[...])

    ang = pos_ref[0].astype(F32) * inv_ref[...]
    cos = jnp.cos(ang)
    sin = jnp.sin(ang)
    lane = lax.broadcasted_iota(jnp.int32, (1, LANES), 1)
    first_half = (lane % HEAD_DIM) < (HEAD_DIM // 2)
    sin_signed = jnp.where(first_half, -sin, sin)
    low_head = lane < HEAD_DIM
    bd = bd_ref[...]

    def norm_rope(blk, nw):
        sq = blk * blk
        hi = sq.astype(BF16)
        lo = (sq - hi.astype(F32)).astype(BF16)
        ss = _dot(hi, bd) + _dot(lo, bd)
        n = blk * lax.rsqrt(ss * (1.0 / HEAD_DIM) + NORM_EPS) * nw
        partner = jnp.where(first_half, pltpu.roll(n, LANES - HEAD_DIM // 2, 1),
                            pltpu.roll(n, HEAD_DIM // 2, 1))
        return n * cos + partner * sin_signed

    def split_heads(blk):
        a_lo = jnp.where(low_head, blk, 0.0)
        b_hi = jnp.where(low_head, 0.0, blk)
        return (a_lo, pltpu.roll(a_lo, HEAD_DIM, 1), pltpu.roll(b_hi, HEAD_DIM, 1), b_hi)

    scale = HEAD_DIM ** -0.5
    for cb in range(n_q):
        blk = qkv[:, cb * LANES:(cb + 1) * LANES]
        q_ref[0, :, cb * LANES:(cb + 1) * LANES] = (norm_rope(blk, qn_ref[...]) * scale).astype(BF16)
    k0 = n_q * LANES
    v0 = k0 + n_kv * LANES
    for cb in range(n_kv):
        kr = norm_rope(qkv[:, k0 + cb * LANES:k0 + (cb + 1) * LANES], kn_ref[...])
        for i, part in enumerate(split_heads(kr)):
            k_ref[0, :, (4 * cb + i) * LANES:(4 * cb + i + 1) * LANES] = part.astype(BF16)
        vr = qkv[:, v0 + cb * LANES:v0 + (cb + 1) * LANES]
        for i, part in enumerate(split_heads(vr)):
            v_ref[0, :, (4 * cb + i) * LANES:(4 * cb + i + 1) * LANES] = part.astype(BF16)


def _l1_qkv(x, sh, sc, pos3, inv_t, qn_t, kn_t, bd, w_qkv, cast_w, *, tm):
    B, L, D = x.shape
    cw_in, cw_out, cw_shape = _cast_rider(cast_w, B * (L // tm), lambda b, l: b * (L // tm) + l)
    n_heads = D // HEAD_DIM
    n_kvh = n_heads // Q_PER_KV
    n_q = n_heads * HEAD_DIM // LANES
    n_kv = n_kvh * HEAD_DIM // LANES
    kw = n_kvh * 2 * LANES
    const2 = lambda b, l: (0, 0)
    per_b = pl.BlockSpec((1, 1, D), lambda b, l: (b, 0, 0))
    row = lambda w: pl.BlockSpec((1, tm, w), lambda b, l: (b, l, 0))
    kern = functools.partial(_l1_qkv_kernel, n_q=n_q, n_kv=n_kv)
    return pl.pallas_call(
        kern,
        grid=(B, L // tm),
        in_specs=[
            row(D), per_b, per_b, row(1),
            pl.BlockSpec((1, LANES), const2),
            pl.BlockSpec((1, LANES), const2),
            pl.BlockSpec((1, LANES), const2),
            pl.BlockSpec((LANES, LANES), const2),
            pl.BlockSpec(w_qkv.shape, const2, pipeline_mode=pl.Buffered(1)),
            cw_in,
        ],
        out_specs=[row(D), row(kw), row(kw), cw_out],
        out_shape=[jax.ShapeDtypeStruct((B, L, D), BF16),
                   jax.ShapeDtypeStruct((B, L, kw), BF16),
                   jax.ShapeDtypeStruct((B, L, kw), BF16),
                   cw_shape],
        compiler_params=_params(("arbitrary", "arbitrary")),
        name="l1_qkv",
    )(x, sh, sc, pos3, inv_t, qn_t, kn_t, bd, w_qkv, cast_w)


def _l1_attn_kernel(sinks_ref, q_ref, kc_ref, kp_ref, vc_ref, vp_ref, x_ref, g_ref, wo_ref, cw_ref,
                    o_ref, cwo_ref, attn_ref, *, tq, n_pairs):
    cwo_ref[...] = cw_ref[...].astype(BF16)
    i = pl.program_id(1)
    gp = Q_PER_KV // 2
    rows = gp * WINDOW
    qi = lax.broadcasted_iota(jnp.int32, (rows, WINDOW), 0) % WINDOW
    kj = lax.broadcasted_iota(jnp.int32, (rows, WINDOW), 1)
    pair_of_row = lax.broadcasted_iota(jnp.int32, (rows, 1), 0) // WINDOW
    mask_cur = kj <= qi
    mask_prev_band = kj > qi

    for n in range(tq // WINDOW):
        r0 = n * WINDOW
        if n == 0:
            k_prev, v_prev = kp_ref[0], vp_ref[0]
            mask_prev = kj > qi + jnp.where(i > 0, 0, WINDOW)
        else:
            k_prev, v_prev = kc_ref[0, r0 - WINDOW:r0, :], vc_ref[0, r0 - WINDOW:r0, :]
            mask_prev = mask_prev_band
        k_cur, v_cur = kc_ref[0, r0:r0 + WINDOW, :], vc_ref[0, r0:r0 + WINDOW, :]
        for kv in range(n_pairs // gp):
            p0 = kv * gp
            qs = jnp.concatenate([q_ref[0, r0:r0 + WINDOW, (p0 + j) * LANES:(p0 + j + 1) * LANES]
                                  for j in range(gp)], axis=0)
            acc = jnp.zeros((rows, LANES), F32)
            for half in range(2):
                c0 = (2 * kv + half) * LANES
                s_p = jnp.where(mask_prev, _dot_nt(qs, k_prev[:, c0:c0 + LANES]), NEG_INF)
                s_c = jnp.where(mask_cur, _dot_nt(qs, k_cur[:, c0:c0 + LANES]), NEG_INF)
                sink = jnp.full((rows, 1), sinks_ref[2 * p0 + half], F32)
                for j in range(1, gp):
                    sink = jnp.where(pair_of_row == j, sinks_ref[2 * (p0 + j) + half], sink)
                m = jnp.maximum(jnp.maximum(jnp.max(s_p, axis=1, keepdims=True),
                                            jnp.max(s_c, axis=1, keepdims=True)), sink)
                e_p = jnp.exp(s_p - m)
                e_c = jnp.exp(s_c - m)
                denom = (jnp.sum(e_p, axis=1, keepdims=True) + jnp.sum(e_c, axis=1, keepdims=True)
                         + jnp.exp(sink - m))
                o_h = _dot(e_p.astype(BF16), v_prev[:, c0:c0 + LANES]) + \
                    _dot(e_c.astype(BF16), v_cur[:, c0:c0 + LANES])
                acc = acc + o_h * (1.0 / denom)
            for j in range(gp):
                attn_ref[r0:r0 + WINDOW, (p0 + j) * LANES:(p0 + j + 1) * LANES] = \
                    acc[j * WINDOW:(j + 1) * WINDOW].astype(BF16)

    out = _dot(attn_ref[...], wo_ref[...])
    o_ref[0] = x_ref[0] + g_ref[0] * out


def _l1_attn(sinks, q, k, v, x, g, w_o, cast_w, *, tq):
    B, L, D = x.shape
    kw = k.shape[2]
    nb = tq // WINDOW
    cur = lambda w: pl.BlockSpec((1, tq, w), lambda b, i: (b, i, 0))
    prev = pl.BlockSpec((1, WINDOW, kw), lambda b, i: (b, jnp.maximum(i * nb - 1, 0), 0))
    kern = functools.partial(_l1_attn_kernel, tq=tq, n_pairs=D // LANES)
    cw_in, cw_out, cw_shape = _cast_rider(cast_w, B * (L // tq), lambda b, i: b * (L // tq) + i)
    return pl.pallas_call(
        kern,
        grid=(B, L // tq),
        in_specs=[
            pl.BlockSpec(memory_space=pltpu.SMEM),
            cur(D), cur(kw), prev, cur(kw), prev, cur(D),
            pl.BlockSpec((1, 1, D), lambda b, i: (b, 0, 0)),
            pl.BlockSpec(w_o.shape, lambda b, i: (0, 0), pipeline_mode=pl.Buffered(1)),
            cw_in,
        ],
        out_specs=[cur(D), cw_out],
        out_shape=[jax.ShapeDtypeStruct((B, L, D), F32), cw_shape],
        scratch_shapes=[pltpu.VMEM((tq, D), BF16)],
        compiler_params=_params(("arbitrary", "arbitrary"), vmem=VMEM_LIMIT_ATTN),
        name="l1_attn",
    )(sinks, q, k, k, v, v, x, g, w_o, cast_w)


def _l1_router_kernel(x_ref, sh_ref, sc_ref, rw_ref, h_ref, meta_ref, gate_ref, cnt_ref, carry,
                      *, tm):
    i = pl.program_id(0)

    @pl.when(i == 0)
    def _():
        carry[...] = jnp.zeros_like(carry)

    h = _rms_modulate(x_ref[...], sh_ref[0], sc_ref[0])
    h_ref[...] = h

    rw = rw_ref[...]
    h_hi = h.astype(BF16)
    h_lo = (h - h_hi.astype(F32)).astype(BF16)
    w_hi = rw.astype(BF16)
    w_lo = (rw - w_hi.astype(F32)).astype(BF16)
    logits = _dot_nt(w_hi, h_hi) + (_dot_nt(w_hi, h_lo) + _dot_nt(w_lo, h_hi))

    eidx = lax.broadcasted_iota(jnp.int32, (N_EXPERTS, tm), 0)
    m1 = jnp.max(logits, axis=0, keepdims=True)
    i1 = jnp.min(jnp.where(logits == m1, eidx, N_EXPERTS), axis=0, keepdims=True)
    rest = jnp.where(eidx == i1, NEG_INF, logits)
    m2 = jnp.max(rest, axis=0, keepdims=True)
    i2 = jnp.min(jnp.where(rest == m2, eidx, N_EXPERTS), axis=0, keepdims=True)
    e2 = jnp.exp(m2 - m1)
    gate1 = 1.0 / (1.0 + e2)
    gate2 = e2 / (1.0 + e2)

    sel1 = eidx == i1
    sel2 = eidx == i2
    ind = (sel1 | sel2).astype(F32)
    before = lax.broadcasted_iota(jnp.int32, (tm, tm), 0) < lax.broadcasted_iota(jnp.int32, (tm, tm), 1)
    excl = _dot(ind.astype(BF16), before.astype(BF16)) + carry[...]
    r1 = jnp.sum(jnp.where(sel1, excl, 0.0), axis=0, keepdims=True).astype(jnp.int32)
    r2 = jnp.sum(jnp.where(sel2, excl, 0.0), axis=0, keepdims=True).astype(jnp.int32)
    carry[...] = carry[...] + jnp.sum(ind, axis=1, keepdims=True)
    cnt_ref[...] = jnp.broadcast_to(carry[...], cnt_ref.shape).astype(jnp.int32)

    meta_ref[...] = jnp.where(eidx == 0, i1, jnp.where(eidx == 1, i2, jnp.where(eidx == 2, r1,
                              jnp.where(eidx == 3, r2, 0))))
    gate_ref[...] = jnp.where(eidx == 0, gate1, jnp.where(eidx == 1, gate2, 0.0))


def _l1_router(x2d, sh, sc, rw_t, *, seq, tm):
    T, D = x2d.shape
    per_b = pl.BlockSpec((1, 1, D), lambda i: (i // (seq // tm), 0, 0))
    kern = functools.partial(_l1_router_kernel, tm=tm)
    return pl.pallas_call(
        kern,
        grid=(T // tm,),
        in_specs=[
            pl.BlockSpec((tm, D), lambda i: (i, 0)),
            per_b, per_b,
            pl.BlockSpec((N_EXPERTS, D), lambda i: (0, 0)),
        ],
        out_specs=[
            pl.BlockSpec((tm, D), lambda i: (i, 0)),
            pl.BlockSpec((N_EXPERTS, tm), lambda i: (0, i)),
            pl.BlockSpec((N_EXPERTS, tm), lambda i: (0, i)),
            pl.BlockSpec((N_EXPERTS, LANES), lambda i: (0, 0)),
        ],
        out_shape=[
            jax.ShapeDtypeStruct((T, D), F32),
            jax.ShapeDtypeStruct((N_EXPERTS, T), jnp.int32),
            jax.ShapeDtypeStruct((N_EXPERTS, T), F32),
            jax.ShapeDtypeStruct((N_EXPERTS, LANES), jnp.int32),
        ],
        scratch_shapes=[pltpu.VMEM((N_EXPERTS, 1), F32)],
        compiler_params=_params(("arbitrary",)),
        name="l1_router",
    )(x2d, sh, sc, rw_t)


def _dispatch_kernel(pstart_ref, meta_ref, tok_ref, *, td, n_slots):
    i = pl.program_id(0)

    @pl.when(i == 0)
    def _():
        def clear(p, carry):
            tok_ref[p] = 0
            return carry

        lax.fori_loop(0, n_slots, clear, 0, unroll=8)

    def place(t, carry):
        for k in range(TOP_K):
            tok_ref[pstart_ref[meta_ref[k, t]] + meta_ref[TOP_K + k, t]] = i * td + t
        return carry

    lax.fori_loop(0, td, place, 0, unroll=4)


def _dispatch(pstart, meta, *, n_slots, td):
    T = meta.shape[1]
    kern = functools.partial(_dispatch_kernel, td=td, n_slots=n_slots)
    return pl.pallas_call(
        kern,
        grid_spec=pltpu.PrefetchScalarGridSpec(
            num_scalar_prefetch=1,
            grid=(T // td,),
            in_specs=[pl.BlockSpec((N_EXPERTS, td), lambda i, *_: (0, i), memory_space=pltpu.SMEM)],
            out_specs=pl.BlockSpec(memory_space=pltpu.SMEM),
        ),
        out_shape=jax.ShapeDtypeStruct((n_slots,), jnp.int32),
        compiler_params=_params(("arbitrary",)),
        name="dispatch",
    )(pstart, meta)


def _experts_kernel(blk_e_ref, nused_ref, tok_cur_ref, tok_next_ref, h_ref, wg_ref, wu_ref, wd_ref,
                    o_ref, xbuf, xs_ref, sem, *, n_blocks, nj):
    b = pl.program_id(0)
    j = pl.program_id(1)
    nused = nused_ref[0]
    slot = b % 2

    per_step = -(-MOE_BLOCK // nj)
    total = per_step * nj

    def row_copy(tok_ref, r, s):
        tok = tok_ref[0, jnp.minimum(r, MOE_BLOCK - 1)]
        return pltpu.make_async_copy(h_ref.at[pl.ds(tok, 1)], xbuf.at[s, pl.ds(r, 1)], sem.at[s])

    def wait_block(s):
        pltpu.make_async_copy(h_ref.at[pl.ds(0, MOE_BLOCK)], xbuf.at[s, pl.ds(0, MOE_BLOCK)],
                              sem.at[s]).wait()
        for r in range(MOE_BLOCK, total):
            pltpu.make_async_copy(h_ref.at[pl.ds(0, 1)], xbuf.at[s, pl.ds(r, 1)], sem.at[s]).wait()

    @pl.when(b < nused)
    def _():
        @pl.when(j == 0)
        def _():
            @pl.when(b == 0)
            def _():
                def issue(r, carry):
                    row_copy(tok_cur_ref, r, 0).start()
                    return carry

                lax.fori_loop(0, total, issue, 0)

            wait_block(slot)
            xs_ref[...] = xbuf[slot, 0:MOE_BLOCK].astype(BF16)
            o_ref[...] = jnp.zeros_like(o_ref)

        for u in range(per_step):
            row_copy(tok_next_ref, j * per_step + u, 1 - slot).start()

        xs = xs_ref[...]
        a = (jax.nn.silu(_dot(xs, wg_ref[...])) * _dot(xs, wu_ref[...])).astype(BF16)
        o_ref[...] += _dot(a, wd_ref[...])

    @pl.when((b == nused) & (j == 0))
    def _():
        wait_block(slot)

    @pl.when((b == n_blocks - 1) & (j == nj - 1) & (nused == n_blocks))
    def _():
        wait_block(1 - slot)

    @pl.when((b >= nused) & (j == 0))
    def _():
        o_ref[...] = jnp.zeros_like(o_ref)


def _experts(blk_e, nused, slot_tok, h, wg, wu, wd, *, n_blocks, tf):
    D = h.shape[1]
    dff = wg.shape[2]
    nj = dff // tf
    tok3 = slot_tok.reshape(n_blocks, 1, MOE_BLOCK)
    spare = -(-MOE_BLOCK // nj) * nj - MOE_BLOCK
    xrows = MOE_BLOCK + -(-spare // SUBLANES) * SUBLANES

    def jj(b, j, nu):
        return jnp.where(b < nu[0], j, nj - 1)

    def tok_spec(shift):
        return pl.BlockSpec((None, 1, MOE_BLOCK),
                            lambda b, j, be, nu: (jnp.minimum(b + shift, n_blocks - 1), 0, 0),
                            memory_space=pltpu.SMEM)

    return pl.pallas_call(
        functools.partial(_experts_kernel, n_blocks=n_blocks, nj=nj),
        grid_spec=pltpu.PrefetchScalarGridSpec(
            num_scalar_prefetch=2,
            grid=(n_blocks, nj),
            in_specs=[
                tok_spec(0), tok_spec(1),
                pl.BlockSpec(memory_space=pl.ANY),
                pl.BlockSpec((None, D, tf), lambda b, j, be, nu: (be[b], 0, jj(b, j, nu))),
                pl.BlockSpec((None, D, tf), lambda b, j, be, nu: (be[b], 0, jj(b, j, nu))),
                pl.BlockSpec((None, tf, D), lambda b, j, be, nu: (be[b], jj(b, j, nu), 0)),
            ],
            out_specs=pl.BlockSpec((MOE_BLOCK, D), lambda b, j, be, nu: (b, 0)),
            scratch_shapes=[pltpu.VMEM((2, xrows, D), F32), pltpu.VMEM((MOE_BLOCK, D), BF16),
                            pltpu.SemaphoreType.DMA((2,))],
        ),
        out_shape=jax.ShapeDtypeStruct((n_blocks * MOE_BLOCK, D), F32),
        compiler_params=_params(("arbitrary", "arbitrary")),
        name="experts",
    )(blk_e, nused, tok3, tok3, h, wg, wu, wd)


def _combine_kernel(pstart_ref, meta_cur_ref, meta_next_ref, gate_ref, x_ref, g_ref, yb_ref, o_ref,
                    buf, sem, *, tc):
    i = pl.program_id(0)
    slot = i % 2

    def gather(meta_ref, s):
        def issue(t, carry):
            for k in range(TOP_K):
                src = pstart_ref[meta_ref[k, t]] + meta_ref[TOP_K + k, t]
                pltpu.make_async_copy(yb_ref.at[pl.ds(src, 1)], buf.at[s, k, pl.ds(t, 1)],
                                      sem.at[s]).start()
            return carry

        lax.fori_loop(0, tc, issue, 0, unroll=4)

    @pl.when(i == 0)
    def _():
        gather(meta_cur_ref, 0)

    @pl.when(i + 1 < pl.num_programs(0))
    def _():
        gather(meta_next_ref, 1 - slot)

    for k in range(TOP_K):
        pltpu.make_async_copy(yb_ref.at[pl.ds(0, tc)], buf.at[slot, k], sem.at[slot]).wait()

    gates = gate_ref[...]
    moe = gates[:, 0:1] * buf[slot, 0] + gates[:, 1:2] * buf[slot, 1]
    o_ref[...] = x_ref[...] + g_ref[0] * moe


def _combine(pstart, meta, gates_t, x2d, g, yb, *, seq, tc):
    T, D = x2d.shape
    kern = functools.partial(_combine_kernel, tc=tc)
    return pl.pallas_call(
        kern,
        grid_spec=pltpu.PrefetchScalarGridSpec(
            num_scalar_prefetch=1,
            grid=(T // tc,),
            in_specs=[
                pl.BlockSpec((N_EXPERTS, tc), lambda i, *_: (0, i), memory_space=pltpu.SMEM),
                pl.BlockSpec((N_EXPERTS, tc), lambda i, *_: (0, jnp.minimum(i + 1, T // tc - 1)),
                             memory_space=pltpu.SMEM),
                pl.BlockSpec((tc, N_EXPERTS), lambda i, *_: (i, 0)),
                pl.BlockSpec((tc, D), lambda i, *_: (i, 0)),
                pl.BlockSpec((1, 1, D), lambda i, *_: (i // (seq // tc), 0, 0)),
                pl.BlockSpec(memory_space=pl.ANY),
            ],
            out_specs=pl.BlockSpec((tc, D), lambda i, *_: (i, 0)),
            scratch_shapes=[pltpu.VMEM((2, TOP_K, tc, D), F32), pltpu.SemaphoreType.DMA((2,))],
        ),
        out_shape=jax.ShapeDtypeStruct((T, D), F32),
        compiler_params=_params(("arbitrary",)),
        name="combine",
    )(pstart, meta, meta, gates_t, x2d, g, yb)


def _mod_params(c, w_mod, b_mod):
    B, D = c.shape
    c_pad = jnp.zeros((SUBLANES, D), F32).at[:B].set(c)
    mod = _adaln(c_pad, w_mod, b_mod)[:B]
    return [m.reshape(B, 1, D) for m in jnp.split(mod, 6, axis=-1)]


def kernel(x, c, positions, l0_w_mod, l0_b_mod, l0_w_in, l0_pool_w, l0_pool_scale, l0_conv_w, l0_conv_b, l0_conv_ln_g, l0_conv_ln_b, l0_w_out, l0_ffn_w_gate, l0_ffn_w_up, l0_ffn_w_down, l1_w_mod, l1_b_mod, l1_w_qkv, l1_q_norm, l1_k_norm, l1_sinks, l1_w_o, l1_router_w, l1_exp_w_gate, l1_exp_w_up, l1_exp_w_down):
    B, L, D = x.shape
    T = B * L
    bf = lambda w: w.astype(BF16)

    sh1, sc1, g1, sh2, sc2, g2 = _mod_params(c, l0_w_mod, l0_b_mod)
    x = _l0_mixer(x, sh1, sc1, g1, bf(l0_w_in), bf(l0_pool_w), l0_pool_scale, l0_conv_w, l0_conv_b,
                  l0_conv_ln_g, l0_conv_ln_b, bf(l0_w_out), tm=min(256, L))
    n_e, _, dfe = l1_exp_w_gate.shape
    x, wg_e = _l0_ffn(x.reshape(T, D), sh2, sc2, g2, bf(l0_ffn_w_gate), bf(l0_ffn_w_up), bf(l0_ffn_w_down),
                      l1_exp_w_gate.reshape(n_e * D, dfe),
                      seq=L, tm=min(512, L), tf=512 if l0_ffn_w_gate.shape[1] % 512 == 0 else 256)
    x = x.reshape(B, L, D)

    sh1, sc1, g1, sh2, sc2, g2 = _mod_params(c, l1_w_mod, l1_b_mod)
    half = HEAD_DIM // 2
    inv = ROPE_THETA ** (-jnp.arange(half, dtype=F32) / half)
    inv_t = jnp.tile(inv, LANES // half).reshape(1, LANES)
    qn_t = jnp.tile(l1_q_norm, LANES // HEAD_DIM).reshape(1, LANES)
    kn_t = jnp.tile(l1_k_norm, LANES // HEAD_DIM).reshape(1, LANES)
    lane = jnp.arange(LANES)
    bd = (lane[:, None] // HEAD_DIM == lane[None, :] // HEAD_DIM).astype(BF16)
    q, k, v, wd_e = _l1_qkv(x, sh1, sc1, positions.reshape(B, L, 1), inv_t, qn_t, kn_t, bd, bf(l1_w_qkv),
                            l1_exp_w_down.reshape(n_e * dfe, D), tm=min(256, L))
    x, wu_e = _l1_attn(l1_sinks, q, k, v, x, g1, bf(l1_w_o), l1_exp_w_up.reshape(n_e * D, dfe),
                       tq=min(256, L))

    x2d = x.reshape(T, D)
    h, meta, gates, cnt = _l1_router(x2d, sh2, sc2, l1_router_w.T, seq=L, tm=min(512, L))
    counts = cnt[:, 0]
    padded = ((counts + MOE_BLOCK - 1) // MOE_BLOCK) * MOE_BLOCK
    pend = jnp.cumsum(padded)
    pstart = pend - padded
    n_blocks = (T * TOP_K + N_EXPERTS * (MOE_BLOCK - 1) + MOE_BLOCK - 1) // MOE_BLOCK
    nused = (pend[-1] // MOE_BLOCK).astype(jnp.int32).reshape(1)
    blk_start = jnp.minimum(jnp.arange(n_blocks, dtype=jnp.int32), nused[0] - 1) * MOE_BLOCK
    blk_e = jnp.minimum(jnp.searchsorted(pend, blk_start, side='right'), N_EXPERTS - 1).astype(jnp.int32)
    slot_tok = _dispatch(pstart.astype(jnp.int32), meta, n_slots=n_blocks * MOE_BLOCK, td=min(2048, T))
    yb = _experts(blk_e, nused, slot_tok, h, wg_e.reshape(n_e, D, dfe), wu_e.reshape(n_e, D, dfe),
                  wd_e.reshape(n_e, dfe, D), n_blocks=n_blocks, tf=1024 if dfe % 1024 == 0 else 512)
    out = _combine(pstart.astype(jnp.int32), meta, gates.T, x2d, g2, yb, seq=L, tc=min(256, L))
    return out.reshape(B, L, D)
```

```python
import functools

import jax
import jax.numpy as jnp
from jax import lax
from jax.experimental import pallas as pl
from jax.experimental.pallas import tpu as pltpu

F32 = jnp.float32
BF16 = jnp.bfloat16

HEAD_DIM = 64
Q_PER_KV = 8
POOL_WINDOWS = (2, 4, 8, 16)
CONV_WIDTH = 31
WINDOW = 128
ROPE_THETA = 10000.0
N_EXPERTS = 8
TOP_K = 2
NORM_EPS = 1e-6
LN_EPS = 1e-5

LANES = 128
SUBLANES = 8
HALO = 32
MOE_BLOCK = 512
VMEM_LIMIT = 56 * 1024 * 1024
VMEM_LIMIT_ATTN = 60 * 1024 * 1024
NEG_INF = float("-inf")


def _params(sem, vmem=VMEM_LIMIT):
    return pltpu.CompilerParams(dimension_semantics=sem, vmem_limit_bytes=vmem)


def _rms_modulate(x, shift, scale):
    ms = jnp.mean(x * x, axis=-1, keepdims=True)
    return x * lax.rsqrt(ms + NORM_EPS) * (1.0 + scale) + shift


def _dot(a, b):
    return jnp.dot(a, b, preferred_element_type=F32)


def _dot_nt(a, b):
    return lax.dot_general(a, b, (((1,), (1,)), ((), ())), preferred_element_type=F32)


def _adaln_kernel(c_ref, w_ref, b_ref, o_ref):
    sc = jax.nn.silu(c_ref[...]).astype(BF16)
    o_ref[...] = _dot(sc, w_ref[...].astype(BF16)) + b_ref[...]


def _adaln(c_pad, w_mod, b_mod):
    d, n = w_mod.shape
    tn = 1024
    return pl.pallas_call(
        _adaln_kernel,
        grid=(n // tn,),
        in_specs=[
            pl.BlockSpec((SUBLANES, d), lambda j: (0, 0)),
            pl.BlockSpec((d, tn), lambda j: (0, j)),
            pl.BlockSpec((1, tn), lambda j: (0, j)),
        ],
        out_specs=pl.BlockSpec((SUBLANES, tn), lambda j: (0, j)),
        out_shape=jax.ShapeDtypeStruct((SUBLANES, n), F32),
        compiler_params=_params(("arbitrary",)),
        name="adaln",
    )(c_pad, w_mod, b_mod.reshape(1, n))


def _l0_mixer_kernel(x_ref, sh_ref, sc_ref, g_ref, w_in_ref, pool_w_ref, pool_scale_ref,
                     conv_w_ref, conv_b_ref, ln_g_ref, ln_b_ref, w_out_ref, o_ref,
                     u_ext, glu_ext, shift_ref, y_ref, mixed_ref, *, tm, d_pool, d_conv, gd):
    l = pl.program_id(1)

    @pl.when(l == 0)
    def _():
        u_ext[0:HALO, :] = jnp.zeros((HALO, d_pool), F32)
        glu_ext[0:HALO, :] = jnp.zeros((HALO, d_conv), F32)

    @pl.when(l > 0)
    def _():
        u_ext[0:HALO, :] = u_ext[tm:tm + HALO, :]
        glu_ext[0:HALO, :] = glu_ext[tm:tm + HALO, :]

    x = x_ref[0]
    h = _rms_modulate(x, sh_ref[0], sc_ref[0]).astype(BF16)
    z = _dot(h, w_in_ref[...])
    u_ext[HALO:HALO + tm, :] = z[:, :d_pool]
    glu_ext[HALO:HALO + tm, :] = z[:, d_pool:d_pool + d_conv] * jax.nn.sigmoid(z[:, d_pool + d_conv:])

    t1 = l * tm + lax.broadcasted_iota(jnp.int32, (tm, 1), 0) + 1
    for g, w in enumerate(POOL_WINDOWS):
        c0 = g * gd
        tok = u_ext[HALO:HALO + tm, c0:c0 + gd]
        s = tok
        for k in range(1, w):
            s = s + u_ext[HALO - k:HALO - k + tm, c0:c0 + gd]
        inv_cnt = 1.0 / jnp.minimum(t1, w).astype(F32)
        pooled = s * inv_cnt - tok
        mixed = _dot(pooled.astype(BF16), pool_w_ref[g]) * pool_scale_ref[:, c0:c0 + gd]
        mixed_ref[:, c0:c0 + gd] = mixed.astype(BF16)

    span = HALO + tm - SUBLANES
    for s in range(1, SUBLANES):
        shift_ref[s - 1, 0:span, :] = glu_ext[s:s + span, :]

    rows = 64
    base = HALO - (CONV_WIDTH - 1)
    for r0 in range(0, tm, rows):
        for c0 in range(0, d_conv, LANES):
            acc = jnp.broadcast_to(conv_b_ref[:, c0:c0 + LANES], (rows, LANES))
            for j in range(CONV_WIDTH):
                s = (base + j) % SUBLANES
                a0 = base + j - s + r0
                if s == 0:
                    tap = glu_ext[a0:a0 + rows, c0:c0 + LANES]
                else:
                    tap = shift_ref[s - 1, a0:a0 + rows, c0:c0 + LANES]
                acc = acc + conv_w_ref[j:j + 1, c0:c0 + LANES] * tap
            y_ref[r0:r0 + rows, c0:c0 + LANES] = acc

    y = y_ref[...]
    mu = jnp.mean(y, axis=-1, keepdims=True)
    yc = y - mu
    var = jnp.mean(yc * yc, axis=-1, keepdims=True)
    ln = yc * lax.rsqrt(var + LN_EPS) * ln_g_ref[...] + ln_b_ref[...]
    mixed_ref[:, d_pool:] = jax.nn.silu(ln).astype(BF16)

    out = _dot(mixed_ref[...], w_out_ref[...])
    o_ref[0] = x + g_ref[0] * out


def _l0_mixer(x, sh, sc, g, w_in, pool_w, pool_scale, conv_w, conv_b, ln_g, ln_b, w_out, *, tm):
    B, L, D = x.shape
    d_pool = pool_scale.shape[0]
    d_conv = conv_b.shape[0]
    gd = d_pool // len(POOL_WINDOWS)
    const2 = lambda b, l: (0, 0)
    const3 = lambda b, l: (0, 0, 0)
    per_b = pl.BlockSpec((1, 1, D), lambda b, l: (b, 0, 0))
    kern = functools.partial(_l0_mixer_kernel, tm=tm, d_pool=d_pool, d_conv=d_conv, gd=gd)
    return pl.pallas_call(
        kern,
        grid=(B, L // tm),
        in_specs=[
            pl.BlockSpec((1, tm, D), lambda b, l: (b, l, 0)),
            per_b, per_b, per_b,
            pl.BlockSpec(w_in.shape, const2, pipeline_mode=pl.Buffered(1)),
            pl.BlockSpec(pool_w.shape, const3, pipeline_mode=pl.Buffered(1)),
            pl.BlockSpec((1, d_pool), const2),
            pl.BlockSpec((CONV_WIDTH, d_conv), const2),
            pl.BlockSpec((1, d_conv), const2),
            pl.BlockSpec((1, d_conv), const2),
            pl.BlockSpec((1, d_conv), const2),
            pl.BlockSpec(w_out.shape, const2, pipeline_mode=pl.Buffered(1)),
        ],
        out_specs=pl.BlockSpec((1, tm, D), lambda b, l: (b, l, 0)),
        out_shape=jax.ShapeDtypeStruct((B, L, D), F32),
        scratch_shapes=[
            pltpu.VMEM((HALO + tm, d_pool), F32),
            pltpu.VMEM((HALO + tm, d_conv), F32),
            pltpu.VMEM((SUBLANES - 1, HALO + tm - SUBLANES, d_conv), F32),
            pltpu.VMEM((tm, d_conv), F32),
            pltpu.VMEM((tm, d_pool + d_conv), BF16),
        ],
        compiler_params=_params(("arbitrary", "arbitrary")),
        name="l0_mixer",
    )(x, sh, sc, g, w_in, pool_w, pool_scale.reshape(1, d_pool), conv_w.reshape(CONV_WIDTH, d_conv),
      conv_b.reshape(1, d_conv), ln_g.reshape(1, d_conv), ln_b.reshape(1, d_conv), w_out)


def _cast_rider(w2d, n_chunks, chunk_of):
    rows, cols = w2d.shape
    assert rows % n_chunks == 0 and (rows // n_chunks) % (2 * SUBLANES) == 0
    spec = lambda: pl.BlockSpec((rows // n_chunks, cols), lambda *g: (chunk_of(*g), 0))
    return spec(), spec(), jax.ShapeDtypeStruct(w2d.shape, BF16)


def _l0_ffn_kernel(x_ref, sh_ref, sc_ref, g_ref, wg_ref, wu_ref, wd_ref, cw_ref, o_ref, cwo_ref, h_ref):
    j = pl.program_id(1)

    @pl.when(j == 0)
    def _():
        h_ref[...] = _rms_modulate(x_ref[...], sh_ref[0], sc_ref[0]).astype(BF16)
        o_ref[...] = jnp.zeros_like(o_ref)

    h = h_ref[...]
    a = (jax.nn.silu(_dot(h, wg_ref[...])) * _dot(h, wu_ref[...])).astype(BF16)
    o_ref[...] += _dot(a, wd_ref[...])
    cwo_ref[...] = cw_ref[...].astype(BF16)

    @pl.when(j == pl.num_programs(1) - 1)
    def _():
        o_ref[...] = x_ref[...] + g_ref[0] * o_ref[...]


def _l0_ffn(x2d, sh, sc, g, wg, wu, wd, cast_w, *, seq, tm, tf):
    T, D = x2d.shape
    dff = wg.shape[1]
    nj = dff // tf
    per_b = pl.BlockSpec((1, 1, D), lambda i, j: (i // (seq // tm), 0, 0))
    jc = min(8, nj)
    cw_in, cw_out, cw_shape = _cast_rider(cast_w, (T // tm) * jc, lambda i, j: i * jc + jnp.minimum(j, jc - 1))
    return pl.pallas_call(
        _l0_ffn_kernel,
        grid=(T // tm, nj),
        in_specs=[
            pl.BlockSpec((tm, D), lambda i, j: (i, 0)),
            per_b, per_b, per_b,
            pl.BlockSpec((D, tf), lambda i, j: (0, j)),
            pl.BlockSpec((D, tf), lambda i, j: (0, j)),
            pl.BlockSpec((tf, D), lambda i, j: (j, 0)),
            cw_in,
        ],
        out_specs=[pl.BlockSpec((tm, D), lambda i, j: (i, 0)), cw_out],
        out_shape=[jax.ShapeDtypeStruct((T, D), F32), cw_shape],
        scratch_shapes=[pltpu.VMEM((tm, D), BF16)],
        compiler_params=_params(("arbitrary", "arbitrary")),
        name="l0_ffn",
    )(x2d, sh, sc, g, wg, wu, wd, cast_w)


def _l1_qkv_kernel(x_ref, sh_ref, sc_ref, pos_ref, inv_ref, qn_ref, kn_ref, bd_ref, w_ref, cw_ref,
                   q_ref, k_ref, v_ref, cwo_ref, *, n_q, n_kv):
    cwo_ref[...] = cw_ref[...].astype(BF16)
    x = x_ref[0]
    h = _rms_modulate(x, sh_ref[0], sc_ref[0]).astype(BF16)
    qkv = _dot(h, w_ref[...])

    ang = pos_ref[0].astype(F32) * inv_ref[...]
    cos = jnp.cos(ang)
    sin = jnp.sin(ang)
    lane = lax.broadcasted_iota(jnp.int32, (1, LANES), 1)
    first_half = (lane % HEAD_DIM) < (HEAD_DIM // 2)
    sin_signed = jnp.where(first_half, -sin, sin)
    low_head = lane < HEAD_DIM
    bd = bd_ref[...]

    def norm_rope(blk, nw):
        sq = blk * blk
        hi = sq.astype(BF16)
        lo = (sq - hi.astype(F32)).astype(BF16)
        ss = _dot(hi, bd) + _dot(lo, bd)
        n = blk * lax.rsqrt(ss * (1.0 / HEAD_DIM) + NORM_EPS) * nw
        partner = jnp.where(first_half, pltpu.roll(n, LANES - HEAD_DIM // 2, 1),
                            pltpu.roll(n, HEAD_DIM // 2, 1))
        return n * cos + partner * sin_signed

    def split_heads(blk):
        a_lo = jnp.where(low_head, blk, 0.0)
        b_hi = jnp.where(low_head, 0.0, blk)
        return (a_lo, pltpu.roll(a_lo, HEAD_DIM, 1), pltpu.roll(b_hi, HEAD_DIM, 1), b_hi)

    scale = HEAD_DIM ** -0.5
    for cb in range(n_q):
        blk = qkv[:, cb * LANES:(cb + 1) * LANES]
        q_ref[0, :, cb * LANES:(cb + 1) * LANES] = (norm_rope(blk, qn_ref[...]) * scale).astype(BF16)
    k0 = n_q * LANES
    v0 = k0 + n_kv * LANES
    for cb in range(n_kv):
        kr = norm_rope(qkv[:, k0 + cb * LANES:k0 + (cb + 1) * LANES], kn_ref[...])
        for i, part in enumerate(split_heads(kr)):
            k_ref[0, :, (4 * cb + i) * LANES:(4 * cb + i + 1) * LANES] = part.astype(BF16)
        vr = qkv[:, v0 + cb * LANES:v0 + (cb + 1) * LANES]
        for i, part in enumerate(split_heads(vr)):
            v_ref[0, :, (4 * cb + i) * LANES:(4 * cb + i + 1) * LANES] = part.astype(BF16)


def _l1_qkv(x, sh, sc, pos3, inv_t, qn_t, kn_t, bd, w_qkv, cast_w, *, tm):
    B, L, D = x.shape
    cw_in, cw_out, cw_shape = _cast_rider(cast_w, B * (L // tm), lambda b, l: b * (L // tm) + l)
    n_heads = D // HEAD_DIM
    n_kvh = n_heads // Q_PER_KV
    n_q = n_heads * HEAD_DIM // LANES
    n_kv = n_kvh * HEAD_DIM // LANES
    kw = n_kvh * 2 * LANES
    const2 = lambda b, l: (0, 0)
    per_b = pl.BlockSpec((1, 1, D), lambda b, l: (b, 0, 0))
    row = lambda w: pl.BlockSpec((1, tm, w), lambda b, l: (b, l, 0))
    kern = functools.partial(_l1_qkv_kernel, n_q=n_q, n_kv=n_kv)
    return pl.pallas_call(
        kern,
        grid=(B, L // tm),
        in_specs=[
            row(D), per_b, per_b, row(1),
            pl.BlockSpec((1, LANES), const2),
            pl.BlockSpec((1, LANES), const2),
            pl.BlockSpec((1, LANES), const2),
            pl.BlockSpec((LANES, LANES), const2),
            pl.BlockSpec(w_qkv.shape, const2, pipeline_mode=pl.Buffered(1)),
            cw_in,
        ],
        out_specs=[row(D), row(kw), row(kw), cw_out],
        out_shape=[jax.ShapeDtypeStruct((B, L, D), BF16),
                   jax.ShapeDtypeStruct((B, L, kw), BF16),
                   jax.ShapeDtypeStruct((B, L, kw), BF16),
                   cw_shape],
        compiler_params=_params(("arbitrary", "arbitrary")),
        name="l1_qkv",
    )(x, sh, sc, pos3, inv_t, qn_t, kn_t, bd, w_qkv, cast_w)


def _l1_attn_kernel(sinks_ref, q_ref, kc_ref, kp_ref, vc_ref, vp_ref, x_ref, g_ref, wo_ref, cw_ref,
                    o_ref, cwo_ref, attn_ref, *, tq, n_pairs):
    cwo_ref[...] = cw_ref[...].astype(BF16)
    i = pl.program_id(1)
    gp = Q_PER_KV // 2
    rows = gp * WINDOW
    qi = lax.broadcasted_iota(jnp.int32, (rows, WINDOW), 0) % WINDOW
    kj = lax.broadcasted_iota(jnp.int32, (rows, WINDOW), 1)
    pair_of_row = lax.broadcasted_iota(jnp.int32, (rows, 1), 0) // WINDOW
    mask_cur = kj <= qi
    mask_prev_band = kj > qi

    for n in range(tq // WINDOW):
        r0 = n * WINDOW
        if n == 0:
            k_prev, v_prev = kp_ref[0], vp_ref[0]
            mask_prev = kj > qi + jnp.where(i > 0, 0, WINDOW)
        else:
            k_prev, v_prev = kc_ref[0, r0 - WINDOW:r0, :], vc_ref[0, r0 - WINDOW:r0, :]
            mask_prev = mask_prev_band
        k_cur, v_cur = kc_ref[0, r0:r0 + WINDOW, :], vc_ref[0, r0:r0 + WINDOW, :]
        for kv in range(n_pairs // gp):
            p0 = kv * gp
            qs = jnp.concatenate([q_ref[0, r0:r0 + WINDOW, (p0 + j) * LANES:(p0 + j + 1) * LANES]
                                  for j in range(gp)], axis=0)
            acc = jnp.zeros((rows, LANES), F32)
            for half in range(2):
                c0 = (2 * kv + half) * LANES
                s_p = jnp.where(mask_prev, _dot_nt(qs, k_prev[:, c0:c0 + LANES]), NEG_INF)
                s_c = jnp.where(mask_cur, _dot_nt(qs, k_cur[:, c0:c0 + LANES]), NEG_INF)
                sink = jnp.full((rows, 1), sinks_ref[2 * p0 + half], F32)
                for j in range(1, gp):
                    sink = jnp.where(pair_of_row == j, sinks_ref[2 * (p0 + j) + half], sink)
                m = jnp.maximum(jnp.max(jnp.maximum(s_p, s_c), axis=1, keepdims=True), sink)
                e_p = jnp.exp(s_p - m)
                e_c = jnp.exp(s_c - m)
                denom = jnp.sum(e_p + e_c, axis=1, keepdims=True) + jnp.exp(sink - m)
                o_h = _dot(e_p.astype(BF16), v_prev[:, c0:c0 + LANES]) + \
                    _dot(e_c.astype(BF16), v_cur[:, c0:c0 + LANES])
                acc = acc + o_h * (1.0 / denom)
            for j in range(gp):
                attn_ref[r0:r0 + WINDOW, (p0 + j) * LANES:(p0 + j + 1) * LANES] = \
                    acc[j * WINDOW:(j + 1) * WINDOW].astype(BF16)

    out = _dot(attn_ref[...], wo_ref[...])
    o_ref[0] = x_ref[0] + g_ref[0] * out


def _l1_attn(sinks, q, k, v, x, g, w_o, cast_w, *, tq):
    B, L, D = x.shape
    kw = k.shape[2]
    nb = tq // WINDOW
    cur = lambda w: pl.BlockSpec((1, tq, w), lambda b, i: (b, i, 0))
    prev = pl.BlockSpec((1, WINDOW, kw), lambda b, i: (b, jnp.maximum(i * nb - 1, 0), 0))
    kern = functools.partial(_l1_attn_kernel, tq=tq, n_pairs=D // LANES)
    cw_in, cw_out, cw_shape = _cast_rider(cast_w, B * (L // tq), lambda b, i: b * (L // tq) + i)
    return pl.pallas_call(
        kern,
        grid=(B, L // tq),
        in_specs=[
            pl.BlockSpec(memory_space=pltpu.SMEM),
            cur(D), cur(kw), prev, cur(kw), prev, cur(D),
            pl.BlockSpec((1, 1, D), lambda b, i: (b, 0, 0)),
            pl.BlockSpec(w_o.shape, lambda b, i: (0, 0), pipeline_mode=pl.Buffered(1)),
            cw_in,
        ],
        out_specs=[cur(D), cw_out],
        out_shape=[jax.ShapeDtypeStruct((B, L, D), F32), cw_shape],
        scratch_shapes=[pltpu.VMEM((tq, D), BF16)],
        compiler_params=_params(("arbitrary", "arbitrary"), vmem=VMEM_LIMIT_ATTN),
        name="l1_attn",
    )(sinks, q, k, k, v, v, x, g, w_o, cast_w)


def _l1_router_kernel(x_ref, sh_ref, sc_ref, rw_ref, h_ref, meta_ref, gate_ref, cnt_ref, carry,
                      *, tm):
    i = pl.program_id(0)

    @pl.when(i == 0)
    def _():
        carry[...] = jnp.zeros_like(carry)

    h = _rms_modulate(x_ref[...], sh_ref[0], sc_ref[0])
    h_ref[...] = h

    rw = rw_ref[...]
    h_hi = h.astype(BF16)
    h_lo = (h - h_hi.astype(F32)).astype(BF16)
    w_hi = rw.astype(BF16)
    w_lo = (rw - w_hi.astype(F32)).astype(BF16)
    logits = _dot_nt(w_hi, h_hi) + (_dot_nt(w_hi, h_lo) + _dot_nt(w_lo, h_hi))

    eidx = lax.broadcasted_iota(jnp.int32, (N_EXPERTS, tm), 0)
    m1 = jnp.max(logits, axis=0, keepdims=True)
    i1 = jnp.min(jnp.where(logits == m1, eidx, N_EXPERTS), axis=0, keepdims=True)
    rest = jnp.where(eidx == i1, NEG_INF, logits)
    m2 = jnp.max(rest, axis=0, keepdims=True)
    i2 = jnp.min(jnp.where(rest == m2, eidx, N_EXPERTS), axis=0, keepdims=True)
    e2 = jnp.exp(m2 - m1)
    gate1 = 1.0 / (1.0 + e2)
    gate2 = e2 / (1.0 + e2)

    sel1 = eidx == i1
    sel2 = eidx == i2
    ind = (sel1 | sel2).astype(F32)
    before = lax.broadcasted_iota(jnp.int32, (tm, tm), 0) < lax.broadcasted_iota(jnp.int32, (tm, tm), 1)
    excl = _dot(ind.astype(BF16), before.astype(BF16)) + carry[...]
    r1 = jnp.sum(jnp.where(sel1, excl, 0.0), axis=0, keepdims=True).astype(jnp.int32)
    r2 = jnp.sum(jnp.where(sel2, excl, 0.0), axis=0, keepdims=True).astype(jnp.int32)
    carry[...] = carry[...] + jnp.sum(ind, axis=1, keepdims=True)
    cnt_ref[...] = jnp.broadcast_to(carry[...], cnt_ref.shape).astype(jnp.int32)

    meta_ref[...] = jnp.where(eidx == 0, i1, jnp.where(eidx == 1, i2, jnp.where(eidx == 2, r1,
                              jnp.where(eidx == 3, r2, 0))))
    gate_ref[...] = jnp.where(eidx == 0, gate1, jnp.where(eidx == 1, gate2, 0.0))


def _l1_router(x2d, sh, sc, rw_t, *, seq, tm):
    T, D = x2d.shape
    per_b = pl.BlockSpec((1, 1, D), lambda i: (i // (seq // tm), 0, 0))
    kern = functools.partial(_l1_router_kernel, tm=tm)
    return pl.pallas_call(
        kern,
        grid=(T // tm,),
        in_specs=[
            pl.BlockSpec((tm, D), lambda i: (i, 0)),
            per_b, per_b,
            pl.BlockSpec((N_EXPERTS, D), lambda i: (0, 0)),
        ],
        out_specs=[
            pl.BlockSpec((tm, D), lambda i: (i, 0)),
            pl.BlockSpec((N_EXPERTS, tm), lambda i: (0, i)),
            pl.BlockSpec((N_EXPERTS, tm), lambda i: (0, i)),
            pl.BlockSpec((N_EXPERTS, LANES), lambda i: (0, 0)),
        ],
        out_shape=[
            jax.ShapeDtypeStruct((T, D), F32),
            jax.ShapeDtypeStruct((N_EXPERTS, T), jnp.int32),
            jax.ShapeDtypeStruct((N_EXPERTS, T), F32),
            jax.ShapeDtypeStruct((N_EXPERTS, LANES), jnp.int32),
        ],
        scratch_shapes=[pltpu.VMEM((N_EXPERTS, 1), F32)],
        compiler_params=_params(("arbitrary",)),
        name="l1_router",
    )(x2d, sh, sc, rw_t)


def _dispatch_kernel(slot_ref, tok_ref, *, td, n_slots):
    i = pl.program_id(0)

    @pl.when(i == 0)
    def _():
        def clear(p, carry):
            tok_ref[p] = 0
            return carry

        lax.fori_loop(0, n_slots, clear, 0, unroll=8)

    def place(t, carry):
        for k in range(TOP_K):
            tok_ref[slot_ref[k, t]] = i * td + t
        return carry

    lax.fori_loop(0, td, place, 0, unroll=8)


def _dispatch(slots, *, n_slots, td):
    T = slots.shape[1]
    kern = functools.partial(_dispatch_kernel, td=td, n_slots=n_slots)
    return pl.pallas_call(
        kern,
        grid=(T // td,),
        in_specs=[pl.BlockSpec((TOP_K, td), lambda i: (0, i), memory_space=pltpu.SMEM)],
        out_specs=pl.BlockSpec(memory_space=pltpu.SMEM),
        out_shape=jax.ShapeDtypeStruct((n_slots,), jnp.int32),
        compiler_params=_params(("arbitrary",)),
        name="dispatch",
    )(slots)


def _experts_kernel(blk_e_ref, nused_ref, tok_cur_ref, tok_next_ref, h_ref, wg_ref, wu_ref, wd_ref,
                    o_ref, xbuf, xs_ref, sem, *, n_blocks, nj):
    b = pl.program_id(0)
    j = pl.program_id(1)
    nused = nused_ref[0]
    slot = b % 2

    per_step = -(-MOE_BLOCK // nj)
    total = per_step * nj

    def row_copy(tok_ref, r, s):
        tok = tok_ref[0, jnp.minimum(r, MOE_BLOCK - 1)]
        return pltpu.make_async_copy(h_ref.at[pl.ds(tok, 1)], xbuf.at[s, pl.ds(r, 1)], sem.at[s])

    def wait_block(s):
        pltpu.make_async_copy(h_ref.at[pl.ds(0, MOE_BLOCK)], xbuf.at[s, pl.ds(0, MOE_BLOCK)],
                              sem.at[s]).wait()
        for r in range(MOE_BLOCK, total):
            pltpu.make_async_copy(h_ref.at[pl.ds(0, 1)], xbuf.at[s, pl.ds(r, 1)], sem.at[s]).wait()

    @pl.when(b < nused)
    def _():
        @pl.when(j == 0)
        def _():
            @pl.when(b == 0)
            def _():
                def issue(r, carry):
                    row_copy(tok_cur_ref, r, 0).start()
                    return carry

                lax.fori_loop(0, total, issue, 0)

            wait_block(slot)
            xs_ref[...] = xbuf[slot, 0:MOE_BLOCK].astype(BF16)
            o_ref[...] = jnp.zeros_like(o_ref)

        for u in range(per_step):
            row_copy(tok_next_ref, j * per_step + u, 1 - slot).start()

        xs = xs_ref[...]
        a = (jax.nn.silu(_dot(xs, wg_ref[...])) * _dot(xs, wu_ref[...])).astype(BF16)
        o_ref[...] += _dot(a, wd_ref[...])

    @pl.when((b == nused) & (j == 0))
    def _():
        wait_block(slot)

    @pl.when((b == n_blocks - 1) & (j == nj - 1) & (nused == n_blocks))
    def _():
        wait_block(1 - slot)

    @pl.when((b >= nused) & (j == 0))
    def _():
        o_ref[...] = jnp.zeros_like(o_ref)


def _experts(blk_e, nused, slot_tok, h, wg, wu, wd, *, n_blocks, tf):
    D = h.shape[1]
    dff = wg.shape[2]
    nj = dff // tf
    tok3 = slot_tok.reshape(n_blocks, 1, MOE_BLOCK)
    spare = -(-MOE_BLOCK // nj) * nj - MOE_BLOCK
    xrows = MOE_BLOCK + -(-spare // SUBLANES) * SUBLANES

    def jj(b, j, nu):
        return jnp.where(b < nu[0], j, nj - 1)

    def tok_spec(shift):
        return pl.BlockSpec((None, 1, MOE_BLOCK),
                            lambda b, j, be, nu: (jnp.minimum(b + shift, n_blocks - 1), 0, 0),
                            memory_space=pltpu.SMEM)

    return pl.pallas_call(
        functools.partial(_experts_kernel, n_blocks=n_blocks, nj=nj),
        grid_spec=pltpu.PrefetchScalarGridSpec(
            num_scalar_prefetch=2,
            grid=(n_blocks, nj),
            in_specs=[
                tok_spec(0), tok_spec(1),
                pl.BlockSpec(memory_space=pl.ANY),
                pl.BlockSpec((None, D, tf), lambda b, j, be, nu: (be[b], 0, jj(b, j, nu))),
                pl.BlockSpec((None, D, tf), lambda b, j, be, nu: (be[b], 0, jj(b, j, nu))),
                pl.BlockSpec((None, tf, D), lambda b, j, be, nu: (be[b], jj(b, j, nu), 0)),
            ],
            out_specs=pl.BlockSpec((MOE_BLOCK, D), lambda b, j, be, nu: (b, 0)),
            scratch_shapes=[pltpu.VMEM((2, xrows, D), F32), pltpu.VMEM((MOE_BLOCK, D), BF16),
                            pltpu.SemaphoreType.DMA((2,))],
        ),
        out_shape=jax.ShapeDtypeStruct((n_blocks * MOE_BLOCK, D), F32),
        compiler_params=_params(("arbitrary", "arbitrary")),
        name="experts",
    )(blk_e, nused, tok3, tok3, h, wg, wu, wd)


def _combine_kernel(slot_cur_ref, slot_next_ref, gate_ref, x_ref, g_ref, yb_ref, o_ref, buf, sem, *, tc):
    i = pl.program_id(0)
    slot = i % 2

    def gather(slot_ref, s):
        def issue(t8, carry):
            for u in range(SUBLANES):
                t = t8 * SUBLANES + u
                for k in range(TOP_K):
                    pltpu.make_async_copy(yb_ref.at[pl.ds(slot_ref[k, t], 1)],
                                          buf.at[s, k, pl.ds(t, 1)], sem.at[s]).start()
            return carry

        lax.fori_loop(0, tc // SUBLANES, issue, 0)

    @pl.when(i == 0)
    def _():
        gather(slot_cur_ref, 0)

    has_next = i + 1 < pl.num_programs(0)

    @pl.when(has_next & (slot == 0))
    def _():
        gather(slot_next_ref, 1)

    @pl.when(has_next & (slot == 1))
    def _():
        gather(slot_next_ref, 0)

    for k in range(TOP_K):
        pltpu.make_async_copy(yb_ref.at[pl.ds(0, tc)], buf.at[slot, k], sem.at[slot]).wait()

    gates = gate_ref[...]
    moe = gates[:, 0:1] * buf[slot, 0] + gates[:, 1:2] * buf[slot, 1]
    o_ref[...] = x_ref[...] + g_ref[0] * moe


def _combine(slots, gates_t, x2d, g, yb, *, seq, tc):
    T, D = x2d.shape
    kern = functools.partial(_combine_kernel, tc=tc)
    return pl.pallas_call(
        kern,
        grid=(T // tc,),
        in_specs=[
            pl.BlockSpec((TOP_K, tc), lambda i: (0, i), memory_space=pltpu.SMEM),
            pl.BlockSpec((TOP_K, tc), lambda i: (0, jnp.minimum(i + 1, T // tc - 1)),
                         memory_space=pltpu.SMEM),
            pl.BlockSpec((tc, N_EXPERTS), lambda i: (i, 0)),
            pl.BlockSpec((tc, D), lambda i: (i, 0)),
            pl.BlockSpec((1, 1, D), lambda i: (i // (seq // tc), 0, 0)),
            pl.BlockSpec(memory_space=pl.ANY),
        ],
        out_specs=pl.BlockSpec((tc, D), lambda i: (i, 0)),
        out_shape=jax.ShapeDtypeStruct((T, D), F32),
        scratch_shapes=[pltpu.VMEM((2, TOP_K, tc, D), F32), pltpu.SemaphoreType.DMA((2,))],
        compiler_params=_params(("arbitrary",)),
        name="combine",
    )(slots, slots, gates_t, x2d, g, yb)


def _mod_params(c, w_mod, b_mod):
    B, D = c.shape
    c_pad = jnp.zeros((SUBLANES, D), F32).at[:B].set(c)
    mod = _adaln(c_pad, w_mod, b_mod)[:B]
    return [m.reshape(B, 1, D) for m in jnp.split(mod, 6, axis=-1)]


def kernel(x, c, positions, l0_w_mod, l0_b_mod, l0_w_in, l0_pool_w, l0_pool_scale, l0_conv_w, l0_conv_b, l0_conv_ln_g, l0_conv_ln_b, l0_w_out, l0_ffn_w_gate, l0_ffn_w_up, l0_ffn_w_down, l1_w_mod, l1_b_mod, l1_w_qkv, l1_q_norm, l1_k_norm, l1_sinks, l1_w_o, l1_router_w, l1_exp_w_gate, l1_exp_w_up, l1_exp_w_down):
    B, L, D = x.shape
    T = B * L
    bf = lambda w: w.astype(BF16)

    sh1, sc1, g1, sh2, sc2, g2 = _mod_params(c, l0_w_mod, l0_b_mod)
    x = _l0_mixer(x, sh1, sc1, g1, bf(l0_w_in), bf(l0_pool_w), l0_pool_scale, l0_conv_w, l0_conv_b,
                  l0_conv_ln_g, l0_conv_ln_b, bf(l0_w_out), tm=min(256, L))
    n_e, _, dfe = l1_exp_w_gate.shape
    x, wg_e = _l0_ffn(x.reshape(T, D), sh2, sc2, g2, bf(l0_ffn_w_gate), bf(l0_ffn_w_up), bf(l0_ffn_w_down),
                      l1_exp_w_gate.reshape(n_e * D, dfe),
                      seq=L, tm=min(512, L), tf=512 if l0_ffn_w_gate.shape[1] % 512 == 0 else 256)
    x = x.reshape(B, L, D)

    sh1, sc1, g1, sh2, sc2, g2 = _mod_params(c, l1_w_mod, l1_b_mod)
    half = HEAD_DIM // 2
    inv = ROPE_THETA ** (-jnp.arange(half, dtype=F32) / half)
    inv_t = jnp.tile(inv, LANES // half).reshape(1, LANES)
    qn_t = jnp.tile(l1_q_norm, LANES // HEAD_DIM).reshape(1, LANES)
    kn_t = jnp.tile(l1_k_norm, LANES // HEAD_DIM).reshape(1, LANES)
    lane = jnp.arange(LANES)
    bd = (lane[:, None] // HEAD_DIM == lane[None, :] // HEAD_DIM).astype(BF16)
    q, k, v, wd_e = _l1_qkv(x, sh1, sc1, positions.reshape(B, L, 1), inv_t, qn_t, kn_t, bd, bf(l1_w_qkv),
                            l1_exp_w_down.reshape(n_e * dfe, D), tm=min(256, L))
    x, wu_e = _l1_attn(l1_sinks, q, k, v, x, g1, bf(l1_w_o), l1_exp_w_up.reshape(n_e * D, dfe),
                       tq=min(256, L))

    x2d = x.reshape(T, D)
    h, meta, gates, cnt = _l1_router(x2d, sh2, sc2, l1_router_w.T, seq=L, tm=min(512, L))
    counts = cnt[:, 0]
    padded = ((counts + MOE_BLOCK - 1) // MOE_BLOCK) * MOE_BLOCK
    pend = jnp.cumsum(padded)
    pstart = pend - padded
    n_blocks = (T * TOP_K + N_EXPERTS * (MOE_BLOCK - 1) + MOE_BLOCK - 1) // MOE_BLOCK
    nused = (pend[-1] // MOE_BLOCK).astype(jnp.int32).reshape(1)
    blk_start = jnp.minimum(jnp.arange(n_blocks, dtype=jnp.int32), nused[0] - 1) * MOE_BLOCK
    blk_e = jnp.minimum(jnp.sum(blk_start[:, None] >= pend[None, :], axis=1), N_EXPERTS - 1).astype(jnp.int32)
    sel = meta[:TOP_K, :, None] == jnp.arange(N_EXPERTS, dtype=jnp.int32)
    slots = meta[TOP_K:2 * TOP_K] + jnp.sum(jnp.where(sel, pstart.astype(jnp.int32), 0), axis=-1)
    slot_tok = _dispatch(slots, n_slots=n_blocks * MOE_BLOCK, td=min(2048, T))
    yb = _experts(blk_e, nused, slot_tok, h, wg_e.reshape(n_e, D, dfe), wu_e.reshape(n_e, D, dfe),
                  wd_e.reshape(n_e, dfe, D), n_blocks=n_blocks, tf=1024 if dfe % 1024 == 0 else 512)
    out = _combine(slots, gates.T, x2d, g2, yb, seq=L, tc=min(256, L))
    return out.reshape(B, L, D)
```

```python
import functools

import jax
import jax.numpy as jnp
from jax import lax
from jax.experimental import pallas as pl
from jax.experimental.pallas import tpu as pltpu

F32 = jnp.float32
BF16 = jnp.bfloat16

HEAD_DIM = 64
Q_PER_KV = 8
POOL_WINDOWS = (2, 4, 8, 16)
CONV_WIDTH = 31
WINDOW = 128
ROPE_THETA = 10000.0
N_EXPERTS = 8
TOP_K = 2
NORM_EPS = 1e-6
LN_EPS = 1e-5

LANES = 128
SUBLANES = 8
HALO = 32
MOE_BLOCK = 512
VMEM_LIMIT = 56 * 1024 * 1024
VMEM_LIMIT_ATTN = 60 * 1024 * 1024
NEG_INF = float("-inf")


def _params(sem, vmem=VMEM_LIMIT):
    return pltpu.CompilerParams(dimension_semantics=sem, vmem_limit_bytes=vmem)


def _rms_modulate(x, shift, scale):
    ms = jnp.mean(x * x, axis=-1, keepdims=True)
    return x * lax.rsqrt(ms + NORM_EPS) * (1.0 + scale) + shift


def _dot(a, b):
    return jnp.dot(a, b, preferred_element_type=F32)


def _dot_nt(a, b):
    return lax.dot_general(a, b, (((1,), (1,)), ((), ())), preferred_element_type=F32)


def _adaln_kernel(c_ref, w_ref, b_ref, o_ref):
    sc = jax.nn.silu(c_ref[...]).astype(BF16)
    o_ref[...] = _dot(sc, w_ref[...].astype(BF16)) + b_ref[...]


def _adaln(c_pad, w_mod, b_mod):
    d, n = w_mod.shape
    tn = 1024
    return pl.pallas_call(
        _adaln_kernel,
        grid=(n // tn,),
        in_specs=[
            pl.BlockSpec((SUBLANES, d), lambda j: (0, 0)),
            pl.BlockSpec((d, tn), lambda j: (0, j)),
            pl.BlockSpec((1, tn), lambda j: (0, j)),
        ],
        out_specs=pl.BlockSpec((SUBLANES, tn), lambda j: (0, j)),
        out_shape=jax.ShapeDtypeStruct((SUBLANES, n), F32),
        compiler_params=_params(("arbitrary",)),
        name="adaln",
    )(c_pad, w_mod, b_mod.reshape(1, n))


def _l0_mixer_kernel(x_ref, sh_ref, sc_ref, g_ref, w_in_ref, pool_w_ref, pool_scale_ref,
                     conv_w_ref, conv_b_ref, ln_g_ref, ln_b_ref, w_out_ref, o_ref,
                     u_ext, glu_ext, shift_ref, y_ref, mixed_ref, *, tm, d_pool, d_conv, gd):
    l = pl.program_id(1)

    @pl.when(l == 0)
    def _():
        u_ext[0:HALO, :] = jnp.zeros((HALO, d_pool), F32)
        glu_ext[0:HALO, :] = jnp.zeros((HALO, d_conv), F32)

    @pl.when(l > 0)
    def _():
        u_ext[0:HALO, :] = u_ext[tm:tm + HALO, :]
        glu_ext[0:HALO, :] = glu_ext[tm:tm + HALO, :]

    x = x_ref[0]
    h = _rms_modulate(x, sh_ref[0], sc_ref[0]).astype(BF16)
    z = _dot(h, w_in_ref[...])
    u_ext[HALO:HALO + tm, :] = z[:, :d_pool]
    glu_ext[HALO:HALO + tm, :] = z[:, d_pool:d_pool + d_conv] * jax.nn.sigmoid(z[:, d_pool + d_conv:])

    t1 = l * tm + lax.broadcasted_iota(jnp.int32, (tm, 1), 0) + 1
    for g, w in enumerate(POOL_WINDOWS):
        c0 = g * gd
        tok = u_ext[HALO:HALO + tm, c0:c0 + gd]
        s = tok
        for k in range(1, w):
            s = s + u_ext[HALO - k:HALO - k + tm, c0:c0 + gd]
        inv_cnt = 1.0 / jnp.minimum(t1, w).astype(F32)
        pooled = s * inv_cnt - tok
        mixed = _dot(pooled.astype(BF16), pool_w_ref[g]) * pool_scale_ref[:, c0:c0 + gd]
        mixed_ref[:, c0:c0 + gd] = mixed.astype(BF16)

    span = HALO + tm - SUBLANES
    for s in range(1, SUBLANES):
        shift_ref[s - 1, 0:span, :] = glu_ext[s:s + span, :]

    rows = 64
    base = HALO - (CONV_WIDTH - 1)
    for r0 in range(0, tm, rows):
        for c0 in range(0, d_conv, LANES):
            acc = jnp.broadcast_to(conv_b_ref[:, c0:c0 + LANES], (rows, LANES))
            for j in range(CONV_WIDTH):
                s = (base + j) % SUBLANES
                a0 = base + j - s + r0
                if s == 0:
                    tap = glu_ext[a0:a0 + rows, c0:c0 + LANES]
                else:
                    tap = shift_ref[s - 1, a0:a0 + rows, c0:c0 + LANES]
                acc = acc + conv_w_ref[j:j + 1, c0:c0 + LANES] * tap
            y_ref[r0:r0 + rows, c0:c0 + LANES] = acc

    y = y_ref[...]
    mu = jnp.mean(y, axis=-1, keepdims=True)
    yc = y - mu
    var = jnp.mean(yc * yc, axis=-1, keepdims=True)
    ln = yc * lax.rsqrt(var + LN_EPS) * ln_g_ref[...] + ln_b_ref[...]
    mixed_ref[:, d_pool:] = jax.nn.silu(ln).astype(BF16)

    out = _dot(mixed_ref[...], w_out_ref[...])
    o_ref[0] = x + g_ref[0] * out


def _l0_mixer(x, sh, sc, g, w_in, pool_w, pool_scale, conv_w, conv_b, ln_g, ln_b, w_out, *, tm):
    B, L, D = x.shape
    d_pool = pool_scale.shape[0]
    d_conv = conv_b.shape[0]
    gd = d_pool // len(POOL_WINDOWS)
    const2 = lambda b, l: (0, 0)
    const3 = lambda b, l: (0, 0, 0)
    per_b = pl.BlockSpec((1, 1, D), lambda b, l: (b, 0, 0))
    kern = functools.partial(_l0_mixer_kernel, tm=tm, d_pool=d_pool, d_conv=d_conv, gd=gd)
    return pl.pallas_call(
        kern,
        grid=(B, L // tm),
        in_specs=[
            pl.BlockSpec((1, tm, D), lambda b, l: (b, l, 0)),
            per_b, per_b, per_b,
            pl.BlockSpec(w_in.shape, const2, pipeline_mode=pl.Buffered(1)),
            pl.BlockSpec(pool_w.shape, const3, pipeline_mode=pl.Buffered(1)),
            pl.BlockSpec((1, d_pool), const2),
            pl.BlockSpec((CONV_WIDTH, d_conv), const2),
            pl.BlockSpec((1, d_conv), const2),
            pl.BlockSpec((1, d_conv), const2),
            pl.BlockSpec((1, d_conv), const2),
            pl.BlockSpec(w_out.shape, const2, pipeline_mode=pl.Buffered(1)),
        ],
        out_specs=pl.BlockSpec((1, tm, D), lambda b, l: (b, l, 0)),
        out_shape=jax.ShapeDtypeStruct((B, L, D), F32),
        scratch_shapes=[
            pltpu.VMEM((HALO + tm, d_pool), F32),
            pltpu.VMEM((HALO + tm, d_conv), F32),
            pltpu.VMEM((SUBLANES - 1, HALO + tm - SUBLANES, d_conv), F32),
            pltpu.VMEM((tm, d_conv), F32),
            pltpu.VMEM((tm, d_pool + d_conv), BF16),
        ],
        compiler_params=_params(("arbitrary", "arbitrary")),
        name="l0_mixer",
    )(x, sh, sc, g, w_in, pool_w, pool_scale.reshape(1, d_pool), conv_w.reshape(CONV_WIDTH, d_conv),
      conv_b.reshape(1, d_conv), ln_g.reshape(1, d_conv), ln_b.reshape(1, d_conv), w_out)


def _cast_rider(w2d, n_chunks, chunk_of):
    rows, cols = w2d.shape
    assert rows % n_chunks == 0 and (rows // n_chunks) % (2 * SUBLANES) == 0
    spec = lambda: pl.BlockSpec((rows // n_chunks, cols), lambda *g: (chunk_of(*g), 0))
    return spec(), spec(), jax.ShapeDtypeStruct(w2d.shape, BF16)


def _l0_ffn_kernel(x_ref, sh_ref, sc_ref, g_ref, wg_ref, wu_ref, wd_ref, cw_ref, o_ref, cwo_ref, h_ref):
    j = pl.program_id(1)

    @pl.when(j == 0)
    def _():
        h_ref[...] = _rms_modulate(x_ref[...], sh_ref[0], sc_ref[0]).astype(BF16)
        o_ref[...] = jnp.zeros_like(o_ref)

    h = h_ref[...]
    a = (jax.nn.silu(_dot(h, wg_ref[...])) * _dot(h, wu_ref[...])).astype(BF16)
    o_ref[...] += _dot(a, wd_ref[...])
    cwo_ref[...] = cw_ref[...].astype(BF16)

    @pl.when(j == pl.num_programs(1) - 1)
    def _():
        o_ref[...] = x_ref[...] + g_ref[0] * o_ref[...]


def _l0_ffn(x2d, sh, sc, g, wg, wu, wd, cast_w, *, seq, tm, tf):
    T, D = x2d.shape
    dff = wg.shape[1]
    nj = dff // tf
    per_b = pl.BlockSpec((1, 1, D), lambda i, j: (i // (seq // tm), 0, 0))
    jc = min(8, nj)
    cw_in, cw_out, cw_shape = _cast_rider(cast_w, (T // tm) * jc, lambda i, j: i * jc + jnp.minimum(j, jc - 1))
    return pl.pallas_call(
        _l0_ffn_kernel,
        grid=(T // tm, nj),
        in_specs=[
            pl.BlockSpec((tm, D), lambda i, j: (i, 0)),
            per_b, per_b, per_b,
            pl.BlockSpec((D, tf), lambda i, j: (0, j)),
            pl.BlockSpec((D, tf), lambda i, j: (0, j)),
            pl.BlockSpec((tf, D), lambda i, j: (j, 0)),
            cw_in,
        ],
        out_specs=[pl.BlockSpec((tm, D), lambda i, j: (i, 0)), cw_out],
        out_shape=[jax.ShapeDtypeStruct((T, D), F32), cw_shape],
        scratch_shapes=[pltpu.VMEM((tm, D), BF16)],
        compiler_params=_params(("arbitrary", "arbitrary")),
        name="l0_ffn",
    )(x2d, sh, sc, g, wg, wu, wd, cast_w)


def _l1_qkv_kernel(x_ref, sh_ref, sc_ref, pos_ref, inv_ref, qn_ref, kn_ref, bd_ref, w_ref, cw_ref,
                   q_ref, k_ref, v_ref, cwo_ref, *, n_q, n_kv):
    cwo_ref[...] = cw_ref[...].astype(BF16)
    x = x_ref[0]
    h = _rms_modulate(x, sh_ref[0], sc_ref[0]).astype(BF16)
    qkv = _dot(h, w_ref[...])

    ang = pos_ref[0].astype(F32) * inv_ref[...]
    cos = jnp.cos(ang)
    sin = jnp.sin(ang)
    lane = lax.broadcasted_iota(jnp.int32, (1, LANES), 1)
    first_half = (lane % HEAD_DIM) < (HEAD_DIM // 2)
    sin_signed = jnp.where(first_half, -sin, sin)
    low_head = lane < HEAD_DIM
    bd = bd_ref[...]

    def norm_rope(blk, nw):
        sq = blk * blk
        hi = sq.astype(BF16)
        lo = (sq - hi.astype(F32)).astype(BF16)
        ss = _dot(hi, bd) + _dot(lo, bd)
        n = blk * lax.rsqrt(ss * (1.0 / HEAD_DIM) + NORM_EPS) * nw
        partner = jnp.where(first_half, pltpu.roll(n, LANES - HEAD_DIM // 2, 1),
                            pltpu.roll(n, HEAD_DIM // 2, 1))
        return n * cos + partner * sin_signed

    def split_heads(blk):
        a_lo = jnp.where(low_head, blk, 0.0)
        b_hi = jnp.where(low_head, 0.0, blk)
        return (a_lo, pltpu.roll(a_lo, HEAD_DIM, 1), pltpu.roll(b_hi, HEAD_DIM, 1), b_hi)

    scale = HEAD_DIM ** -0.5
    for cb in range(n_q):
        blk = qkv[:, cb * LANES:(cb + 1) * LANES]
        q_ref[0, :, cb * LANES:(cb + 1) * LANES] = (norm_rope(blk, qn_ref[...]) * scale).astype(BF16)
    k0 = n_q * LANES
    v0 = k0 + n_kv * LANES
    for cb in range(n_kv):
        kr = norm_rope(qkv[:, k0 + cb * LANES:k0 + (cb + 1) * LANES], kn_ref[...])
        for i, part in enumerate(split_heads(kr)):
            k_ref[0, :, (4 * cb + i) * LANES:(4 * cb + i + 1) * LANES] = part.astype(BF16)
        vr = qkv[:, v0 + cb * LANES:v0 + (cb + 1) * LANES]
        for i, part in enumerate(split_heads(vr)):
            v_ref[0, :, (4 * cb + i) * LANES:(4 * cb + i + 1) * LANES] = part.astype(BF16)


def _l1_qkv(x, sh, sc, pos3, inv_t, qn_t, kn_t, bd, w_qkv, cast_w, *, tm):
    B, L, D = x.shape
    cw_in, cw_out, cw_shape = _cast_rider(cast_w, B * (L // tm), lambda b, l: b * (L // tm) + l)
    n_heads = D // HEAD_DIM
    n_kvh = n_heads // Q_PER_KV
    n_q = n_heads * HEAD_DIM // LANES
    n_kv = n_kvh * HEAD_DIM // LANES
    kw = n_kvh * 2 * LANES
    const2 = lambda b, l: (0, 0)
    per_b = pl.BlockSpec((1, 1, D), lambda b, l: (b, 0, 0))
    row = lambda w: pl.BlockSpec((1, tm, w), lambda b, l: (b, l, 0))
    kern = functools.partial(_l1_qkv_kernel, n_q=n_q, n_kv=n_kv)
    return pl.pallas_call(
        kern,
        grid=(B, L // tm),
        in_specs=[
            row(D), per_b, per_b, row(1),
            pl.BlockSpec((1, LANES), const2),
            pl.BlockSpec((1, LANES), const2),
            pl.BlockSpec((1, LANES), const2),
            pl.BlockSpec((LANES, LANES), const2),
            pl.BlockSpec(w_qkv.shape, const2, pipeline_mode=pl.Buffered(1)),
            cw_in,
        ],
        out_specs=[row(D), row(kw), row(kw), cw_out],
        out_shape=[jax.ShapeDtypeStruct((B, L, D), BF16),
                   jax.ShapeDtypeStruct((B, L, kw), BF16),
                   jax.ShapeDtypeStruct((B, L, kw), BF16),
                   cw_shape],
        compiler_params=_params(("arbitrary", "arbitrary")),
        name="l1_qkv",
    )(x, sh, sc, pos3, inv_t, qn_t, kn_t, bd, w_qkv, cast_w)


def _l1_attn_kernel(sinks_ref, q_ref, kc_ref, kp_ref, vc_ref, vp_ref, x_ref, g_ref, wo_ref, cw_ref,
                    o_ref, cwo_ref, attn_ref, *, tq, n_pairs):
    cwo_ref[...] = cw_ref[...].astype(BF16)
    i = pl.program_id(1)
    gp = Q_PER_KV // 2
    rows = gp * WINDOW
    qi = lax.broadcasted_iota(jnp.int32, (rows, WINDOW), 0) % WINDOW
    kj = lax.broadcasted_iota(jnp.int32, (rows, WINDOW), 1)
    pair_of_row = lax.broadcasted_iota(jnp.int32, (rows, 1), 0) // WINDOW
    mask_cur = kj <= qi
    mask_prev_band = kj > qi

    for n in range(tq // WINDOW):
        r0 = n * WINDOW
        if n == 0:
            k_prev, v_prev = kp_ref[0], vp_ref[0]
            mask_prev = kj > qi + jnp.where(i > 0, 0, WINDOW)
        else:
            k_prev, v_prev = kc_ref[0, r0 - WINDOW:r0, :], vc_ref[0, r0 - WINDOW:r0, :]
            mask_prev = mask_prev_band
        k_cur, v_cur = kc_ref[0, r0:r0 + WINDOW, :], vc_ref[0, r0:r0 + WINDOW, :]
        for kv in range(n_pairs // gp):
            p0 = kv * gp
            qs = jnp.concatenate([q_ref[0, r0:r0 + WINDOW, (p0 + j) * LANES:(p0 + j + 1) * LANES]
                                  for j in range(gp)], axis=0)
            acc = jnp.zeros((rows, LANES), F32)
            for half in range(2):
                c0 = (2 * kv + half) * LANES
                s_p = jnp.where(mask_prev, _dot_nt(qs, k_prev[:, c0:c0 + LANES]), NEG_INF)
                s_c = jnp.where(mask_cur, _dot_nt(qs, k_cur[:, c0:c0 + LANES]), NEG_INF)
                sink = jnp.full((rows, 1), sinks_ref[2 * p0 + half], F32)
                for j in range(1, gp):
                    sink = jnp.where(pair_of_row == j, sinks_ref[2 * (p0 + j) + half], sink)
                m = jnp.maximum(jnp.max(jnp.maximum(s_p, s_c), axis=1, keepdims=True), sink)
                e_p = jnp.exp(s_p - m)
                e_c = jnp.exp(s_c - m)
                denom = jnp.sum(e_p + e_c, axis=1, keepdims=True) + jnp.exp(sink - m)
                o_h = _dot(e_p.astype(BF16), v_prev[:, c0:c0 + LANES]) + \
                    _dot(e_c.astype(BF16), v_cur[:, c0:c0 + LANES])
                acc = acc + o_h * (1.0 / denom)
            for j in range(gp):
                attn_ref[r0:r0 + WINDOW, (p0 + j) * LANES:(p0 + j + 1) * LANES] = \
                    acc[j * WINDOW:(j + 1) * WINDOW].astype(BF16)

    out = _dot(attn_ref[...], wo_ref[...])
    o_ref[0] = x_ref[0] + g_ref[0] * out


def _l1_attn(sinks, q, k, v, x, g, w_o, cast_w, *, tq):
    B, L, D = x.shape
    kw = k.shape[2]
    nb = tq // WINDOW
    cur = lambda w: pl.BlockSpec((1, tq, w), lambda b, i: (b, i, 0))
    prev = pl.BlockSpec((1, WINDOW, kw), lambda b, i: (b, jnp.maximum(i * nb - 1, 0), 0))
    kern = functools.partial(_l1_attn_kernel, tq=tq, n_pairs=D // LANES)
    cw_in, cw_out, cw_shape = _cast_rider(cast_w, B * (L // tq), lambda b, i: b * (L // tq) + i)
    return pl.pallas_call(
        kern,
        grid=(B, L // tq),
        in_specs=[
            pl.BlockSpec(memory_space=pltpu.SMEM),
            cur(D), cur(kw), prev, cur(kw), prev, cur(D),
            pl.BlockSpec((1, 1, D), lambda b, i: (b, 0, 0)),
            pl.BlockSpec(w_o.shape, lambda b, i: (0, 0), pipeline_mode=pl.Buffered(1)),
            cw_in,
        ],
        out_specs=[cur(D), cw_out],
        out_shape=[jax.ShapeDtypeStruct((B, L, D), F32), cw_shape],
        scratch_shapes=[pltpu.VMEM((tq, D), BF16)],
        compiler_params=_params(("arbitrary", "arbitrary"), vmem=VMEM_LIMIT_ATTN),
        name="l1_attn",
    )(sinks, q, k, k, v, v, x, g, w_o, cast_w)


def _l1_router_kernel(x_ref, sh_ref, sc_ref, rw_ref, h_ref, meta_ref, gate_ref, cnt_ref, carry,
                      *, tm):
    i = pl.program_id(0)

    @pl.when(i == 0)
    def _():
        carry[...] = jnp.zeros_like(carry)

    h = _rms_modulate(x_ref[...], sh_ref[0], sc_ref[0])
    h_ref[...] = h

    rw = rw_ref[...]
    h_hi = h.astype(BF16)
    h_lo = (h - h_hi.astype(F32)).astype(BF16)
    w_hi = rw.astype(BF16)
    w_lo = (rw - w_hi.astype(F32)).astype(BF16)
    logits = _dot_nt(w_hi, h_hi) + (_dot_nt(w_hi, h_lo) + _dot_nt(w_lo, h_hi))

    eidx = lax.broadcasted_iota(jnp.int32, (N_EXPERTS, tm), 0)
    m1 = jnp.max(logits, axis=0, keepdims=True)
    i1 = jnp.min(jnp.where(logits == m1, eidx, N_EXPERTS), axis=0, keepdims=True)
    rest = jnp.where(eidx == i1, NEG_INF, logits)
    m2 = jnp.max(rest, axis=0, keepdims=True)
    i2 = jnp.min(jnp.where(rest == m2, eidx, N_EXPERTS), axis=0, keepdims=True)
    e2 = jnp.exp(m2 - m1)
    gate1 = 1.0 / (1.0 + e2)
    gate2 = e2 / (1.0 + e2)

    sel1 = eidx == i1
    sel2 = eidx == i2
    ind = (sel1 | sel2).astype(F32)
    before = lax.broadcasted_iota(jnp.int32, (tm, tm), 0) < lax.broadcasted_iota(jnp.int32, (tm, tm), 1)
    excl = _dot(ind.astype(BF16), before.astype(BF16)) + carry[...]
    r1 = jnp.sum(jnp.where(sel1, excl, 0.0), axis=0, keepdims=True).astype(jnp.int32)
    r2 = jnp.sum(jnp.where(sel2, excl, 0.0), axis=0, keepdims=True).astype(jnp.int32)
    carry[...] = carry[...] + jnp.sum(ind, axis=1, keepdims=True)
    cnt_ref[...] = jnp.broadcast_to(carry[...], cnt_ref.shape).astype(jnp.int32)

    meta_ref[...] = jnp.where(eidx == 0, i1, jnp.where(eidx == 1, i2, jnp.where(eidx == 2, r1,
                              jnp.where(eidx == 3, r2, 0))))
    gate_ref[...] = jnp.where(eidx == 0, gate1, jnp.where(eidx == 1, gate2, 0.0))


def _l1_router(x2d, sh, sc, rw_t, *, seq, tm):
    T, D = x2d.shape
    per_b = pl.BlockSpec((1, 1, D), lambda i: (i // (seq // tm), 0, 0))
    kern = functools.partial(_l1_router_kernel, tm=tm)
    return pl.pallas_call(
        kern,
        grid=(T // tm,),
        in_specs=[
            pl.BlockSpec((tm, D), lambda i: (i, 0)),
            per_b, per_b,
            pl.BlockSpec((N_EXPERTS, D), lambda i: (0, 0)),
        ],
        out_specs=[
            pl.BlockSpec((tm, D), lambda i: (i, 0)),
            pl.BlockSpec((N_EXPERTS, tm), lambda i: (0, i)),
            pl.BlockSpec((N_EXPERTS, tm), lambda i: (0, i)),
            pl.BlockSpec((N_EXPERTS, LANES), lambda i: (0, 0)),
        ],
        out_shape=[
            jax.ShapeDtypeStruct((T, D), F32),
            jax.ShapeDtypeStruct((N_EXPERTS, T), jnp.int32),
            jax.ShapeDtypeStruct((N_EXPERTS, T), F32),
            jax.ShapeDtypeStruct((N_EXPERTS, LANES), jnp.int32),
        ],
        scratch_shapes=[pltpu.VMEM((N_EXPERTS, 1), F32)],
        compiler_params=_params(("arbitrary",)),
        name="l1_router",
    )(x2d, sh, sc, rw_t)


def _dispatch_kernel(slot_ref, tok_ref, *, td, n_slots):
    i = pl.program_id(0)

    @pl.when(i == 0)
    def _():
        def clear(p, carry):
            tok_ref[p] = 0
            return carry

        lax.fori_loop(0, n_slots, clear, 0, unroll=8)

    def place(t, carry):
        for k in range(TOP_K):
            tok_ref[slot_ref[k, t]] = i * td + t
        return carry

    lax.fori_loop(0, td, place, 0, unroll=8)


def _dispatch(slots, *, n_slots, td):
    T = slots.shape[1]
    kern = functools.partial(_dispatch_kernel, td=td, n_slots=n_slots)
    return pl.pallas_call(
        kern,
        grid=(T // td,),
        in_specs=[pl.BlockSpec((TOP_K, td), lambda i: (0, i), memory_space=pltpu.SMEM)],
        out_specs=pl.BlockSpec(memory_space=pltpu.SMEM),
        out_shape=jax.ShapeDtypeStruct((n_slots,), jnp.int32),
        compiler_params=_params(("arbitrary",)),
        name="dispatch",
    )(slots)


def _experts_kernel(blk_e_ref, nused_ref, valid_ref, tok_cur_ref, tok_next_ref, h_ref, wg_ref, wu_ref,
                    wd_ref, o_ref, xbuf, xs_ref, sem, *, n_blocks, nj):
    b = pl.program_id(0)
    j = pl.program_id(1)
    nused = nused_ref[0]
    slot = b % 2

    per_step = -(-MOE_BLOCK // nj)
    total = per_step * nj

    def row_copy(tok_ref, r, s):
        tok = tok_ref[0, jnp.minimum(r, MOE_BLOCK - 1)]
        return pltpu.make_async_copy(h_ref.at[pl.ds(tok, 1)], xbuf.at[s, pl.ds(r, 1)], sem.at[s])

    def wait_block(s):
        pltpu.make_async_copy(h_ref.at[pl.ds(0, MOE_BLOCK)], xbuf.at[s, pl.ds(0, MOE_BLOCK)],
                              sem.at[s]).wait()
        for r in range(MOE_BLOCK, total):
            pltpu.make_async_copy(h_ref.at[pl.ds(0, 1)], xbuf.at[s, pl.ds(r, 1)], sem.at[s]).wait()

    @pl.when(b < nused)
    def _():
        @pl.when(j == 0)
        def _():
            @pl.when(b == 0)
            def _():
                def issue(r, carry):
                    row_copy(tok_cur_ref, r, 0).start()
                    return carry

                lax.fori_loop(0, total, issue, 0)

            wait_block(slot)
            xs_ref[...] = xbuf[slot, 0:MOE_BLOCK].astype(BF16)
            o_ref[...] = jnp.zeros_like(o_ref)

        def ffn_rows(n):
            for u in range(per_step):
                row_copy(tok_next_ref, j * per_step + u, 1 - slot).start()

            xs = xs_ref[0:n]
            a = (jax.nn.silu(_dot(xs, wg_ref[...])) * _dot(xs, wu_ref[...])).astype(BF16)
            o_ref[0:n] += _dot(a, wd_ref[...])

        half_full = valid_ref[b] <= MOE_BLOCK // 2

        @pl.when(jnp.logical_not(half_full))
        def _():
            ffn_rows(MOE_BLOCK)

        @pl.when(half_full)
        def _():
            ffn_rows(MOE_BLOCK // 2)

    @pl.when((b == nused) & (j == 0))
    def _():
        wait_block(slot)

    @pl.when((b == n_blocks - 1) & (j == nj - 1) & (nused == n_blocks))
    def _():
        wait_block(1 - slot)

    @pl.when((b >= nused) & (j == 0))
    def _():
        o_ref[...] = jnp.zeros_like(o_ref)


def _experts(blk_e, nused, valid, slot_tok, h, wg, wu, wd, *, n_blocks, tf):
    D = h.shape[1]
    dff = wg.shape[2]
    nj = dff // tf
    tok3 = slot_tok.reshape(n_blocks, 1, MOE_BLOCK)
    spare = -(-MOE_BLOCK // nj) * nj - MOE_BLOCK
    xrows = MOE_BLOCK + -(-spare // SUBLANES) * SUBLANES

    def jj(b, j, nu):
        return jnp.where(b < nu[0], j, nj - 1)

    def tok_spec(shift):
        return pl.BlockSpec((None, 1, MOE_BLOCK),
                            lambda b, j, be, nu, va: (jnp.minimum(b + shift, n_blocks - 1), 0, 0),
                            memory_space=pltpu.SMEM)

    return pl.pallas_call(
        functools.partial(_experts_kernel, n_blocks=n_blocks, nj=nj),
        grid_spec=pltpu.PrefetchScalarGridSpec(
            num_scalar_prefetch=3,
            grid=(n_blocks, nj),
            in_specs=[
                tok_spec(0), tok_spec(1),
                pl.BlockSpec(memory_space=pl.ANY),
                pl.BlockSpec((None, D, tf), lambda b, j, be, nu, va: (be[b], 0, jj(b, j, nu))),
                pl.BlockSpec((None, D, tf), lambda b, j, be, nu, va: (be[b], 0, jj(b, j, nu))),
                pl.BlockSpec((None, tf, D), lambda b, j, be, nu, va: (be[b], jj(b, j, nu), 0)),
            ],
            out_specs=pl.BlockSpec((MOE_BLOCK, D), lambda b, j, be, nu, va: (b, 0)),
            scratch_shapes=[pltpu.VMEM((2, xrows, D), F32), pltpu.VMEM((MOE_BLOCK, D), BF16),
                            pltpu.SemaphoreType.DMA((2,))],
        ),
        out_shape=jax.ShapeDtypeStruct((n_blocks * MOE_BLOCK, D), F32),
        compiler_params=_params(("arbitrary", "arbitrary")),
        name="experts",
    )(blk_e, nused, valid, tok3, tok3, h, wg, wu, wd)


def _combine_kernel(slot_cur_ref, slot_next_ref, gate_ref, x_ref, g_ref, yb_ref, o_ref, buf, sem, *, tc):
    i = pl.program_id(0)
    slot = i % 2

    def gather(slot_ref, s):
        def issue(t8, carry):
            for u in range(SUBLANES):
                t = t8 * SUBLANES + u
                for k in range(TOP_K):
                    pltpu.make_async_copy(yb_ref.at[pl.ds(slot_ref[k, t], 1)],
                                          buf.at[s, k, pl.ds(t, 1)], sem.at[s]).start()
            return carry

        lax.fori_loop(0, tc // SUBLANES, issue, 0)

    @pl.when(i == 0)
    def _():
        gather(slot_cur_ref, 0)

    has_next = i + 1 < pl.num_programs(0)

    @pl.when(has_next & (slot == 0))
    def _():
        gather(slot_next_ref, 1)

    @pl.when(has_next & (slot == 1))
    def _():
        gather(slot_next_ref, 0)

    for k in range(TOP_K):
        pltpu.make_async_copy(yb_ref.at[pl.ds(0, tc)], buf.at[slot, k], sem.at[slot]).wait()

    gates = gate_ref[...]
    moe = gates[:, 0:1] * buf[slot, 0] + gates[:, 1:2] * buf[slot, 1]
    o_ref[...] = x_ref[...] + g_ref[0] * moe


def _combine(slots, gates_t, x2d, g, yb, *, seq, tc):
    T, D = x2d.shape
    kern = functools.partial(_combine_kernel, tc=tc)
    return pl.pallas_call(
        kern,
        grid=(T // tc,),
        in_specs=[
            pl.BlockSpec((TOP_K, tc), lambda i: (0, i), memory_space=pltpu.SMEM),
            pl.BlockSpec((TOP_K, tc), lambda i: (0, jnp.minimum(i + 1, T // tc - 1)),
                         memory_space=pltpu.SMEM),
            pl.BlockSpec((tc, N_EXPERTS), lambda i: (i, 0)),
            pl.BlockSpec((tc, D), lambda i: (i, 0)),
            pl.BlockSpec((1, 1, D), lambda i: (i // (seq // tc), 0, 0)),
            pl.BlockSpec(memory_space=pl.ANY),
        ],
        out_specs=pl.BlockSpec((tc, D), lambda i: (i, 0)),
        out_shape=jax.ShapeDtypeStruct((T, D), F32),
        scratch_shapes=[pltpu.VMEM((2, TOP_K, tc, D), F32), pltpu.SemaphoreType.DMA((2,))],
        compiler_params=_params(("arbitrary",)),
        name="combine",
    )(slots, slots, gates_t, x2d, g, yb)


def _mod_params(c, w_mod, b_mod):
    B, D = c.shape
    c_pad = jnp.zeros((SUBLANES, D), F32).at[:B].set(c)
    mod = _adaln(c_pad, w_mod, b_mod)[:B]
    return [m.reshape(B, 1, D) for m in jnp.split(mod, 6, axis=-1)]


def kernel(x, c, positions, l0_w_mod, l0_b_mod, l0_w_in, l0_pool_w, l0_pool_scale, l0_conv_w, l0_conv_b, l0_conv_ln_g, l0_conv_ln_b, l0_w_out, l0_ffn_w_gate, l0_ffn_w_up, l0_ffn_w_down, l1_w_mod, l1_b_mod, l1_w_qkv, l1_q_norm, l1_k_norm, l1_sinks, l1_w_o, l1_router_w, l1_exp_w_gate, l1_exp_w_up, l1_exp_w_down):
    B, L, D = x.shape
    T = B * L
    bf = lambda w: w.astype(BF16)

    sh1, sc1, g1, sh2, sc2, g2 = _mod_params(c, l0_w_mod, l0_b_mod)
    x = _l0_mixer(x, sh1, sc1, g1, bf(l0_w_in), bf(l0_pool_w), l0_pool_scale, l0_conv_w, l0_conv_b,
                  l0_conv_ln_g, l0_conv_ln_b, bf(l0_w_out), tm=min(256, L))
    n_e, _, dfe = l1_exp_w_gate.shape
    x, wg_e = _l0_ffn(x.reshape(T, D), sh2, sc2, g2, bf(l0_ffn_w_gate), bf(l0_ffn_w_up), bf(l0_ffn_w_down),
                      l1_exp_w_gate.reshape(n_e * D, dfe),
                      seq=L, tm=min(512, L), tf=512 if l0_ffn_w_gate.shape[1] % 512 == 0 else 256)
    x = x.reshape(B, L, D)

    sh1, sc1, g1, sh2, sc2, g2 = _mod_params(c, l1_w_mod, l1_b_mod)
    half = HEAD_DIM // 2
    inv = ROPE_THETA ** (-jnp.arange(half, dtype=F32) / half)
    inv_t = jnp.tile(inv, LANES // half).reshape(1, LANES)
    qn_t = jnp.tile(l1_q_norm, LANES // HEAD_DIM).reshape(1, LANES)
    kn_t = jnp.tile(l1_k_norm, LANES // HEAD_DIM).reshape(1, LANES)
    lane = jnp.arange(LANES)
    bd = (lane[:, None] // HEAD_DIM == lane[None, :] // HEAD_DIM).astype(BF16)
    q, k, v, wd_e = _l1_qkv(x, sh1, sc1, positions.reshape(B, L, 1), inv_t, qn_t, kn_t, bd, bf(l1_w_qkv),
                            l1_exp_w_down.reshape(n_e * dfe, D), tm=min(256, L))
    x, wu_e = _l1_attn(l1_sinks, q, k, v, x, g1, bf(l1_w_o), l1_exp_w_up.reshape(n_e * D, dfe),
                       tq=min(256, L))

    x2d = x.reshape(T, D)
    h, meta, gates, cnt = _l1_router(x2d, sh2, sc2, l1_router_w.T, seq=L, tm=min(512, L))
    counts = cnt[:, 0]
    padded = ((counts + MOE_BLOCK - 1) // MOE_BLOCK) * MOE_BLOCK
    pend = jnp.cumsum(padded)
    pstart = pend - padded
    n_blocks = (T * TOP_K + N_EXPERTS * (MOE_BLOCK - 1) + MOE_BLOCK - 1) // MOE_BLOCK
    nused = (pend[-1] // MOE_BLOCK).astype(jnp.int32).reshape(1)
    blk_start = jnp.minimum(jnp.arange(n_blocks, dtype=jnp.int32), nused[0] - 1) * MOE_BLOCK
    blk_e = jnp.minimum(jnp.sum(blk_start[:, None] >= pend[None, :], axis=1), N_EXPERTS - 1).astype(jnp.int32)
    blk_onehot = blk_e[:, None] == jnp.arange(N_EXPERTS, dtype=jnp.int32)
    blk_end = jnp.sum(jnp.where(blk_onehot, (pstart + counts).astype(jnp.int32), 0), axis=1)
    valid = jnp.clip(blk_end - blk_start, 0, MOE_BLOCK).astype(jnp.int32)
    sel = meta[:TOP_K, :, None] == jnp.arange(N_EXPERTS, dtype=jnp.int32)
    slots = meta[TOP_K:2 * TOP_K] + jnp.sum(jnp.where(sel, pstart.astype(jnp.int32), 0), axis=-1)
    slot_tok = _dispatch(slots, n_slots=n_blocks * MOE_BLOCK, td=min(2048, T))
    yb = _experts(blk_e, nused, valid, slot_tok, h, wg_e.reshape(n_e, D, dfe), wu_e.reshape(n_e, D, dfe),
                  wd_e.reshape(n_e, dfe, D), n_blocks=n_blocks, tf=1024 if dfe % 1024 == 0 else 512)
    out = _combine(slots, gates.T, x2d, g2, yb, seq=L, tc=min(256, L))
    return out.reshape(B, L, D)
```

```python
import functools

import jax
import jax.numpy as jnp
from jax import lax
from jax.experimental import pallas as pl
from jax.experimental.pallas import tpu as pltpu

F32 = jnp.float32
BF16 = jnp.bfloat16

HEAD_DIM = 64
Q_PER_KV = 8
POOL_WINDOWS = (2, 4, 8, 16)
CONV_WIDTH = 31
WINDOW = 128
ROPE_THETA = 10000.0
N_EXPERTS = 8
TOP_K = 2
NORM_EPS = 1e-6
LN_EPS = 1e-5

LANES = 128
SUBLANES = 8
V7X_VMEM_BYTES = 64 * 1024 * 1024
VMEM_LIMIT = V7X_VMEM_BYTES - 8 * 1024 * 1024
VMEM_LIMIT_ATTN = V7X_VMEM_BYTES - 4 * 1024 * 1024
HALO = 32
MOE_BLOCK = 512
NEG_INF = float("-inf")

TN_ADALN = 1024
TM_MIXER = 256
CONV_ROWS = 64
TM_FFN = 512
TF_FFN = 512
TM_QKV = 256
TQ_ATTN = 256
TM_ROUTER = 512
TD_DISPATCH = 2048
TF_EXPERT = 1024
TC_COMBINE = 512


def _fit(preferred, extent):
    tile = min(preferred, extent)
    while extent % tile:
        tile //= 2
    return tile


def _params(sem, vmem=VMEM_LIMIT):
    return pltpu.CompilerParams(dimension_semantics=sem, vmem_limit_bytes=vmem)


def _rms_modulate(x, shift, scale):
    ms = jnp.mean(x * x, axis=-1, keepdims=True)
    return x * lax.rsqrt(ms + NORM_EPS) * (1.0 + scale) + shift


def _dot(a, b):
    return jnp.dot(a, b, preferred_element_type=F32)


def _dot_nt(a, b):
    return lax.dot_general(a, b, (((1,), (1,)), ((), ())), preferred_element_type=F32)


def _adaln_kernel(c_ref, w_ref, b_ref, o_ref):
    sc = jax.nn.silu(c_ref[...]).astype(BF16)
    o_ref[...] = _dot(sc, w_ref[...].astype(BF16)) + b_ref[...]


def _adaln(c_pad, w_mod, b_mod):
    d, n = w_mod.shape
    tn = TN_ADALN
    return pl.pallas_call(
        _adaln_kernel,
        grid=(n // tn,),
        in_specs=[
            pl.BlockSpec((SUBLANES, d), lambda j: (0, 0)),
            pl.BlockSpec((d, tn), lambda j: (0, j)),
            pl.BlockSpec((1, tn), lambda j: (0, j)),
        ],
        out_specs=pl.BlockSpec((SUBLANES, tn), lambda j: (0, j)),
        out_shape=jax.ShapeDtypeStruct((SUBLANES, n), F32),
        compiler_params=_params(("arbitrary",)),
        name="adaln",
    )(c_pad, w_mod, b_mod.reshape(1, n))


def _l0_mixer_kernel(x_ref, sh_ref, sc_ref, g_ref, w_in_ref, pool_w_ref, pool_scale_ref,
                     conv_w_ref, conv_b_ref, ln_g_ref, ln_b_ref, w_out_ref, o_ref,
                     u_ext, glu_ext, shift_ref, y_ref, mixed_ref, *, tm, d_pool, d_conv, gd):
    l = pl.program_id(1)

    @pl.when(l == 0)
    def _():
        u_ext[0:HALO, :] = jnp.zeros((HALO, d_pool), F32)
        glu_ext[0:HALO, :] = jnp.zeros((HALO, d_conv), F32)

    @pl.when(l > 0)
    def _():
        u_ext[0:HALO, :] = u_ext[tm:tm + HALO, :]
        glu_ext[0:HALO, :] = glu_ext[tm:tm + HALO, :]

    x = x_ref[0]
    h = _rms_modulate(x, sh_ref[0], sc_ref[0]).astype(BF16)
    z = _dot(h, w_in_ref[...])
    u_ext[HALO:HALO + tm, :] = z[:, :d_pool]
    glu_ext[HALO:HALO + tm, :] = z[:, d_pool:d_pool + d_conv] * jax.nn.sigmoid(z[:, d_pool + d_conv:])

    t1 = l * tm + lax.broadcasted_iota(jnp.int32, (tm, 1), 0) + 1
    for g, w in enumerate(POOL_WINDOWS):
        c0 = g * gd
        tok = u_ext[HALO:HALO + tm, c0:c0 + gd]
        s = tok
        for k in range(1, w):
            s = s + u_ext[HALO - k:HALO - k + tm, c0:c0 + gd]
        inv_cnt = 1.0 / jnp.minimum(t1, w).astype(F32)
        pooled = s * inv_cnt - tok
        mixed = _dot(pooled.astype(BF16), pool_w_ref[g]) * pool_scale_ref[:, c0:c0 + gd]
        mixed_ref[:, c0:c0 + gd] = mixed.astype(BF16)

    span = HALO + tm - SUBLANES
    for s in range(1, SUBLANES):
        shift_ref[s - 1, 0:span, :] = glu_ext[s:s + span, :]

    rows = CONV_ROWS
    base = HALO - (CONV_WIDTH - 1)
    for r0 in range(0, tm, rows):
        for c0 in range(0, d_conv, LANES):
            acc = jnp.broadcast_to(conv_b_ref[:, c0:c0 + LANES], (rows, LANES))
            for j in range(CONV_WIDTH):
                s = (base + j) % SUBLANES
                a0 = base + j - s + r0
                if s == 0:
                    tap = glu_ext[a0:a0 + rows, c0:c0 + LANES]
                else:
                    tap = shift_ref[s - 1, a0:a0 + rows, c0:c0 + LANES]
                acc = acc + conv_w_ref[j:j + 1, c0:c0 + LANES] * tap
            y_ref[r0:r0 + rows, c0:c0 + LANES] = acc

    y = y_ref[...]
    mu = jnp.mean(y, axis=-1, keepdims=True)
    yc = y - mu
    var = jnp.mean(yc * yc, axis=-1, keepdims=True)
    ln = yc * lax.rsqrt(var + LN_EPS) * ln_g_ref[...] + ln_b_ref[...]
    mixed_ref[:, d_pool:] = jax.nn.silu(ln).astype(BF16)

    out = _dot(mixed_ref[...], w_out_ref[...])
    o_ref[0] = x + g_ref[0] * out


def _l0_mixer(x, sh, sc, g, w_in, pool_w, pool_scale, conv_w, conv_b, ln_g, ln_b, w_out, *, tm):
    B, L, D = x.shape
    d_pool = pool_scale.shape[0]
    d_conv = conv_b.shape[0]
    gd = d_pool // len(POOL_WINDOWS)
    const2 = lambda b, l: (0, 0)
    const3 = lambda b, l: (0, 0, 0)
    per_b = pl.BlockSpec((1, 1, D), lambda b, l: (b, 0, 0))
    kern = functools.partial(_l0_mixer_kernel, tm=tm, d_pool=d_pool, d_conv=d_conv, gd=gd)
    return pl.pallas_call(
        kern,
        grid=(B, L // tm),
        in_specs=[
            pl.BlockSpec((1, tm, D), lambda b, l: (b, l, 0)),
            per_b, per_b, per_b,
            pl.BlockSpec(w_in.shape, const2, pipeline_mode=pl.Buffered(1)),
            pl.BlockSpec(pool_w.shape, const3, pipeline_mode=pl.Buffered(1)),
            pl.BlockSpec((1, d_pool), const2),
            pl.BlockSpec((CONV_WIDTH, d_conv), const2),
            pl.BlockSpec((1, d_conv), const2),
            pl.BlockSpec((1, d_conv), const2),
            pl.BlockSpec((1, d_conv), const2),
            pl.BlockSpec(w_out.shape, const2, pipeline_mode=pl.Buffered(1)),
        ],
        out_specs=pl.BlockSpec((1, tm, D), lambda b, l: (b, l, 0)),
        out_shape=jax.ShapeDtypeStruct((B, L, D), F32),
        scratch_shapes=[
            pltpu.VMEM((HALO + tm, d_pool), F32),
            pltpu.VMEM((HALO + tm, d_conv), F32),
            pltpu.VMEM((SUBLANES - 1, HALO + tm - SUBLANES, d_conv), F32),
            pltpu.VMEM((tm, d_conv), F32),
            pltpu.VMEM((tm, d_pool + d_conv), BF16),
        ],
        compiler_params=_params(("arbitrary", "arbitrary")),
        name="l0_mixer",
    )(x, sh, sc, g, w_in, pool_w, pool_scale.reshape(1, d_pool), conv_w.reshape(CONV_WIDTH, d_conv),
      conv_b.reshape(1, d_conv), ln_g.reshape(1, d_conv), ln_b.reshape(1, d_conv), w_out)


def _cast_rider(w2d, n_chunks, chunk_of):
    rows, cols = w2d.shape
    assert rows % n_chunks == 0 and (rows // n_chunks) % (2 * SUBLANES) == 0
    spec = lambda: pl.BlockSpec((rows // n_chunks, cols), lambda *g: (chunk_of(*g), 0))
    return spec(), spec(), jax.ShapeDtypeStruct(w2d.shape, BF16)


def _l0_ffn_kernel(x_ref, sh_ref, sc_ref, g_ref, wg_ref, wu_ref, wd_ref, cw_ref, o_ref, cwo_ref, h_ref):
    j = pl.program_id(1)

    @pl.when(j == 0)
    def _():
        h_ref[...] = _rms_modulate(x_ref[...], sh_ref[0], sc_ref[0]).astype(BF16)
        o_ref[...] = jnp.zeros_like(o_ref)

    h = h_ref[...]
    a = (jax.nn.silu(_dot(h, wg_ref[...])) * _dot(h, wu_ref[...])).astype(BF16)
    o_ref[...] += _dot(a, wd_ref[...])
    cwo_ref[...] = cw_ref[...].astype(BF16)

    @pl.when(j == pl.num_programs(1) - 1)
    def _():
        o_ref[...] = x_ref[...] + g_ref[0] * o_ref[...]


def _l0_ffn(x2d, sh, sc, g, wg, wu, wd, cast_w, *, seq, tm, tf):
    T, D = x2d.shape
    dff = wg.shape[1]
    nj = dff // tf
    per_b = pl.BlockSpec((1, 1, D), lambda i, j: (i // (seq // tm), 0, 0))
    jc = min(8, nj)
    cw_in, cw_out, cw_shape = _cast_rider(cast_w, (T // tm) * jc, lambda i, j: i * jc + jnp.minimum(j, jc - 1))
    return pl.pallas_call(
        _l0_ffn_kernel,
        grid=(T // tm, nj),
        in_specs=[
            pl.BlockSpec((tm, D), lambda i, j: (i, 0)),
            per_b, per_b, per_b,
            pl.BlockSpec((D, tf), lambda i, j: (0, j)),
            pl.BlockSpec((D, tf), lambda i, j: (0, j)),
            pl.BlockSpec((tf, D), lambda i, j: (j, 0)),
            cw_in,
        ],
        out_specs=[pl.BlockSpec((tm, D), lambda i, j: (i, 0)), cw_out],
        out_shape=[jax.ShapeDtypeStruct((T, D), F32), cw_shape],
        scratch_shapes=[pltpu.VMEM((tm, D), BF16)],
        compiler_params=_params(("arbitrary", "arbitrary")),
        name="l0_ffn",
    )(x2d, sh, sc, g, wg, wu, wd, cast_w)


def _l1_qkv_kernel(x_ref, sh_ref, sc_ref, pos_ref, inv_ref, qn_ref, kn_ref, bd_ref, w_ref, cw_ref,
                   q_ref, k_ref, v_ref, cwo_ref, *, n_q, n_kv):
    cwo_ref[...] = cw_ref[...].astype(BF16)
    x = x_ref[0]
    h = _rms_modulate(x, sh_ref[0], sc_ref[0]).astype(BF16)
    qkv = _dot(h, w_ref[...])

    ang = pos_ref[0].astype(F32) * inv_ref[...]
    cos = jnp.cos(ang)
    sin = jnp.sin(ang)
    lane = lax.broadcasted_iota(jnp.int32, (1, LANES), 1)
    first_half = (lane % HEAD_DIM) < (HEAD_DIM // 2)
    sin_signed = jnp.where(first_half, -sin, sin)
    low_head = lane < HEAD_DIM
    bd = bd_ref[...]

    def norm_rope(blk, nw):
        sq = blk * blk
        hi = sq.astype(BF16)
        lo = (sq - hi.astype(F32)).astype(BF16)
        ss = _dot(hi, bd) + _dot(lo, bd)
        n = blk * lax.rsqrt(ss * (1.0 / HEAD_DIM) + NORM_EPS) * nw
        partner = jnp.where(first_half, pltpu.roll(n, LANES - HEAD_DIM // 2, 1),
                            pltpu.roll(n, HEAD_DIM // 2, 1))
        return n * cos + partner * sin_signed

    def split_heads(blk):
        a_lo = jnp.where(low_head, blk, 0.0)
        b_hi = jnp.where(low_head, 0.0, blk)
        return (a_lo, pltpu.roll(a_lo, HEAD_DIM, 1), pltpu.roll(b_hi, HEAD_DIM, 1), b_hi)

    scale = HEAD_DIM ** -0.5
    for cb in range(n_q):
        blk = qkv[:, cb * LANES:(cb + 1) * LANES]
        q_ref[0, :, cb * LANES:(cb + 1) * LANES] = (norm_rope(blk, qn_ref[...]) * scale).astype(BF16)
    k0 = n_q * LANES
    v0 = k0 + n_kv * LANES
    for cb in range(n_kv):
        kr = norm_rope(qkv[:, k0 + cb * LANES:k0 + (cb + 1) * LANES], kn_ref[...])
        for i, part in enumerate(split_heads(kr)):
            k_ref[0, :, (4 * cb + i) * LANES:(4 * cb + i + 1) * LANES] = part.astype(BF16)
        vr = qkv[:, v0 + cb * LANES:v0 + (cb + 1) * LANES]
        for i, part in enumerate(split_heads(vr)):
            v_ref[0, :, (4 * cb + i) * LANES:(4 * cb + i + 1) * LANES] = part.astype(BF16)


def _l1_qkv(x, sh, sc, pos3, inv_t, qn_t, kn_t, bd, w_qkv, cast_w, *, tm):
    B, L, D = x.shape
    cw_in, cw_out, cw_shape = _cast_rider(cast_w, B * (L // tm), lambda b, l: b * (L // tm) + l)
    n_heads = D // HEAD_DIM
    n_kvh = n_heads // Q_PER_KV
    n_q = n_heads * HEAD_DIM // LANES
    n_kv = n_kvh * HEAD_DIM // LANES
    kw = n_kvh * 2 * LANES
    const2 = lambda b, l: (0, 0)
    per_b = pl.BlockSpec((1, 1, D), lambda b, l: (b, 0, 0))
    row = lambda w: pl.BlockSpec((1, tm, w), lambda b, l: (b, l, 0))
    kern = functools.partial(_l1_qkv_kernel, n_q=n_q, n_kv=n_kv)
    return pl.pallas_call(
        kern,
        grid=(B, L // tm),
        in_specs=[
            row(D), per_b, per_b, row(1),
            pl.BlockSpec((1, LANES), const2),
            pl.BlockSpec((1, LANES), const2),
            pl.BlockSpec((1, LANES), const2),
            pl.BlockSpec((LANES, LANES), const2),
            pl.BlockSpec(w_qkv.shape, const2, pipeline_mode=pl.Buffered(1)),
            cw_in,
        ],
        out_specs=[row(D), row(kw), row(kw), cw_out],
        out_shape=[jax.ShapeDtypeStruct((B, L, D), BF16),
                   jax.ShapeDtypeStruct((B, L, kw), BF16),
                   jax.ShapeDtypeStruct((B, L, kw), BF16),
                   cw_shape],
        compiler_params=_params(("arbitrary", "arbitrary")),
        name="l1_qkv",
    )(x, sh, sc, pos3, inv_t, qn_t, kn_t, bd, w_qkv, cast_w)


def _l1_attn_kernel(sinks_ref, q_ref, kc_ref, kp_ref, vc_ref, vp_ref, x_ref, g_ref, wo_ref, cw_ref,
                    o_ref, cwo_ref, attn_ref, *, tq, n_pairs):
    cwo_ref[...] = cw_ref[...].astype(BF16)
    i = pl.program_id(1)
    gp = Q_PER_KV // 2
    rows = gp * WINDOW
    qi = lax.broadcasted_iota(jnp.int32, (rows, WINDOW), 0) % WINDOW
    kj = lax.broadcasted_iota(jnp.int32, (rows, WINDOW), 1)
    pair_of_row = lax.broadcasted_iota(jnp.int32, (rows, 1), 0) // WINDOW
    mask_cur = kj <= qi
    mask_prev_band = kj > qi

    for n in range(tq // WINDOW):
        r0 = n * WINDOW
        if n == 0:
            k_prev, v_prev = kp_ref[0], vp_ref[0]
            mask_prev = kj > qi + jnp.where(i > 0, 0, WINDOW)
        else:
            k_prev, v_prev = kc_ref[0, r0 - WINDOW:r0, :], vc_ref[0, r0 - WINDOW:r0, :]
            mask_prev = mask_prev_band
        k_cur, v_cur = kc_ref[0, r0:r0 + WINDOW, :], vc_ref[0, r0:r0 + WINDOW, :]
        for kv in range(n_pairs // gp):
            p0 = kv * gp
            qs = jnp.concatenate([q_ref[0, r0:r0 + WINDOW, (p0 + j) * LANES:(p0 + j + 1) * LANES]
                                  for j in range(gp)], axis=0)
            acc = jnp.zeros((rows, LANES), F32)
            for half in range(2):
                c0 = (2 * kv + half) * LANES
                s_p = jnp.where(mask_prev, _dot_nt(qs, k_prev[:, c0:c0 + LANES]), NEG_INF)
                s_c = jnp.where(mask_cur, _dot_nt(qs, k_cur[:, c0:c0 + LANES]), NEG_INF)
                sink = jnp.full((rows, 1), sinks_ref[2 * p0 + half], F32)
                for j in range(1, gp):
                    sink = jnp.where(pair_of_row == j, sinks_ref[2 * (p0 + j) + half], sink)
                m = jnp.maximum(jnp.max(jnp.maximum(s_p, s_c), axis=1, keepdims=True), sink)
                e_p = jnp.exp(s_p - m)
                e_c = jnp.exp(s_c - m)
                denom = jnp.sum(e_p + e_c, axis=1, keepdims=True) + jnp.exp(sink - m)
                o_h = _dot(e_p.astype(BF16), v_prev[:, c0:c0 + LANES]) + \
                    _dot(e_c.astype(BF16), v_cur[:, c0:c0 + LANES])
                acc = acc + o_h * (1.0 / denom)
            for j in range(gp):
                attn_ref[r0:r0 + WINDOW, (p0 + j) * LANES:(p0 + j + 1) * LANES] = \
                    acc[j * WINDOW:(j + 1) * WINDOW].astype(BF16)

    out = _dot(attn_ref[...], wo_ref[...])
    o_ref[0] = x_ref[0] + g_ref[0] * out


def _l1_attn(sinks, q, k, v, x, g, w_o, cast_w, *, tq):
    B, L, D = x.shape
    kw = k.shape[2]
    nb = tq // WINDOW
    cur = lambda w: pl.BlockSpec((1, tq, w), lambda b, i: (b, i, 0))
    prev = pl.BlockSpec((1, WINDOW, kw), lambda b, i: (b, jnp.maximum(i * nb - 1, 0), 0))
    kern = functools.partial(_l1_attn_kernel, tq=tq, n_pairs=D // LANES)
    cw_in, cw_out, cw_shape = _cast_rider(cast_w, B * (L // tq), lambda b, i: b * (L // tq) + i)
    return pl.pallas_call(
        kern,
        grid=(B, L // tq),
        in_specs=[
            pl.BlockSpec(memory_space=pltpu.SMEM),
            cur(D), cur(kw), prev, cur(kw), prev, cur(D),
            pl.BlockSpec((1, 1, D), lambda b, i: (b, 0, 0)),
            pl.BlockSpec(w_o.shape, lambda b, i: (0, 0), pipeline_mode=pl.Buffered(1)),
            cw_in,
        ],
        out_specs=[cur(D), cw_out],
        out_shape=[jax.ShapeDtypeStruct((B, L, D), F32), cw_shape],
        scratch_shapes=[pltpu.VMEM((tq, D), BF16)],
        compiler_params=_params(("arbitrary", "arbitrary"), vmem=VMEM_LIMIT_ATTN),
        name="l1_attn",
    )(sinks, q, k, k, v, v, x, g, w_o, cast_w)


def _l1_router_kernel(x_ref, sh_ref, sc_ref, rw_ref, h_ref, meta_ref, gate_ref, cnt_ref, carry,
                      *, tm):
    i = pl.program_id(0)

    @pl.when(i == 0)
    def _():
        carry[...] = jnp.zeros_like(carry)

    h = _rms_modulate(x_ref[...], sh_ref[0], sc_ref[0])
    h_ref[...] = h

    rw = rw_ref[...]
    h_hi = h.astype(BF16)
    h_lo = (h - h_hi.astype(F32)).astype(BF16)
    w_hi = rw.astype(BF16)
    w_lo = (rw - w_hi.astype(F32)).astype(BF16)
    logits = _dot_nt(w_hi, h_hi) + (_dot_nt(w_hi, h_lo) + _dot_nt(w_lo, h_hi))

    eidx = lax.broadcasted_iota(jnp.int32, (N_EXPERTS, tm), 0)
    m1 = jnp.max(logits, axis=0, keepdims=True)
    i1 = jnp.min(jnp.where(logits == m1, eidx, N_EXPERTS), axis=0, keepdims=True)
    rest = jnp.where(eidx == i1, NEG_INF, logits)
    m2 = jnp.max(rest, axis=0, keepdims=True)
    i2 = jnp.min(jnp.where(rest == m2, eidx, N_EXPERTS), axis=0, keepdims=True)
    e2 = jnp.exp(m2 - m1)
    gate1 = 1.0 / (1.0 + e2)
    gate2 = e2 / (1.0 + e2)

    sel1 = eidx == i1
    sel2 = eidx == i2
    ind = (sel1 | sel2).astype(F32)
    before = lax.broadcasted_iota(jnp.int32, (tm, tm), 0) < lax.broadcasted_iota(jnp.int32, (tm, tm), 1)
    excl = _dot(ind.astype(BF16), before.astype(BF16)) + carry[...]
    r1 = jnp.sum(jnp.where(sel1, excl, 0.0), axis=0, keepdims=True).astype(jnp.int32)
    r2 = jnp.sum(jnp.where(sel2, excl, 0.0), axis=0, keepdims=True).astype(jnp.int32)
    carry[...] = carry[...] + jnp.sum(ind, axis=1, keepdims=True)
    cnt_ref[...] = jnp.broadcast_to(carry[...], cnt_ref.shape).astype(jnp.int32)

    meta_ref[...] = jnp.where(eidx == 0, i1, jnp.where(eidx == 1, i2, jnp.where(eidx == 2, r1,
                              jnp.where(eidx == 3, r2, 0))))
    gate_ref[...] = jnp.where(eidx == 0, gate1, jnp.where(eidx == 1, gate2, 0.0))


def _l1_router(x2d, sh, sc, rw_t, *, seq, tm):
    T, D = x2d.shape
    per_b = pl.BlockSpec((1, 1, D), lambda i: (i // (seq // tm), 0, 0))
    kern = functools.partial(_l1_router_kernel, tm=tm)
    return pl.pallas_call(
        kern,
        grid=(T // tm,),
        in_specs=[
            pl.BlockSpec((tm, D), lambda i: (i, 0)),
            per_b, per_b,
            pl.BlockSpec((N_EXPERTS, D), lambda i: (0, 0)),
        ],
        out_specs=[
            pl.BlockSpec((tm, D), lambda i: (i, 0)),
            pl.BlockSpec((N_EXPERTS, tm), lambda i: (0, i)),
            pl.BlockSpec((N_EXPERTS, tm), lambda i: (0, i)),
            pl.BlockSpec((N_EXPERTS, LANES), lambda i: (0, 0)),
        ],
        out_shape=[
            jax.ShapeDtypeStruct((T, D), F32),
            jax.ShapeDtypeStruct((N_EXPERTS, T), jnp.int32),
            jax.ShapeDtypeStruct((N_EXPERTS, T), F32),
            jax.ShapeDtypeStruct((N_EXPERTS, LANES), jnp.int32),
        ],
        scratch_shapes=[pltpu.VMEM((N_EXPERTS, 1), F32)],
        compiler_params=_params(("arbitrary",)),
        name="l1_router",
    )(x2d, sh, sc, rw_t)


def _dispatch_kernel(pad_lo_ref, pad_hi_ref, slot_ref, tok_ref, *, td):
    i = pl.program_id(0)

    @pl.when(i == 0)
    def _():
        def clear(p, carry):
            tok_ref[p] = 0
            return carry

        for e in range(N_EXPERTS + 1):
            lax.fori_loop(pad_lo_ref[e], pad_hi_ref[e], clear, 0)

    def place(t, carry):
        for k in range(TOP_K):
            tok_ref[slot_ref[k, t]] = i * td + t
        return carry

    lax.fori_loop(0, td, place, 0, unroll=8)


def _dispatch(pad_lo, pad_hi, slots, *, n_slots, td):
    T = slots.shape[1]
    kern = functools.partial(_dispatch_kernel, td=td)
    return pl.pallas_call(
        kern,
        grid_spec=pltpu.PrefetchScalarGridSpec(
            num_scalar_prefetch=2,
            grid=(T // td,),
            in_specs=[pl.BlockSpec((TOP_K, td), lambda i, *_: (0, i), memory_space=pltpu.SMEM)],
            out_specs=pl.BlockSpec(memory_space=pltpu.SMEM),
        ),
        out_shape=jax.ShapeDtypeStruct((n_slots,), jnp.int32),
        compiler_params=_params(("arbitrary",)),
        name="dispatch",
    )(pad_lo, pad_hi, slots)


def _experts_kernel(blk_e_ref, nused_ref, valid_ref, tok_cur_ref, tok_next_ref, h_ref, wg_ref, wu_ref,
                    wd_ref, o_ref, xbuf, xs_ref, sem, *, n_blocks, nj):
    b = pl.program_id(0)
    j = pl.program_id(1)
    nused = nused_ref[0]
    slot = b % 2

    per_step = -(-MOE_BLOCK // nj)
    total = per_step * nj

    def row_copy(tok_ref, r, s):
        tok = tok_ref[0, jnp.minimum(r, MOE_BLOCK - 1)]
        return pltpu.make_async_copy(h_ref.at[pl.ds(tok, 1)], xbuf.at[s, pl.ds(r, 1)], sem.at[s])

    def wait_block(s):
        pltpu.make_async_copy(h_ref.at[pl.ds(0, MOE_BLOCK)], xbuf.at[s, pl.ds(0, MOE_BLOCK)],
                              sem.at[s]).wait()
        for r in range(MOE_BLOCK, total):
            pltpu.make_async_copy(h_ref.at[pl.ds(0, 1)], xbuf.at[s, pl.ds(r, 1)], sem.at[s]).wait()

    @pl.when(b < nused)
    def _():
        @pl.when(j == 0)
        def _():
            @pl.when(b == 0)
            def _():
                def issue(r, carry):
                    row_copy(tok_cur_ref, r, 0).start()
                    return carry

                lax.fori_loop(0, total, issue, 0)

            wait_block(slot)
            xs_ref[...] = xbuf[slot, 0:MOE_BLOCK].astype(BF16)
            o_ref[...] = jnp.zeros_like(o_ref)

        def ffn_rows(n):
            for u in range(per_step):
                row_copy(tok_next_ref, j * per_step + u, 1 - slot).start()

            xs = xs_ref[0:n]
            a = (jax.nn.silu(_dot(xs, wg_ref[...])) * _dot(xs, wu_ref[...])).astype(BF16)
            o_ref[0:n] += _dot(a, wd_ref[...])

        half_full = valid_ref[b] <= MOE_BLOCK // 2

        @pl.when(jnp.logical_not(half_full))
        def _():
            ffn_rows(MOE_BLOCK)

        @pl.when(half_full)
        def _():
            ffn_rows(MOE_BLOCK // 2)

    @pl.when((b == nused) & (j == 0))
    def _():
        wait_block(slot)

    @pl.when((b == n_blocks - 1) & (j == nj - 1) & (nused == n_blocks))
    def _():
        wait_block(1 - slot)

    @pl.when((b >= nused) & (j == 0))
    def _():
        o_ref[...] = jnp.zeros_like(o_ref)


def _experts(blk_e, nused, valid, slot_tok, h, wg, wu, wd, *, n_blocks, tf):
    D = h.shape[1]
    dff = wg.shape[2]
    nj = dff // tf
    tok3 = slot_tok.reshape(n_blocks, 1, MOE_BLOCK)
    spare = -(-MOE_BLOCK // nj) * nj - MOE_BLOCK
    xrows = MOE_BLOCK + -(-spare // SUBLANES) * SUBLANES

    def jj(b, j, nu):
        return jnp.where(b < nu[0], j, nj - 1)

    def tok_spec(shift):
        return pl.BlockSpec((None, 1, MOE_BLOCK),
                            lambda b, j, be, nu, va: (jnp.minimum(b + shift, n_blocks - 1), 0, 0),
                            memory_space=pltpu.SMEM)

    return pl.pallas_call(
        functools.partial(_experts_kernel, n_blocks=n_blocks, nj=nj),
        grid_spec=pltpu.PrefetchScalarGridSpec(
            num_scalar_prefetch=3,
            grid=(n_blocks, nj),
            in_specs=[
                tok_spec(0), tok_spec(1),
                pl.BlockSpec(memory_space=pl.ANY),
                pl.BlockSpec((None, D, tf), lambda b, j, be, nu, va: (be[b], 0, jj(b, j, nu))),
                pl.BlockSpec((None, D, tf), lambda b, j, be, nu, va: (be[b], 0, jj(b, j, nu))),
                pl.BlockSpec((None, tf, D), lambda b, j, be, nu, va: (be[b], jj(b, j, nu), 0)),
            ],
            out_specs=pl.BlockSpec((MOE_BLOCK, D), lambda b, j, be, nu, va: (b, 0)),
            scratch_shapes=[pltpu.VMEM((2, xrows, D), F32), pltpu.VMEM((MOE_BLOCK, D), BF16),
                            pltpu.SemaphoreType.DMA((2,))],
        ),
        out_shape=jax.ShapeDtypeStruct((n_blocks * MOE_BLOCK, D), F32),
        compiler_params=_params(("arbitrary", "arbitrary")),
        name="experts",
    )(blk_e, nused, valid, tok3, tok3, h, wg, wu, wd)


def _combine_kernel(slot_cur_ref, slot_next_ref, gate_ref, x_ref, g_ref, yb_ref, o_ref, buf, sem, *, tc):
    i = pl.program_id(0)
    slot = i % 2

    def gather(slot_ref, s):
        def issue(t8, carry):
            for u in range(SUBLANES):
                t = t8 * SUBLANES + u
                for k in range(TOP_K):
                    pltpu.make_async_copy(yb_ref.at[pl.ds(slot_ref[k, t], 1)],
                                          buf.at[s, k, pl.ds(t, 1)], sem.at[s]).start()
            return carry

        lax.fori_loop(0, tc // SUBLANES, issue, 0)

    @pl.when(i == 0)
    def _():
        gather(slot_cur_ref, 0)

    has_next = i + 1 < pl.num_programs(0)

    @pl.when(has_next & (slot == 0))
    def _():
        gather(slot_next_ref, 1)

    @pl.when(has_next & (slot == 1))
    def _():
        gather(slot_next_ref, 0)

    for k in range(TOP_K):
        pltpu.make_async_copy(yb_ref.at[pl.ds(0, tc)], buf.at[slot, k], sem.at[slot]).wait()

    gates = gate_ref[...]
    moe = gates[:, 0:1] * buf[slot, 0] + gates[:, 1:2] * buf[slot, 1]
    o_ref[...] = x_ref[...] + g_ref[0] * moe


def _combine(slots, gates_t, x2d, g, yb, *, seq, tc):
    T, D = x2d.shape
    kern = functools.partial(_combine_kernel, tc=tc)
    return pl.pallas_call(
        kern,
        grid=(T // tc,),
        in_specs=[
            pl.BlockSpec((TOP_K, tc), lambda i: (0, i), memory_space=pltpu.SMEM),
            pl.BlockSpec((TOP_K, tc), lambda i: (0, jnp.minimum(i + 1, T // tc - 1)),
                         memory_space=pltpu.SMEM),
            pl.BlockSpec((tc, N_EXPERTS), lambda i: (i, 0)),
            pl.BlockSpec((tc, D), lambda i: (i, 0)),
            pl.BlockSpec((1, 1, D), lambda i: (i // (seq // tc), 0, 0)),
            pl.BlockSpec(memory_space=pl.ANY),
        ],
        out_specs=pl.BlockSpec((tc, D), lambda i: (i, 0)),
        out_shape=jax.ShapeDtypeStruct((T, D), F32),
        scratch_shapes=[pltpu.VMEM((2, TOP_K, tc, D), F32), pltpu.SemaphoreType.DMA((2,))],
        compiler_params=_params(("arbitrary",)),
        name="combine",
    )(slots, slots, gates_t, x2d, g, yb)


def _mod_params(c, w_mod, b_mod):
    B, D = c.shape
    c_pad = jnp.zeros((SUBLANES, D), F32).at[:B].set(c)
    mod = _adaln(c_pad, w_mod, b_mod)[:B]
    return [m.reshape(B, 1, D) for m in jnp.split(mod, 6, axis=-1)]


def kernel(x, c, positions, l0_w_mod, l0_b_mod, l0_w_in, l0_pool_w, l0_pool_scale, l0_conv_w, l0_conv_b, l0_conv_ln_g, l0_conv_ln_b, l0_w_out, l0_ffn_w_gate, l0_ffn_w_up, l0_ffn_w_down, l1_w_mod, l1_b_mod, l1_w_qkv, l1_q_norm, l1_k_norm, l1_sinks, l1_w_o, l1_router_w, l1_exp_w_gate, l1_exp_w_up, l1_exp_w_down):
    B, L, D = x.shape
    T = B * L
    bf = lambda w: w.astype(BF16)

    sh1, sc1, g1, sh2, sc2, g2 = _mod_params(c, l0_w_mod, l0_b_mod)
    x = _l0_mixer(x, sh1, sc1, g1, bf(l0_w_in), bf(l0_pool_w), l0_pool_scale, l0_conv_w, l0_conv_b,
                  l0_conv_ln_g, l0_conv_ln_b, bf(l0_w_out), tm=_fit(TM_MIXER, L))
    n_e, _, dfe = l1_exp_w_gate.shape
    x, wg_e = _l0_ffn(x.reshape(T, D), sh2, sc2, g2, bf(l0_ffn_w_gate), bf(l0_ffn_w_up), bf(l0_ffn_w_down),
                      l1_exp_w_gate.reshape(n_e * D, dfe),
                      seq=L, tm=_fit(TM_FFN, L), tf=_fit(TF_FFN, l0_ffn_w_gate.shape[1]))
    x = x.reshape(B, L, D)

    sh1, sc1, g1, sh2, sc2, g2 = _mod_params(c, l1_w_mod, l1_b_mod)
    half = HEAD_DIM // 2
    inv = ROPE_THETA ** (-jnp.arange(half, dtype=F32) / half)
    inv_t = jnp.tile(inv, LANES // half).reshape(1, LANES)
    qn_t = jnp.tile(l1_q_norm, LANES // HEAD_DIM).reshape(1, LANES)
    kn_t = jnp.tile(l1_k_norm, LANES // HEAD_DIM).reshape(1, LANES)
    lane = jnp.arange(LANES)
    bd = (lane[:, None] // HEAD_DIM == lane[None, :] // HEAD_DIM).astype(BF16)
    q, k, v, wd_e = _l1_qkv(x, sh1, sc1, positions.reshape(B, L, 1), inv_t, qn_t, kn_t, bd, bf(l1_w_qkv),
                            l1_exp_w_down.reshape(n_e * dfe, D), tm=_fit(TM_QKV, L))
    x, wu_e = _l1_attn(l1_sinks, q, k, v, x, g1, bf(l1_w_o), l1_exp_w_up.reshape(n_e * D, dfe),
                       tq=_fit(TQ_ATTN, L))

    x2d = x.reshape(T, D)
    h, meta, gates, cnt = _l1_router(x2d, sh2, sc2, l1_router_w.T, seq=L, tm=_fit(TM_ROUTER, L))
    counts = cnt[:, 0]
    padded = ((counts + MOE_BLOCK - 1) // MOE_BLOCK) * MOE_BLOCK
    pend = jnp.cumsum(padded)
    pstart = pend - padded
    n_blocks = (T * TOP_K + N_EXPERTS * (MOE_BLOCK - 1) + MOE_BLOCK - 1) // MOE_BLOCK
    nused = (pend[-1] // MOE_BLOCK).astype(jnp.int32).reshape(1)
    blk_start = jnp.minimum(jnp.arange(n_blocks, dtype=jnp.int32), nused[0] - 1) * MOE_BLOCK
    blk_e = jnp.minimum(jnp.sum(blk_start[:, None] >= pend[None, :], axis=1), N_EXPERTS - 1).astype(jnp.int32)
    blk_onehot = blk_e[:, None] == jnp.arange(N_EXPERTS, dtype=jnp.int32)
    blk_end = jnp.sum(jnp.where(blk_onehot, (pstart + counts).astype(jnp.int32), 0), axis=1)
    valid = jnp.clip(blk_end - blk_start, 0, MOE_BLOCK).astype(jnp.int32)
    sel = meta[:TOP_K, :, None] == jnp.arange(N_EXPERTS, dtype=jnp.int32)
    slots = meta[TOP_K:2 * TOP_K] + jnp.sum(jnp.where(sel, pstart.astype(jnp.int32), 0), axis=-1)
    n_slots = n_blocks * MOE_BLOCK
    pad_lo = jnp.concatenate([pstart + counts, pend[-1:]]).astype(jnp.int32)
    pad_hi = jnp.concatenate([pend, jnp.full((1,), n_slots, pend.dtype)]).astype(jnp.int32)
    slot_tok = _dispatch(pad_lo, pad_hi, slots, n_slots=n_slots, td=_fit(TD_DISPATCH, T))
    yb = _experts(blk_e, nused, valid, slot_tok, h, wg_e.reshape(n_e, D, dfe), wu_e.reshape(n_e, D, dfe),
                  wd_e.reshape(n_e, dfe, D), n_blocks=n_blocks, tf=_fit(TF_EXPERT, dfe))
    out = _combine(slots, gates.T, x2d, g2, yb, seq=L, tc=_fit(TC_COMBINE, L))
    return out.reshape(B, L, D)
```

```python
import functools

import jax
import jax.numpy as jnp
from jax import lax
from jax.experimental import pallas as pl
from jax.experimental.pallas import tpu as pltpu

F32 = jnp.float32
BF16 = jnp.bfloat16

HEAD_DIM = 64
Q_PER_KV = 8
POOL_WINDOWS = (2, 4, 8, 16)
CONV_WIDTH = 31
WINDOW = 128
ROPE_THETA = 10000.0
N_EXPERTS = 8
TOP_K = 2
NORM_EPS = 1e-6
LN_EPS = 1e-5

LANES = 128
SUBLANES = 8
V7X_VMEM_BYTES = 64 * 1024 * 1024
VMEM_LIMIT = V7X_VMEM_BYTES - 8 * 1024 * 1024
VMEM_LIMIT_ATTN = V7X_VMEM_BYTES - 4 * 1024 * 1024
HALO = 32
MOE_BLOCK = 512
NEG_INF = float("-inf")

TN_ADALN = 1024
TM_MIXER = 256
CONV_ROWS = 64
TM_FFN = 512
TF_FFN = 512
TM_QKV = 256
TQ_ATTN = 256
TM_ROUTER = 512
TD_DISPATCH = 2048
TF_EXPERT = 1024
TC_COMBINE = 512


def _fit(preferred, extent):
    tile = min(preferred, extent)
    while extent % tile:
        tile //= 2
    return tile


def _params(sem, vmem=VMEM_LIMIT):
    return pltpu.CompilerParams(dimension_semantics=sem, vmem_limit_bytes=vmem)


def _rms_modulate(x, shift, scale):
    ms = jnp.mean(x * x, axis=-1, keepdims=True)
    return x * lax.rsqrt(ms + NORM_EPS) * (1.0 + scale) + shift


def _dot(a, b):
    return jnp.dot(a, b, preferred_element_type=F32)


def _dot_nt(a, b):
    return lax.dot_general(a, b, (((1,), (1,)), ((), ())), preferred_element_type=F32)


def _adaln_kernel(c_ref, w_ref, b_ref, o_ref):
    sc = jax.nn.silu(c_ref[...]).astype(BF16)
    o_ref[...] = _dot(sc, w_ref[...].astype(BF16)) + b_ref[...]


def _adaln(c_pad, w_mod, b_mod):
    d, n = w_mod.shape
    tn = TN_ADALN
    return pl.pallas_call(
        _adaln_kernel,
        grid=(n // tn,),
        in_specs=[
            pl.BlockSpec((SUBLANES, d), lambda j: (0, 0)),
            pl.BlockSpec((d, tn), lambda j: (0, j)),
            pl.BlockSpec((1, tn), lambda j: (0, j)),
        ],
        out_specs=pl.BlockSpec((SUBLANES, tn), lambda j: (0, j)),
        out_shape=jax.ShapeDtypeStruct((SUBLANES, n), F32),
        compiler_params=_params(("arbitrary",)),
        name="adaln",
    )(c_pad, w_mod, b_mod.reshape(1, n))


def _l0_mixer_kernel(x_ref, sh_ref, sc_ref, g_ref, w_in_ref, pool_w_ref, pool_scale_ref,
                     conv_w_ref, conv_b_ref, ln_g_ref, ln_b_ref, w_out_ref, o_ref,
                     u_ext, glu_ext, shift_ref, y_ref, mixed_ref, *, tm, d_pool, d_conv, gd):
    l = pl.program_id(1)

    @pl.when(l == 0)
    def _():
        u_ext[0:HALO, :] = jnp.zeros((HALO, d_pool), F32)
        glu_ext[0:HALO, :] = jnp.zeros((HALO, d_conv), F32)

    @pl.when(l > 0)
    def _():
        u_ext[0:HALO, :] = u_ext[tm:tm + HALO, :]
        glu_ext[0:HALO, :] = glu_ext[tm:tm + HALO, :]

    x = x_ref[0]
    h = _rms_modulate(x, sh_ref[0], sc_ref[0]).astype(BF16)
    z = _dot(h, w_in_ref[...])
    u_ext[HALO:HALO + tm, :] = z[:, :d_pool]
    glu_ext[HALO:HALO + tm, :] = z[:, d_pool:d_pool + d_conv] * jax.nn.sigmoid(z[:, d_pool + d_conv:])

    t1 = l * tm + lax.broadcasted_iota(jnp.int32, (tm, 1), 0) + 1
    for g, w in enumerate(POOL_WINDOWS):
        c0 = g * gd
        tok = u_ext[HALO:HALO + tm, c0:c0 + gd]
        s = tok
        for k in range(1, w):
            s = s + u_ext[HALO - k:HALO - k + tm, c0:c0 + gd]
        inv_cnt = 1.0 / jnp.minimum(t1, w).astype(F32)
        pooled = s * inv_cnt - tok
        mixed = _dot(pooled.astype(BF16), pool_w_ref[g]) * pool_scale_ref[:, c0:c0 + gd]
        mixed_ref[:, c0:c0 + gd] = mixed.astype(BF16)

    span = HALO + tm - SUBLANES
    for s in range(1, SUBLANES):
        shift_ref[s - 1, 0:span, :] = glu_ext[s:s + span, :]

    rows = CONV_ROWS
    base = HALO - (CONV_WIDTH - 1)
    for r0 in range(0, tm, rows):
        for c0 in range(0, d_conv, LANES):
            acc = jnp.broadcast_to(conv_b_ref[:, c0:c0 + LANES], (rows, LANES))
            for j in range(CONV_WIDTH):
                s = (base + j) % SUBLANES
                a0 = base + j - s + r0
                if s == 0:
                    tap = glu_ext[a0:a0 + rows, c0:c0 + LANES]
                else:
                    tap = shift_ref[s - 1, a0:a0 + rows, c0:c0 + LANES]
                acc = acc + conv_w_ref[j:j + 1, c0:c0 + LANES] * tap
            y_ref[r0:r0 + rows, c0:c0 + LANES] = acc

    y = y_ref[...]
    mu = jnp.mean(y, axis=-1, keepdims=True)
    yc = y - mu
    var = jnp.mean(yc * yc, axis=-1, keepdims=True)
    ln = yc * lax.rsqrt(var + LN_EPS) * ln_g_ref[...] + ln_b_ref[...]
    mixed_ref[:, d_pool:] = jax.nn.silu(ln).astype(BF16)

    out = _dot(mixed_ref[...], w_out_ref[...])
    o_ref[0] = x + g_ref[0] * out


def _l0_mixer(x, sh, sc, g, w_in, pool_w, pool_scale, conv_w, conv_b, ln_g, ln_b, w_out, *, tm):
    B, L, D = x.shape
    d_pool = pool_scale.shape[0]
    d_conv = conv_b.shape[0]
    gd = d_pool // len(POOL_WINDOWS)
    const2 = lambda b, l: (0, 0)
    const3 = lambda b, l: (0, 0, 0)
    per_b = pl.BlockSpec((1, 1, D), lambda b, l: (b, 0, 0))
    kern = functools.partial(_l0_mixer_kernel, tm=tm, d_pool=d_pool, d_conv=d_conv, gd=gd)
    return pl.pallas_call(
        kern,
        grid=(B, L // tm),
        in_specs=[
            pl.BlockSpec((1, tm, D), lambda b, l: (b, l, 0)),
            per_b, per_b, per_b,
            pl.BlockSpec(w_in.shape, const2, pipeline_mode=pl.Buffered(1)),
            pl.BlockSpec(pool_w.shape, const3, pipeline_mode=pl.Buffered(1)),
            pl.BlockSpec((1, d_pool), const2),
            pl.BlockSpec((CONV_WIDTH, d_conv), const2),
            pl.BlockSpec((1, d_conv), const2),
            pl.BlockSpec((1, d_conv), const2),
            pl.BlockSpec((1, d_conv), const2),
            pl.BlockSpec(w_out.shape, const2, pipeline_mode=pl.Buffered(1)),
        ],
        out_specs=pl.BlockSpec((1, tm, D), lambda b, l: (b, l, 0)),
        out_shape=jax.ShapeDtypeStruct((B, L, D), F32),
        scratch_shapes=[
            pltpu.VMEM((HALO + tm, d_pool), F32),
            pltpu.VMEM((HALO + tm, d_conv), F32),
            pltpu.VMEM((SUBLANES - 1, HALO + tm - SUBLANES, d_conv), F32),
            pltpu.VMEM((tm, d_conv), F32),
            pltpu.VMEM((tm, d_pool + d_conv), BF16),
        ],
        compiler_params=_params(("arbitrary", "arbitrary")),
        name="l0_mixer",
    )(x, sh, sc, g, w_in, pool_w, pool_scale.reshape(1, d_pool), conv_w.reshape(CONV_WIDTH, d_conv),
      conv_b.reshape(1, d_conv), ln_g.reshape(1, d_conv), ln_b.reshape(1, d_conv), w_out)


def _cast_rider(w2d, n_chunks, chunk_of):
    rows, cols = w2d.shape
    assert rows % n_chunks == 0 and (rows // n_chunks) % (2 * SUBLANES) == 0
    spec = lambda: pl.BlockSpec((rows // n_chunks, cols), lambda *g: (chunk_of(*g), 0))
    return spec(), spec(), jax.ShapeDtypeStruct(w2d.shape, BF16)


def _l0_ffn_kernel(x_ref, sh_ref, sc_ref, g_ref, wg_ref, wu_ref, wd_ref, cw_ref, o_ref, cwo_ref, h_ref):
    j = pl.program_id(1)

    @pl.when(j == 0)
    def _():
        h_ref[...] = _rms_modulate(x_ref[...], sh_ref[0], sc_ref[0]).astype(BF16)
        o_ref[...] = jnp.zeros_like(o_ref)

    h = h_ref[...]
    a = (jax.nn.silu(_dot(h, wg_ref[...])) * _dot(h, wu_ref[...])).astype(BF16)
    o_ref[...] += _dot(a, wd_ref[...])
    cwo_ref[...] = cw_ref[...].astype(BF16)

    @pl.when(j == pl.num_programs(1) - 1)
    def _():
        o_ref[...] = x_ref[...] + g_ref[0] * o_ref[...]


def _l0_ffn(x2d, sh, sc, g, wg, wu, wd, cast_w, *, seq, tm, tf):
    T, D = x2d.shape
    dff = wg.shape[1]
    nj = dff // tf
    per_b = pl.BlockSpec((1, 1, D), lambda i, j: (i // (seq // tm), 0, 0))
    jc = min(8, nj)
    cw_in, cw_out, cw_shape = _cast_rider(cast_w, (T // tm) * jc, lambda i, j: i * jc + jnp.minimum(j, jc - 1))
    return pl.pallas_call(
        _l0_ffn_kernel,
        grid=(T // tm, nj),
        in_specs=[
            pl.BlockSpec((tm, D), lambda i, j: (i, 0)),
            per_b, per_b, per_b,
            pl.BlockSpec((D, tf), lambda i, j: (0, j)),
            pl.BlockSpec((D, tf), lambda i, j: (0, j)),
            pl.BlockSpec((tf, D), lambda i, j: (j, 0)),
            cw_in,
        ],
        out_specs=[pl.BlockSpec((tm, D), lambda i, j: (i, 0)), cw_out],
        out_shape=[jax.ShapeDtypeStruct((T, D), F32), cw_shape],
        scratch_shapes=[pltpu.VMEM((tm, D), BF16)],
        compiler_params=_params(("arbitrary", "arbitrary")),
        name="l0_ffn",
    )(x2d, sh, sc, g, wg, wu, wd, cast_w)


def _l1_qkv_kernel(x_ref, sh_ref, sc_ref, pos_ref, inv_ref, qn_ref, kn_ref, bd_ref, w_ref, cw_ref,
                   q_ref, k_ref, v_ref, cwo_ref, *, n_q, n_kv):
    cwo_ref[...] = cw_ref[...].astype(BF16)
    x = x_ref[0]
    h = _rms_modulate(x, sh_ref[0], sc_ref[0]).astype(BF16)
    qkv = _dot(h, w_ref[...])

    ang = pos_ref[0].astype(F32) * inv_ref[...]
    cos = jnp.cos(ang)
    sin = jnp.sin(ang)
    lane = lax.broadcasted_iota(jnp.int32, (1, LANES), 1)
    first_half = (lane % HEAD_DIM) < (HEAD_DIM // 2)
    sin_signed = jnp.where(first_half, -sin, sin)
    low_head = lane < HEAD_DIM
    bd = bd_ref[...]

    def norm_rope(blk, nw):
        sq = blk * blk
        hi = sq.astype(BF16)
        lo = (sq - hi.astype(F32)).astype(BF16)
        ss = _dot(hi, bd) + _dot(lo, bd)
        n = blk * lax.rsqrt(ss * (1.0 / HEAD_DIM) + NORM_EPS) * nw
        partner = jnp.where(first_half, pltpu.roll(n, LANES - HEAD_DIM // 2, 1),
                            pltpu.roll(n, HEAD_DIM // 2, 1))
        return n * cos + partner * sin_signed

    def split_heads(blk):
        a_lo = jnp.where(low_head, blk, 0.0)
        b_hi = jnp.where(low_head, 0.0, blk)
        return (a_lo, pltpu.roll(a_lo, HEAD_DIM, 1), pltpu.roll(b_hi, HEAD_DIM, 1), b_hi)

    scale = HEAD_DIM ** -0.5
    for cb in range(n_q):
        blk = qkv[:, cb * LANES:(cb + 1) * LANES]
        q_ref[0, :, cb * LANES:(cb + 1) * LANES] = (norm_rope(blk, qn_ref[...]) * scale).astype(BF16)
    k0 = n_q * LANES
    v0 = k0 + n_kv * LANES
    for cb in range(n_kv):
        kr = norm_rope(qkv[:, k0 + cb * LANES:k0 + (cb + 1) * LANES], kn_ref[...])
        for i, part in enumerate(split_heads(kr)):
            k_ref[0, :, (4 * cb + i) * LANES:(4 * cb + i + 1) * LANES] = part.astype(BF16)
        vr = qkv[:, v0 + cb * LANES:v0 + (cb + 1) * LANES]
        for i, part in enumerate(split_heads(vr)):
            v_ref[0, :, (4 * cb + i) * LANES:(4 * cb + i + 1) * LANES] = part.astype(BF16)


def _l1_qkv(x, sh, sc, pos3, inv_t, qn_t, kn_t, bd, w_qkv, cast_w, *, tm):
    B, L, D = x.shape
    cw_in, cw_out, cw_shape = _cast_rider(cast_w, B * (L // tm), lambda b, l: b * (L // tm) + l)
    n_heads = D // HEAD_DIM
    n_kvh = n_heads // Q_PER_KV
    n_q = n_heads * HEAD_DIM // LANES
    n_kv = n_kvh * HEAD_DIM // LANES
    kw = n_kvh * 2 * LANES
    const2 = lambda b, l: (0, 0)
    per_b = pl.BlockSpec((1, 1, D), lambda b, l: (b, 0, 0))
    row = lambda w: pl.BlockSpec((1, tm, w), lambda b, l: (b, l, 0))
    kern = functools.partial(_l1_qkv_kernel, n_q=n_q, n_kv=n_kv)
    return pl.pallas_call(
        kern,
        grid=(B, L // tm),
        in_specs=[
            row(D), per_b, per_b, row(1),
            pl.BlockSpec((1, LANES), const2),
            pl.BlockSpec((1, LANES), const2),
            pl.BlockSpec((1, LANES), const2),
            pl.BlockSpec((LANES, LANES), const2),
            pl.BlockSpec(w_qkv.shape, const2, pipeline_mode=pl.Buffered(1)),
            cw_in,
        ],
        out_specs=[row(D), row(kw), row(kw), cw_out],
        out_shape=[jax.ShapeDtypeStruct((B, L, D), BF16),
                   jax.ShapeDtypeStruct((B, L, kw), BF16),
                   jax.ShapeDtypeStruct((B, L, kw), BF16),
                   cw_shape],
        compiler_params=_params(("arbitrary", "arbitrary")),
        name="l1_qkv",
    )(x, sh, sc, pos3, inv_t, qn_t, kn_t, bd, w_qkv, cast_w)


def _l1_attn_kernel(sinks_ref, q_ref, kc_ref, kp_ref, vc_ref, vp_ref, x_ref, g_ref, wo_ref, cw_ref,
                    o_ref, cwo_ref, attn_ref, *, tq, n_pairs):
    cwo_ref[...] = cw_ref[...].astype(BF16)
    i = pl.program_id(1)
    gp = Q_PER_KV // 2
    rows = gp * WINDOW
    qi = lax.broadcasted_iota(jnp.int32, (rows, WINDOW), 0) % WINDOW
    kj = lax.broadcasted_iota(jnp.int32, (rows, WINDOW), 1)
    pair_of_row = lax.broadcasted_iota(jnp.int32, (rows, 1), 0) // WINDOW
    mask_cur = kj <= qi
    mask_prev_band = kj > qi

    for n in range(tq // WINDOW):
        r0 = n * WINDOW
        if n == 0:
            k_prev, v_prev = kp_ref[0], vp_ref[0]
            mask_prev = kj > qi + jnp.where(i > 0, 0, WINDOW)
        else:
            k_prev, v_prev = kc_ref[0, r0 - WINDOW:r0, :], vc_ref[0, r0 - WINDOW:r0, :]
            mask_prev = mask_prev_band
        k_cur, v_cur = kc_ref[0, r0:r0 + WINDOW, :], vc_ref[0, r0:r0 + WINDOW, :]
        for kv in range(n_pairs // gp):
            p0 = kv * gp
            qs = jnp.concatenate([q_ref[0, r0:r0 + WINDOW, (p0 + j) * LANES:(p0 + j + 1) * LANES]
                                  for j in range(gp)], axis=0)
            acc = jnp.zeros((rows, LANES), F32)
            for half in range(2):
                c0 = (2 * kv + half) * LANES
                s_p = jnp.where(mask_prev, _dot_nt(qs, k_prev[:, c0:c0 + LANES]), NEG_INF)
                s_c = jnp.where(mask_cur, _dot_nt(qs, k_cur[:, c0:c0 + LANES]), NEG_INF)
                sink = jnp.full((rows, 1), sinks_ref[2 * p0 + half], F32)
                for j in range(1, gp):
                    sink = jnp.where(pair_of_row == j, sinks_ref[2 * (p0 + j) + half], sink)
                m = jnp.maximum(jnp.max(jnp.maximum(s_p, s_c), axis=1, keepdims=True), sink)
                e_p = jnp.exp(s_p - m)
                e_c = jnp.exp(s_c - m)
                denom = jnp.sum(e_p + e_c, axis=1, keepdims=True) + jnp.exp(sink - m)
                o_h = _dot(e_p.astype(BF16), v_prev[:, c0:c0 + LANES]) + \
                    _dot(e_c.astype(BF16), v_cur[:, c0:c0 + LANES])
                acc = acc + o_h * (1.0 / denom)
            for j in range(gp):
                attn_ref[r0:r0 + WINDOW, (p0 + j) * LANES:(p0 + j + 1) * LANES] = \
                    acc[j * WINDOW:(j + 1) * WINDOW].astype(BF16)

    out = _dot(attn_ref[...], wo_ref[...])
    o_ref[0] = x_ref[0] + g_ref[0] * out


def _l1_attn(sinks, q, k, v, x, g, w_o, cast_w, *, tq):
    B, L, D = x.shape
    kw = k.shape[2]
    nb = tq // WINDOW
    cur = lambda w: pl.BlockSpec((1, tq, w), lambda b, i: (b, i, 0))
    prev = pl.BlockSpec((1, WINDOW, kw), lambda b, i: (b, jnp.maximum(i * nb - 1, 0), 0))
    kern = functools.partial(_l1_attn_kernel, tq=tq, n_pairs=D // LANES)
    cw_in, cw_out, cw_shape = _cast_rider(cast_w, B * (L // tq), lambda b, i: b * (L // tq) + i)
    return pl.pallas_call(
        kern,
        grid=(B, L // tq),
        in_specs=[
            pl.BlockSpec(memory_space=pltpu.SMEM),
            cur(D), cur(kw), prev, cur(kw), prev, cur(D),
            pl.BlockSpec((1, 1, D), lambda b, i: (b, 0, 0)),
            pl.BlockSpec(w_o.shape, lambda b, i: (0, 0), pipeline_mode=pl.Buffered(1)),
            cw_in,
        ],
        out_specs=[cur(D), cw_out],
        out_shape=[jax.ShapeDtypeStruct((B, L, D), F32), cw_shape],
        scratch_shapes=[pltpu.VMEM((tq, D), BF16)],
        compiler_params=_params(("arbitrary", "arbitrary"), vmem=VMEM_LIMIT_ATTN),
        name="l1_attn",
    )(sinks, q, k, k, v, v, x, g, w_o, cast_w)


def _l1_router_kernel(x_ref, sh_ref, sc_ref, rw_ref, h_ref, meta_ref, gate_ref, cnt_ref, carry,
                      *, tm):
    i = pl.program_id(0)

    @pl.when(i == 0)
    def _():
        carry[...] = jnp.zeros_like(carry)

    h = _rms_modulate(x_ref[...], sh_ref[0], sc_ref[0])
    h_ref[...] = h

    rw = rw_ref[...]
    h_hi = h.astype(BF16)
    h_lo = (h - h_hi.astype(F32)).astype(BF16)
    w_hi = rw.astype(BF16)
    w_lo = (rw - w_hi.astype(F32)).astype(BF16)
    logits = _dot_nt(w_hi, h_hi) + (_dot_nt(w_hi, h_lo) + _dot_nt(w_lo, h_hi))

    eidx = lax.broadcasted_iota(jnp.int32, (N_EXPERTS, tm), 0)
    m1 = jnp.max(logits, axis=0, keepdims=True)
    i1 = jnp.min(jnp.where(logits == m1, eidx, N_EXPERTS), axis=0, keepdims=True)
    rest = jnp.where(eidx == i1, NEG_INF, logits)
    m2 = jnp.max(rest, axis=0, keepdims=True)
    i2 = jnp.min(jnp.where(rest == m2, eidx, N_EXPERTS), axis=0, keepdims=True)
    e2 = jnp.exp(m2 - m1)
    gate1 = 1.0 / (1.0 + e2)
    gate2 = e2 / (1.0 + e2)

    sel1 = eidx == i1
    sel2 = eidx == i2
    ind = (sel1 | sel2).astype(F32)
    before = lax.broadcasted_iota(jnp.int32, (tm, tm), 0) < lax.broadcasted_iota(jnp.int32, (tm, tm), 1)
    excl = _dot(ind.astype(BF16), before.astype(BF16)) + carry[...]
    r1 = jnp.sum(jnp.where(sel1, excl, 0.0), axis=0, keepdims=True).astype(jnp.int32)
    r2 = jnp.sum(jnp.where(sel2, excl, 0.0), axis=0, keepdims=True).astype(jnp.int32)
    carry[...] = carry[...] + jnp.sum(ind, axis=1, keepdims=True)
    cnt_ref[...] = jnp.broadcast_to(carry[...], cnt_ref.shape).astype(jnp.int32)

    meta_ref[...] = jnp.where(eidx == 0, i1, jnp.where(eidx == 1, i2, jnp.where(eidx == 2, r1,
                              jnp.where(eidx == 3, r2, 0))))
    gate_ref[...] = jnp.where(eidx == 0, gate1, jnp.where(eidx == 1, gate2, 0.0))


def _l1_router(x2d, sh, sc, rw_t, *, seq, tm):
    T, D = x2d.shape
    per_b = pl.BlockSpec((1, 1, D), lambda i: (i // (seq // tm), 0, 0))
    kern = functools.partial(_l1_router_kernel, tm=tm)
    return pl.pallas_call(
        kern,
        grid=(T // tm,),
        in_specs=[
            pl.BlockSpec((tm, D), lambda i: (i, 0)),
            per_b, per_b,
            pl.BlockSpec((N_EXPERTS, D), lambda i: (0, 0)),
        ],
        out_specs=[
            pl.BlockSpec((tm, D), lambda i: (i, 0)),
            pl.BlockSpec((N_EXPERTS, tm), lambda i: (0, i)),
            pl.BlockSpec((N_EXPERTS, tm), lambda i: (0, i)),
            pl.BlockSpec((N_EXPERTS, LANES), lambda i: (0, 0)),
        ],
        out_shape=[
            jax.ShapeDtypeStruct((T, D), F32),
            jax.ShapeDtypeStruct((N_EXPERTS, T), jnp.int32),
            jax.ShapeDtypeStruct((N_EXPERTS, T), F32),
            jax.ShapeDtypeStruct((N_EXPERTS, LANES), jnp.int32),
        ],
        scratch_shapes=[pltpu.VMEM((N_EXPERTS, 1), F32)],
        compiler_params=_params(("arbitrary",)),
        name="l1_router",
    )(x2d, sh, sc, rw_t)


def _dispatch_kernel(pad_lo_ref, pad_hi_ref, slot_ref, tok_ref, *, td):
    i = pl.program_id(0)

    @pl.when(i == 0)
    def _():
        def clear(p, carry):
            tok_ref[p] = 0
            return carry

        for e in range(N_EXPERTS + 1):
            lax.fori_loop(pad_lo_ref[e], pad_hi_ref[e], clear, 0)

    def place(t, carry):
        for k in range(TOP_K):
            tok_ref[slot_ref[k, t]] = i * td + t
        return carry

    lax.fori_loop(0, td, place, 0, unroll=8)


def _dispatch(pad_lo, pad_hi, slots, *, n_slots, td):
    T = slots.shape[1]
    kern = functools.partial(_dispatch_kernel, td=td)
    return pl.pallas_call(
        kern,
        grid_spec=pltpu.PrefetchScalarGridSpec(
            num_scalar_prefetch=2,
            grid=(T // td,),
            in_specs=[pl.BlockSpec((TOP_K, td), lambda i, *_: (0, i), memory_space=pltpu.SMEM)],
            out_specs=pl.BlockSpec(memory_space=pltpu.SMEM),
        ),
        out_shape=jax.ShapeDtypeStruct((n_slots,), jnp.int32),
        compiler_params=_params(("arbitrary",)),
        name="dispatch",
    )(pad_lo, pad_hi, slots)


def _experts_kernel(blk_e_ref, nused_ref, valid_ref, tok_cur_ref, tok_next_ref, h_ref, wg_ref, wu_ref,
                    wd_ref, o_ref, xbuf, xs_ref, sem, *, n_blocks, nj):
    b = pl.program_id(0)
    j = pl.program_id(1)
    nused = nused_ref[0]
    slot = b % 2

    per_step = -(-MOE_BLOCK // nj)
    total = per_step * nj

    def row_copy(tok_ref, r, s):
        tok = tok_ref[0, jnp.minimum(r, MOE_BLOCK - 1)]
        return pltpu.make_async_copy(h_ref.at[pl.ds(tok, 1)], xbuf.at[s, pl.ds(r, 1)], sem.at[s])

    def wait_block(s):
        pltpu.make_async_copy(h_ref.at[pl.ds(0, MOE_BLOCK)], xbuf.at[s, pl.ds(0, MOE_BLOCK)],
                              sem.at[s]).wait()
        for r in range(MOE_BLOCK, total):
            pltpu.make_async_copy(h_ref.at[pl.ds(0, 1)], xbuf.at[s, pl.ds(r, 1)], sem.at[s]).wait()

    @pl.when(b < nused)
    def _():
        @pl.when(j == 0)
        def _():
            @pl.when(b == 0)
            def _():
                def issue(r, carry):
                    row_copy(tok_cur_ref, r, 0).start()
                    return carry

                lax.fori_loop(0, total, issue, 0)

            wait_block(slot)
            xs_ref[...] = xbuf[slot, 0:MOE_BLOCK].astype(BF16)
            o_ref[...] = jnp.zeros_like(o_ref)

        def ffn_rows(n):
            for u in range(per_step):
                row_copy(tok_next_ref, j * per_step + u, 1 - slot).start(priority=1)

            xs = xs_ref[0:n]
            a = (jax.nn.silu(_dot(xs, wg_ref[...])) * _dot(xs, wu_ref[...])).astype(BF16)
            o_ref[0:n] += _dot(a, wd_ref[...])

        half_full = valid_ref[b] <= MOE_BLOCK // 2

        @pl.when(jnp.logical_not(half_full))
        def _():
            ffn_rows(MOE_BLOCK)

        @pl.when(half_full)
        def _():
            ffn_rows(MOE_BLOCK // 2)

    @pl.when((b == nused) & (j == 0))
    def _():
        wait_block(slot)

    @pl.when((b == n_blocks - 1) & (j == nj - 1) & (nused == n_blocks))
    def _():
        wait_block(1 - slot)

    @pl.when((b >= nused) & (j == 0))
    def _():
        o_ref[...] = jnp.zeros_like(o_ref)


def _experts(blk_e, nused, valid, slot_tok, h, wg, wu, wd, *, n_blocks, tf):
    D = h.shape[1]
    dff = wg.shape[2]
    nj = dff // tf
    tok3 = slot_tok.reshape(n_blocks, 1, MOE_BLOCK)
    spare = -(-MOE_BLOCK // nj) * nj - MOE_BLOCK
    xrows = MOE_BLOCK + -(-spare // SUBLANES) * SUBLANES

    def jj(b, j, nu):
        return jnp.where(b < nu[0], j, nj - 1)

    def tok_spec(shift):
        return pl.BlockSpec((None, 1, MOE_BLOCK),
                            lambda b, j, be, nu, va: (jnp.minimum(b + shift, n_blocks - 1), 0, 0),
                            memory_space=pltpu.SMEM)

    return pl.pallas_call(
        functools.partial(_experts_kernel, n_blocks=n_blocks, nj=nj),
        grid_spec=pltpu.PrefetchScalarGridSpec(
            num_scalar_prefetch=3,
            grid=(n_blocks, nj),
            in_specs=[
                tok_spec(0), tok_spec(1),
                pl.BlockSpec(memory_space=pl.ANY),
                pl.BlockSpec((None, D, tf), lambda b, j, be, nu, va: (be[b], 0, jj(b, j, nu))),
                pl.BlockSpec((None, D, tf), lambda b, j, be, nu, va: (be[b], 0, jj(b, j, nu))),
                pl.BlockSpec((None, tf, D), lambda b, j, be, nu, va: (be[b], jj(b, j, nu), 0)),
            ],
            out_specs=pl.BlockSpec((MOE_BLOCK, D), lambda b, j, be, nu, va: (b, 0)),
            scratch_shapes=[pltpu.VMEM((2, xrows, D), F32), pltpu.VMEM((MOE_BLOCK, D), BF16),
                            pltpu.SemaphoreType.DMA((2,))],
        ),
        out_shape=jax.ShapeDtypeStruct((n_blocks * MOE_BLOCK, D), F32),
        compiler_params=_params(("arbitrary", "arbitrary")),
        name="experts",
    )(blk_e, nused, valid, tok3, tok3, h, wg, wu, wd)


def _combine_kernel(slot_cur_ref, slot_next_ref, gate_ref, x_ref, g_ref, yb_ref, o_ref, buf, sem, *, tc):
    i = pl.program_id(0)
    slot = i % 2

    def gather(slot_ref, s):
        def issue(t8, carry):
            for u in range(SUBLANES):
                t = t8 * SUBLANES + u
                for k in range(TOP_K):
                    pltpu.make_async_copy(yb_ref.at[pl.ds(slot_ref[k, t], 1)],
                                          buf.at[s, k, pl.ds(t, 1)], sem.at[s]).start(priority=k)
            return carry

        lax.fori_loop(0, tc // SUBLANES, issue, 0)

    @pl.when(i == 0)
    def _():
        gather(slot_cur_ref, 0)

    has_next = i + 1 < pl.num_programs(0)

    @pl.when(has_next & (slot == 0))
    def _():
        gather(slot_next_ref, 1)

    @pl.when(has_next & (slot == 1))
    def _():
        gather(slot_next_ref, 0)

    for k in range(TOP_K):
        pltpu.make_async_copy(yb_ref.at[pl.ds(0, tc)], buf.at[slot, k], sem.at[slot]).wait()

    gates = gate_ref[...]
    moe = gates[:, 0:1] * buf[slot, 0] + gates[:, 1:2] * buf[slot, 1]
    o_ref[...] = x_ref[...] + g_ref[0] * moe


def _combine(slots, gates_t, x2d, g, yb, *, seq, tc):
    T, D = x2d.shape
    kern = functools.partial(_combine_kernel, tc=tc)
    return pl.pallas_call(
        kern,
        grid=(T // tc,),
        in_specs=[
            pl.BlockSpec((TOP_K, tc), lambda i: (0, i), memory_space=pltpu.SMEM),
            pl.BlockSpec((TOP_K, tc), lambda i: (0, jnp.minimum(i + 1, T // tc - 1)),
                         memory_space=pltpu.SMEM),
            pl.BlockSpec((tc, N_EXPERTS), lambda i: (i, 0)),
            pl.BlockSpec((tc, D), lambda i: (i, 0)),
            pl.BlockSpec((1, 1, D), lambda i: (i // (seq // tc), 0, 0)),
            pl.BlockSpec(memory_space=pl.ANY),
        ],
        out_specs=pl.BlockSpec((tc, D), lambda i: (i, 0)),
        out_shape=jax.ShapeDtypeStruct((T, D), F32),
        scratch_shapes=[pltpu.VMEM((2, TOP_K, tc, D), F32), pltpu.SemaphoreType.DMA((2,))],
        compiler_params=_params(("arbitrary",)),
        name="combine",
    )(slots, slots, gates_t, x2d, g, yb)


def _mod_params(c, w_mod, b_mod):
    B, D = c.shape
    c_pad = jnp.zeros((SUBLANES, D), F32).at[:B].set(c)
    mod = _adaln(c_pad, w_mod, b_mod)[:B]
    return [m.reshape(B, 1, D) for m in jnp.split(mod, 6, axis=-1)]


def kernel(x, c, positions, l0_w_mod, l0_b_mod, l0_w_in, l0_pool_w, l0_pool_scale, l0_conv_w, l0_conv_b, l0_conv_ln_g, l0_conv_ln_b, l0_w_out, l0_ffn_w_gate, l0_ffn_w_up, l0_ffn_w_down, l1_w_mod, l1_b_mod, l1_w_qkv, l1_q_norm, l1_k_norm, l1_sinks, l1_w_o, l1_router_w, l1_exp_w_gate, l1_exp_w_up, l1_exp_w_down):
    B, L, D = x.shape
    T = B * L
    bf = lambda w: w.astype(BF16)

    sh1, sc1, g1, sh2, sc2, g2 = _mod_params(c, l0_w_mod, l0_b_mod)
    x = _l0_mixer(x, sh1, sc1, g1, bf(l0_w_in), bf(l0_pool_w), l0_pool_scale, l0_conv_w, l0_conv_b,
                  l0_conv_ln_g, l0_conv_ln_b, bf(l0_w_out), tm=_fit(TM_MIXER, L))
    n_e, _, dfe = l1_exp_w_gate.shape
    x, wg_e = _l0_ffn(x.reshape(T, D), sh2, sc2, g2, bf(l0_ffn_w_gate), bf(l0_ffn_w_up), bf(l0_ffn_w_down),
                      l1_exp_w_gate.reshape(n_e * D, dfe),
                      seq=L, tm=_fit(TM_FFN, L), tf=_fit(TF_FFN, l0_ffn_w_gate.shape[1]))
    x = x.reshape(B, L, D)

    sh1, sc1, g1, sh2, sc2, g2 = _mod_params(c, l1_w_mod, l1_b_mod)
    half = HEAD_DIM // 2
    inv = ROPE_THETA ** (-jnp.arange(half, dtype=F32) / half)
    inv_t = jnp.tile(inv, LANES // half).reshape(1, LANES)
    qn_t = jnp.tile(l1_q_norm, LANES // HEAD_DIM).reshape(1, LANES)
    kn_t = jnp.tile(l1_k_norm, LANES // HEAD_DIM).reshape(1, LANES)
    lane = jnp.arange(LANES)
    bd = (lane[:, None] // HEAD_DIM == lane[None, :] // HEAD_DIM).astype(BF16)
    q, k, v, wd_e = _l1_qkv(x, sh1, sc1, positions.reshape(B, L, 1), inv_t, qn_t, kn_t, bd, bf(l1_w_qkv),
                            l1_exp_w_down.reshape(n_e * dfe, D), tm=_fit(TM_QKV, L))
    x, wu_e = _l1_attn(l1_sinks, q, k, v, x, g1, bf(l1_w_o), l1_exp_w_up.reshape(n_e * D, dfe),
                       tq=_fit(TQ_ATTN, L))

    x2d = x.reshape(T, D)
    h, meta, gates, cnt = _l1_router(x2d, sh2, sc2, l1_router_w.T, seq=L, tm=_fit(TM_ROUTER, L))
    counts = cnt[:, 0]
    padded = ((counts + MOE_BLOCK - 1) // MOE_BLOCK) * MOE_BLOCK
    pend = jnp.cumsum(padded)
    pstart = pend - padded
    n_blocks = (T * TOP_K + N_EXPERTS * (MOE_BLOCK - 1) + MOE_BLOCK - 1) // MOE_BLOCK
    nused = (pend[-1] // MOE_BLOCK).astype(jnp.int32).reshape(1)
    blk_start = jnp.minimum(jnp.arange(n_blocks, dtype=jnp.int32), nused[0] - 1) * MOE_BLOCK
    blk_e = jnp.minimum(jnp.sum(blk_start[:, None] >= pend[None, :], axis=1), N_EXPERTS - 1).astype(jnp.int32)
    blk_onehot = blk_e[:, None] == jnp.arange(N_EXPERTS, dtype=jnp.int32)
    blk_end = jnp.sum(jnp.where(blk_onehot, (pstart + counts).astype(jnp.int32), 0), axis=1)
    valid = jnp.clip(blk_end - blk_start, 0, MOE_BLOCK).astype(jnp.int32)
    sel = meta[:TOP_K, :, None] == jnp.arange(N_EXPERTS, dtype=jnp.int32)
    slots = meta[TOP_K:2 * TOP_K] + jnp.sum(jnp.where(sel, pstart.astype(jnp.int32), 0), axis=-1)
    n_slots = n_blocks * MOE_BLOCK
    pad_lo = jnp.concatenate([pstart + counts, pend[-1:]]).astype(jnp.int32)
    pad_hi = jnp.concatenate([pend, jnp.full((1,), n_slots, pend.dtype)]).astype(jnp.int32)
    slot_tok = _dispatch(pad_lo, pad_hi, slots, n_slots=n_slots, td=_fit(TD_DISPATCH, T))
    yb = _experts(blk_e, nused, valid, slot_tok, h, wg_e.reshape(n_e, D, dfe), wu_e.reshape(n_e, D, dfe),
                  wd_e.reshape(n_e, dfe, D), n_blocks=n_blocks, tf=_fit(TF_EXPERT, dfe))
    out = _combine(slots, gates.T, x2d, g2, yb, seq=L, tc=_fit(TC_COMBINE, L))
    return out.reshape(B, L, D)
```

```python
import functools

import jax
import jax.numpy as jnp
from jax import lax
from jax.experimental import pallas as pl
from jax.experimental.pallas import tpu as pltpu

F32 = jnp.float32
BF16 = jnp.bfloat16

HEAD_DIM = 64
Q_PER_KV = 8
POOL_WINDOWS = (2, 4, 8, 16)
CONV_WIDTH = 31
WINDOW = 128
ROPE_THETA = 10000.0
N_EXPERTS = 8
TOP_K = 2
NORM_EPS = 1e-6
LN_EPS = 1e-5

LANES = 128
SUBLANES = 8
V7X_VMEM_BYTES = 64 * 1024 * 1024
VMEM_LIMIT = V7X_VMEM_BYTES - 8 * 1024 * 1024
VMEM_LIMIT_ATTN = V7X_VMEM_BYTES - 4 * 1024 * 1024
HALO = 32
MOE_BLOCK = 512
NEG_INF = float("-inf")

TN_ADALN = 1024
TM_MIXER = 256
CONV_ROWS = 64
TM_FFN = 512
TF_FFN = 512
TM_QKV = 256
TQ_ATTN = 256
TM_ROUTER = 512
TD_DISPATCH = 2048
TF_EXPERT = 1024
TC_COMBINE = 512


def _fit(preferred, extent):
    tile = min(preferred, extent)
    while extent % tile:
        tile //= 2
    return tile


def _params(sem, vmem=VMEM_LIMIT):
    return pltpu.CompilerParams(dimension_semantics=sem, vmem_limit_bytes=vmem)


def _rms_modulate(x, shift, scale):
    ms = jnp.mean(x * x, axis=-1, keepdims=True)
    return x * lax.rsqrt(ms + NORM_EPS) * (1.0 + scale) + shift


def _dot(a, b):
    return jnp.dot(a, b, preferred_element_type=F32)


def _dot_nt(a, b):
    return lax.dot_general(a, b, (((1,), (1,)), ((), ())), preferred_element_type=F32)


def _adaln_kernel(c_ref, w_ref, b_ref, o_ref):
    sc = jax.nn.silu(c_ref[...]).astype(BF16)
    o_ref[...] = _dot(sc, w_ref[...].astype(BF16)) + b_ref[...]


def _adaln(c_pad, w_mod, b_mod):
    d, n = w_mod.shape
    tn = TN_ADALN
    return pl.pallas_call(
        _adaln_kernel,
        grid=(n // tn,),
        in_specs=[
            pl.BlockSpec((SUBLANES, d), lambda j: (0, 0)),
            pl.BlockSpec((d, tn), lambda j: (0, j)),
            pl.BlockSpec((1, tn), lambda j: (0, j)),
        ],
        out_specs=pl.BlockSpec((SUBLANES, tn), lambda j: (0, j)),
        out_shape=jax.ShapeDtypeStruct((SUBLANES, n), F32),
        compiler_params=_params(("arbitrary",)),
        name="adaln",
    )(c_pad, w_mod, b_mod.reshape(1, n))


def _l0_mixer_kernel(*refs, tm, d_pool, d_conv, gd, n_cast):
    (x_ref, sh_ref, sc_ref, g_ref, w_in_ref, pool_w_ref, pool_scale_ref,
     conv_w_ref, conv_b_ref, ln_g_ref, ln_b_ref, w_out_ref) = refs[:12]
    cast_in = refs[12:12 + n_cast]
    o_ref = refs[12 + n_cast]
    cast_out = refs[13 + n_cast:13 + 2 * n_cast]
    u_ext, glu_ext, shift_ref, y_ref, mixed_ref = refs[13 + 2 * n_cast:]
    l = pl.program_id(1)
    for cw_ref, cwo_ref in zip(cast_in, cast_out):
        cwo_ref[...] = cw_ref[...].astype(BF16)

    @pl.when(l == 0)
    def _():
        u_ext[0:HALO, :] = jnp.zeros((HALO, d_pool), F32)
        glu_ext[0:HALO, :] = jnp.zeros((HALO, d_conv), F32)

    @pl.when(l > 0)
    def _():
        u_ext[0:HALO, :] = u_ext[tm:tm + HALO, :]
        glu_ext[0:HALO, :] = glu_ext[tm:tm + HALO, :]

    x = x_ref[0]
    h = _rms_modulate(x, sh_ref[0], sc_ref[0]).astype(BF16)
    z = _dot(h, w_in_ref[...])
    u_ext[HALO:HALO + tm, :] = z[:, :d_pool]
    glu_ext[HALO:HALO + tm, :] = z[:, d_pool:d_pool + d_conv] * jax.nn.sigmoid(z[:, d_pool + d_conv:])

    t1 = l * tm + lax.broadcasted_iota(jnp.int32, (tm, 1), 0) + 1
    for g, w in enumerate(POOL_WINDOWS):
        c0 = g * gd
        tok = u_ext[HALO:HALO + tm, c0:c0 + gd]
        s = tok
        for k in range(1, w):
            s = s + u_ext[HALO - k:HALO - k + tm, c0:c0 + gd]
        inv_cnt = 1.0 / jnp.minimum(t1, w).astype(F32)
        pooled = s * inv_cnt - tok
        mixed = _dot(pooled.astype(BF16), pool_w_ref[g]) * pool_scale_ref[:, c0:c0 + gd]
        mixed_ref[:, c0:c0 + gd] = mixed.astype(BF16)

    span = HALO + tm - SUBLANES
    for s in range(1, SUBLANES):
        shift_ref[s - 1, 0:span, :] = glu_ext[s:s + span, :]

    rows = CONV_ROWS
    base = HALO - (CONV_WIDTH - 1)
    for r0 in range(0, tm, rows):
        for c0 in range(0, d_conv, LANES):
            acc = jnp.broadcast_to(conv_b_ref[:, c0:c0 + LANES], (rows, LANES))
            for j in range(CONV_WIDTH):
                s = (base + j) % SUBLANES
                a0 = base + j - s + r0
                if s == 0:
                    tap = glu_ext[a0:a0 + rows, c0:c0 + LANES]
                else:
                    tap = shift_ref[s - 1, a0:a0 + rows, c0:c0 + LANES]
                acc = acc + conv_w_ref[j:j + 1, c0:c0 + LANES] * tap
            y_ref[r0:r0 + rows, c0:c0 + LANES] = acc

    y = y_ref[...]
    mu = jnp.mean(y, axis=-1, keepdims=True)
    yc = y - mu
    var = jnp.mean(yc * yc, axis=-1, keepdims=True)
    ln = yc * lax.rsqrt(var + LN_EPS) * ln_g_ref[...] + ln_b_ref[...]
    mixed_ref[:, d_pool:] = jax.nn.silu(ln).astype(BF16)

    out = _dot(mixed_ref[...], w_out_ref[...])
    o_ref[0] = x + g_ref[0] * out


def _l0_mixer(x, sh, sc, g, w_in, pool_w, pool_scale, conv_w, conv_b, ln_g, ln_b, w_out, cast_ws, *, tm):
    B, L, D = x.shape
    d_pool = pool_scale.shape[0]
    d_conv = conv_b.shape[0]
    gd = d_pool // len(POOL_WINDOWS)
    const2 = lambda b, l: (0, 0)
    const3 = lambda b, l: (0, 0, 0)
    per_b = pl.BlockSpec((1, 1, D), lambda b, l: (b, 0, 0))
    kern = functools.partial(_l0_mixer_kernel, tm=tm, d_pool=d_pool, d_conv=d_conv, gd=gd,
                             n_cast=len(cast_ws))
    nl = L // tm
    riders = []
    for w in cast_ws:
        n = _rider_chunks(w.shape[0], B * nl)
        riders.append(_cast_rider(w, n, lambda b, l, n=n: jnp.minimum(b * nl + l, n - 1)))
    return pl.pallas_call(
        kern,
        grid=(B, nl),
        in_specs=[
            pl.BlockSpec((1, tm, D), lambda b, l: (b, l, 0)),
            per_b, per_b, per_b,
            pl.BlockSpec(w_in.shape, const2, pipeline_mode=pl.Buffered(1)),
            pl.BlockSpec(pool_w.shape, const3, pipeline_mode=pl.Buffered(1)),
            pl.BlockSpec((1, d_pool), const2),
            pl.BlockSpec((CONV_WIDTH, d_conv), const2),
            pl.BlockSpec((1, d_conv), const2),
            pl.BlockSpec((1, d_conv), const2),
            pl.BlockSpec((1, d_conv), const2),
            pl.BlockSpec(w_out.shape, const2, pipeline_mode=pl.Buffered(1)),
        ] + [r[0] for r in riders],
        out_specs=[pl.BlockSpec((1, tm, D), lambda b, l: (b, l, 0))] + [r[1] for r in riders],
        out_shape=[jax.ShapeDtypeStruct((B, L, D), F32)] + [r[2] for r in riders],
        scratch_shapes=[
            pltpu.VMEM((HALO + tm, d_pool), F32),
            pltpu.VMEM((HALO + tm, d_conv), F32),
            pltpu.VMEM((SUBLANES - 1, HALO + tm - SUBLANES, d_conv), F32),
            pltpu.VMEM((tm, d_conv), F32),
            pltpu.VMEM((tm, d_pool + d_conv), BF16),
        ],
        compiler_params=_params(("arbitrary", "arbitrary")),
        name="l0_mixer",
    )(x, sh, sc, g, w_in, pool_w, pool_scale.reshape(1, d_pool), conv_w.reshape(CONV_WIDTH, d_conv),
      conv_b.reshape(1, d_conv), ln_g.reshape(1, d_conv), ln_b.reshape(1, d_conv), w_out, *cast_ws)


def _cast_rider(w2d, n_chunks, chunk_of):
    rows, cols = w2d.shape
    assert rows % n_chunks == 0 and (rows // n_chunks) % (2 * SUBLANES) == 0
    spec = lambda: pl.BlockSpec((rows // n_chunks, cols), lambda *g: (chunk_of(*g), 0))
    return spec(), spec(), jax.ShapeDtypeStruct(w2d.shape, BF16)


def _rider_chunks(rows, steps):
    for n in range(steps, 0, -1):
        if rows % n == 0 and (rows // n) % (2 * SUBLANES) == 0:
            return n
    raise ValueError(f"{rows} rows cannot be chunked")


def _l0_ffn_kernel(x_ref, sh_ref, sc_ref, g_ref, wg_ref, wu_ref, wd_ref, cw_ref, o_ref, cwo_ref, h_ref):
    j = pl.program_id(1)

    @pl.when(j == 0)
    def _():
        h_ref[...] = _rms_modulate(x_ref[...], sh_ref[0], sc_ref[0]).astype(BF16)
        o_ref[...] = jnp.zeros_like(o_ref)

    h = h_ref[...]
    a = (jax.nn.silu(_dot(h, wg_ref[...])) * _dot(h, wu_ref[...])).astype(BF16)
    o_ref[...] += _dot(a, wd_ref[...])
    cwo_ref[...] = cw_ref[...].astype(BF16)

    @pl.when(j == pl.num_programs(1) - 1)
    def _():
        o_ref[...] = x_ref[...] + g_ref[0] * o_ref[...]


def _l0_ffn(x2d, sh, sc, g, wg, wu, wd, cast_w, *, seq, tm, tf):
    T, D = x2d.shape
    dff = wg.shape[1]
    nj = dff // tf
    per_b = pl.BlockSpec((1, 1, D), lambda i, j: (i // (seq // tm), 0, 0))
    jc = min(8, nj)
    cw_in, cw_out, cw_shape = _cast_rider(cast_w, (T // tm) * jc, lambda i, j: i * jc + jnp.minimum(j, jc - 1))
    return pl.pallas_call(
        _l0_ffn_kernel,
        grid=(T // tm, nj),
        in_specs=[
            pl.BlockSpec((tm, D), lambda i, j: (i, 0)),
            per_b, per_b, per_b,
            pl.BlockSpec((D, tf), lambda i, j: (0, j)),
            pl.BlockSpec((D, tf), lambda i, j: (0, j)),
            pl.BlockSpec((tf, D), lambda i, j: (j, 0)),
            cw_in,
        ],
        out_specs=[pl.BlockSpec((tm, D), lambda i, j: (i, 0)), cw_out],
        out_shape=[jax.ShapeDtypeStruct((T, D), F32), cw_shape],
        scratch_shapes=[pltpu.VMEM((tm, D), BF16)],
        compiler_params=_params(("arbitrary", "arbitrary")),
        name="l0_ffn",
    )(x2d, sh, sc, g, wg, wu, wd, cast_w)


def _l1_qkv_kernel(x_ref, sh_ref, sc_ref, pos_ref, inv_ref, qn_ref, kn_ref, bd_ref, w_ref, cw_ref,
                   q_ref, k_ref, v_ref, cwo_ref, *, n_q, n_kv):
    cwo_ref[...] = cw_ref[...].astype(BF16)
    x = x_ref[0]
    h = _rms_modulate(x, sh_ref[0], sc_ref[0]).astype(BF16)
    qkv = _dot(h, w_ref[...])

    ang = pos_ref[0].astype(F32) * inv_ref[...]
    cos = jnp.cos(ang)
    sin = jnp.sin(ang)
    lane = lax.broadcasted_iota(jnp.int32, (1, LANES), 1)
    first_half = (lane % HEAD_DIM) < (HEAD_DIM // 2)
    sin_signed = jnp.where(first_half, -sin, sin)
    low_head = lane < HEAD_DIM
    bd = bd_ref[...]

    def norm_rope(blk, nw):
        sq = blk * blk
        hi = sq.astype(BF16)
        lo = (sq - hi.astype(F32)).astype(BF16)
        ss = _dot(hi, bd) + _dot(lo, bd)
        n = blk * lax.rsqrt(ss * (1.0 / HEAD_DIM) + NORM_EPS) * nw
        partner = jnp.where(first_half, pltpu.roll(n, LANES - HEAD_DIM // 2, 1),
                            pltpu.roll(n, HEAD_DIM // 2, 1))
        return n * cos + partner * sin_signed

    def split_heads(blk):
        a_lo = jnp.where(low_head, blk, 0.0)
        b_hi = jnp.where(low_head, 0.0, blk)
        return (a_lo, pltpu.roll(a_lo, HEAD_DIM, 1), pltpu.roll(b_hi, HEAD_DIM, 1), b_hi)

    scale = HEAD_DIM ** -0.5
    for cb in range(n_q):
        blk = qkv[:, cb * LANES:(cb + 1) * LANES]
        q_ref[0, :, cb * LANES:(cb + 1) * LANES] = (norm_rope(blk, qn_ref[...]) * scale).astype(BF16)
    k0 = n_q * LANES
    v0 = k0 + n_kv * LANES
    for cb in range(n_kv):
        kr = norm_rope(qkv[:, k0 + cb * LANES:k0 + (cb + 1) * LANES], kn_ref[...])
        for i, part in enumerate(split_heads(kr)):
            k_ref[0, :, (4 * cb + i) * LANES:(4 * cb + i + 1) * LANES] = part.astype(BF16)
        vr = qkv[:, v0 + cb * LANES:v0 + (cb + 1) * LANES]
        for i, part in enumerate(split_heads(vr)):
            v_ref[0, :, (4 * cb + i) * LANES:(4 * cb + i + 1) * LANES] = part.astype(BF16)


def _l1_qkv(x, sh, sc, pos3, inv_t, qn_t, kn_t, bd, w_qkv, cast_w, *, tm):
    B, L, D = x.shape
    cw_in, cw_out, cw_shape = _cast_rider(cast_w, B * (L // tm), lambda b, l: b * (L // tm) + l)
    n_heads = D // HEAD_DIM
    n_kvh = n_heads // Q_PER_KV
    n_q = n_heads * HEAD_DIM // LANES
    n_kv = n_kvh * HEAD_DIM // LANES
    kw = n_kvh * 2 * LANES
    const2 = lambda b, l: (0, 0)
    per_b = pl.BlockSpec((1, 1, D), lambda b, l: (b, 0, 0))
    row = lambda w: pl.BlockSpec((1, tm, w), lambda b, l: (b, l, 0))
    kern = functools.partial(_l1_qkv_kernel, n_q=n_q, n_kv=n_kv)
    return pl.pallas_call(
        kern,
        grid=(B, L // tm),
        in_specs=[
            row(D), per_b, per_b, row(1),
            pl.BlockSpec((1, LANES), const2),
            pl.BlockSpec((1, LANES), const2),
            pl.BlockSpec((1, LANES), const2),
            pl.BlockSpec((LANES, LANES), const2),
            pl.BlockSpec(w_qkv.shape, const2, pipeline_mode=pl.Buffered(1)),
            cw_in,
        ],
        out_specs=[row(D), row(kw), row(kw), cw_out],
        out_shape=[jax.ShapeDtypeStruct((B, L, D), BF16),
                   jax.ShapeDtypeStruct((B, L, kw), BF16),
                   jax.ShapeDtypeStruct((B, L, kw), BF16),
                   cw_shape],
        compiler_params=_params(("arbitrary", "arbitrary")),
        name="l1_qkv",
    )(x, sh, sc, pos3, inv_t, qn_t, kn_t, bd, w_qkv, cast_w)


def _l1_attn_kernel(sinks_ref, q_ref, kc_ref, kp_ref, vc_ref, vp_ref, x_ref, g_ref, wo_ref, cw_ref,
                    o_ref, cwo_ref, attn_ref, *, tq, n_pairs):
    cwo_ref[...] = cw_ref[...].astype(BF16)
    i = pl.program_id(1)
    gp = Q_PER_KV // 2
    rows = gp * WINDOW
    qi = lax.broadcasted_iota(jnp.int32, (rows, WINDOW), 0) % WINDOW
    kj = lax.broadcasted_iota(jnp.int32, (rows, WINDOW), 1)
    pair_of_row = lax.broadcasted_iota(jnp.int32, (rows, 1), 0) // WINDOW
    mask_cur = kj <= qi
    mask_prev_band = kj > qi

    for n in range(tq // WINDOW):
        r0 = n * WINDOW
        if n == 0:
            k_prev, v_prev = kp_ref[0], vp_ref[0]
            mask_prev = kj > qi + jnp.where(i > 0, 0, WINDOW)
        else:
            k_prev, v_prev = kc_ref[0, r0 - WINDOW:r0, :], vc_ref[0, r0 - WINDOW:r0, :]
            mask_prev = mask_prev_band
        k_cur, v_cur = kc_ref[0, r0:r0 + WINDOW, :], vc_ref[0, r0:r0 + WINDOW, :]
        for kv in range(n_pairs // gp):
            p0 = kv * gp
            qs = jnp.concatenate([q_ref[0, r0:r0 + WINDOW, (p0 + j) * LANES:(p0 + j + 1) * LANES]
                                  for j in range(gp)], axis=0)
            acc = jnp.zeros((rows, LANES), F32)
            for half in range(2):
                c0 = (2 * kv + half) * LANES
                s_p = jnp.where(mask_prev, _dot_nt(qs, k_prev[:, c0:c0 + LANES]), NEG_INF)
                s_c = jnp.where(mask_cur, _dot_nt(qs, k_cur[:, c0:c0 + LANES]), NEG_INF)
                sink = jnp.full((rows, 1), sinks_ref[2 * p0 + half], F32)
                for j in range(1, gp):
                    sink = jnp.where(pair_of_row == j, sinks_ref[2 * (p0 + j) + half], sink)
                m = jnp.maximum(jnp.max(jnp.maximum(s_p, s_c), axis=1, keepdims=True), sink)
                e_p = jnp.exp(s_p - m)
                e_c = jnp.exp(s_c - m)
                denom = jnp.sum(e_p + e_c, axis=1, keepdims=True) + jnp.exp(sink - m)
                o_h = _dot(e_p.astype(BF16), v_prev[:, c0:c0 + LANES]) + \
                    _dot(e_c.astype(BF16), v_cur[:, c0:c0 + LANES])
                acc = acc + o_h * (1.0 / denom)
            for j in range(gp):
                attn_ref[r0:r0 + WINDOW, (p0 + j) * LANES:(p0 + j + 1) * LANES] = \
                    acc[j * WINDOW:(j + 1) * WINDOW].astype(BF16)

    out = _dot(attn_ref[...], wo_ref[...])
    o_ref[0] = x_ref[0] + g_ref[0] * out


def _l1_attn(sinks, q, k, v, x, g, w_o, cast_w, *, tq):
    B, L, D = x.shape
    kw = k.shape[2]
    nb = tq // WINDOW
    cur = lambda w: pl.BlockSpec((1, tq, w), lambda b, i: (b, i, 0))
    prev = pl.BlockSpec((1, WINDOW, kw), lambda b, i: (b, jnp.maximum(i * nb - 1, 0), 0))
    kern = functools.partial(_l1_attn_kernel, tq=tq, n_pairs=D // LANES)
    cw_in, cw_out, cw_shape = _cast_rider(cast_w, B * (L // tq), lambda b, i: b * (L // tq) + i)
    return pl.pallas_call(
        kern,
        grid=(B, L // tq),
        in_specs=[
            pl.BlockSpec(memory_space=pltpu.SMEM),
            cur(D), cur(kw), prev, cur(kw), prev, cur(D),
            pl.BlockSpec((1, 1, D), lambda b, i: (b, 0, 0)),
            pl.BlockSpec(w_o.shape, lambda b, i: (0, 0), pipeline_mode=pl.Buffered(1)),
            cw_in,
        ],
        out_specs=[cur(D), cw_out],
        out_shape=[jax.ShapeDtypeStruct((B, L, D), F32), cw_shape],
        scratch_shapes=[pltpu.VMEM((tq, D), BF16)],
        compiler_params=_params(("arbitrary", "arbitrary"), vmem=VMEM_LIMIT_ATTN),
        name="l1_attn",
    )(sinks, q, k, k, v, v, x, g, w_o, cast_w)


def _l1_router_kernel(x_ref, sh_ref, sc_ref, rw_ref, h_ref, meta_ref, gate_ref, cnt_ref, carry,
                      *, tm):
    i = pl.program_id(0)

    @pl.when(i == 0)
    def _():
        carry[...] = jnp.zeros_like(carry)

    h = _rms_modulate(x_ref[...], sh_ref[0], sc_ref[0])
    h_ref[...] = h

    rw = rw_ref[...]
    h_hi = h.astype(BF16)
    h_lo = (h - h_hi.astype(F32)).astype(BF16)
    w_hi = rw.astype(BF16)
    w_lo = (rw - w_hi.astype(F32)).astype(BF16)
    logits = _dot_nt(w_hi, h_hi) + (_dot_nt(w_hi, h_lo) + _dot_nt(w_lo, h_hi))

    eidx = lax.broadcasted_iota(jnp.int32, (N_EXPERTS, tm), 0)
    m1 = jnp.max(logits, axis=0, keepdims=True)
    i1 = jnp.min(jnp.where(logits == m1, eidx, N_EXPERTS), axis=0, keepdims=True)
    rest = jnp.where(eidx == i1, NEG_INF, logits)
    m2 = jnp.max(rest, axis=0, keepdims=True)
    i2 = jnp.min(jnp.where(rest == m2, eidx, N_EXPERTS), axis=0, keepdims=True)
    e2 = jnp.exp(m2 - m1)
    gate1 = 1.0 / (1.0 + e2)
    gate2 = e2 / (1.0 + e2)

    sel1 = eidx == i1
    sel2 = eidx == i2
    ind = (sel1 | sel2).astype(F32)
    before = lax.broadcasted_iota(jnp.int32, (tm, tm), 0) < lax.broadcasted_iota(jnp.int32, (tm, tm), 1)
    excl = _dot(ind.astype(BF16), before.astype(BF16)) + carry[...]
    r1 = jnp.sum(jnp.where(sel1, excl, 0.0), axis=0, keepdims=True).astype(jnp.int32)
    r2 = jnp.sum(jnp.where(sel2, excl, 0.0), axis=0, keepdims=True).astype(jnp.int32)
    carry[...] = carry[...] + jnp.sum(ind, axis=1, keepdims=True)
    cnt_ref[...] = jnp.broadcast_to(carry[...], cnt_ref.shape).astype(jnp.int32)

    meta_ref[...] = jnp.where(eidx == 0, i1, jnp.where(eidx == 1, i2, jnp.where(eidx == 2, r1,
                              jnp.where(eidx == 3, r2, 0))))
    gate_ref[...] = jnp.where(eidx == 0, gate1, jnp.where(eidx == 1, gate2, 0.0))


def _l1_router(x2d, sh, sc, rw_t, *, seq, tm):
    T, D = x2d.shape
    per_b = pl.BlockSpec((1, 1, D), lambda i: (i // (seq // tm), 0, 0))
    kern = functools.partial(_l1_router_kernel, tm=tm)
    return pl.pallas_call(
        kern,
        grid=(T // tm,),
        in_specs=[
            pl.BlockSpec((tm, D), lambda i: (i, 0)),
            per_b, per_b,
            pl.BlockSpec((N_EXPERTS, D), lambda i: (0, 0)),
        ],
        out_specs=[
            pl.BlockSpec((tm, D), lambda i: (i, 0)),
            pl.BlockSpec((N_EXPERTS, tm), lambda i: (0, i)),
            pl.BlockSpec((N_EXPERTS, tm), lambda i: (0, i)),
            pl.BlockSpec((N_EXPERTS, LANES), lambda i: (0, 0)),
        ],
        out_shape=[
            jax.ShapeDtypeStruct((T, D), F32),
            jax.ShapeDtypeStruct((N_EXPERTS, T), jnp.int32),
            jax.ShapeDtypeStruct((N_EXPERTS, T), F32),
            jax.ShapeDtypeStruct((N_EXPERTS, LANES), jnp.int32),
        ],
        scratch_shapes=[pltpu.VMEM((N_EXPERTS, 1), F32)],
        compiler_params=_params(("arbitrary",)),
        name="l1_router",
    )(x2d, sh, sc, rw_t)


def _dispatch_kernel(pad_lo_ref, pad_hi_ref, slot_ref, tok_ref, *, td):
    i = pl.program_id(0)

    @pl.when(i == 0)
    def _():
        def clear(p, carry):
            tok_ref[p] = 0
            return carry

        for e in range(N_EXPERTS + 1):
            lax.fori_loop(pad_lo_ref[e], pad_hi_ref[e], clear, 0)

    def place(t, carry):
        for k in range(TOP_K):
            tok_ref[slot_ref[k, t]] = i * td + t
        return carry

    lax.fori_loop(0, td, place, 0, unroll=8)


def _dispatch(pad_lo, pad_hi, slots, *, n_slots, td):
    T = slots.shape[1]
    kern = functools.partial(_dispatch_kernel, td=td)
    return pl.pallas_call(
        kern,
        grid_spec=pltpu.PrefetchScalarGridSpec(
            num_scalar_prefetch=2,
            grid=(T // td,),
            in_specs=[pl.BlockSpec((TOP_K, td), lambda i, *_: (0, i), memory_space=pltpu.SMEM)],
            out_specs=pl.BlockSpec(memory_space=pltpu.SMEM),
        ),
        out_shape=jax.ShapeDtypeStruct((n_slots,), jnp.int32),
        compiler_params=_params(("arbitrary",)),
        name="dispatch",
    )(pad_lo, pad_hi, slots)


def _experts_kernel(blk_e_ref, nused_ref, valid_ref, tok_cur_ref, tok_next_ref, h_ref, wg_ref, wu_ref,
                    wd_ref, o_ref, xbuf, xs_ref, sem, *, n_blocks, nj):
    b = pl.program_id(0)
    j = pl.program_id(1)
    nused = nused_ref[0]
    slot = b % 2

    per_step = -(-MOE_BLOCK // nj)
    total = per_step * nj

    def row_copy(tok_ref, r, s):
        tok = tok_ref[0, jnp.minimum(r, MOE_BLOCK - 1)]
        return pltpu.make_async_copy(h_ref.at[pl.ds(tok, 1)], xbuf.at[s, pl.ds(r, 1)], sem.at[s])

    def wait_block(s):
        pltpu.make_async_copy(h_ref.at[pl.ds(0, MOE_BLOCK)], xbuf.at[s, pl.ds(0, MOE_BLOCK)],
                              sem.at[s]).wait()
        for r in range(MOE_BLOCK, total):
            pltpu.make_async_copy(h_ref.at[pl.ds(0, 1)], xbuf.at[s, pl.ds(r, 1)], sem.at[s]).wait()

    @pl.when(b < nused)
    def _():
        @pl.when(j == 0)
        def _():
            @pl.when(b == 0)
            def _():
                def issue(r, carry):
                    row_copy(tok_cur_ref, r, 0).start()
                    return carry

                lax.fori_loop(0, total, issue, 0)

            wait_block(slot)
            xs_ref[...] = xbuf[slot, 0:MOE_BLOCK].astype(BF16)
            o_ref[...] = jnp.zeros_like(o_ref)

        def ffn_rows(n):
            for u in range(per_step):
                row_copy(tok_next_ref, j * per_step + u, 1 - slot).start()

            xs = xs_ref[0:n]
            a = (jax.nn.silu(_dot(xs, wg_ref[...])) * _dot(xs, wu_ref[...])).astype(BF16)
            o_ref[0:n] += _dot(a, wd_ref[...])

        half_full = valid_ref[b] <= MOE_BLOCK // 2

        @pl.when(jnp.logical_not(half_full))
        def _():
            ffn_rows(MOE_BLOCK)

        @pl.when(half_full)
        def _():
            ffn_rows(MOE_BLOCK // 2)

    @pl.when((b == nused) & (j == 0))
    def _():
        wait_block(slot)

    @pl.when((b == n_blocks - 1) & (j == nj - 1) & (nused == n_blocks))
    def _():
        wait_block(1 - slot)

    @pl.when((b >= nused) & (j == 0))
    def _():
        o_ref[...] = jnp.zeros_like(o_ref)


def _experts(blk_e, nused, valid, slot_tok, h, wg, wu, wd, *, n_blocks, tf):
    D = h.shape[1]
    dff = wg.shape[2]
    nj = dff // tf
    tok3 = slot_tok.reshape(n_blocks, 1, MOE_BLOCK)
    spare = -(-MOE_BLOCK // nj) * nj - MOE_BLOCK
    xrows = MOE_BLOCK + -(-spare // SUBLANES) * SUBLANES

    def jj(b, j, nu):
        return jnp.where(b < nu[0], j, nj - 1)

    def tok_spec(shift):
        return pl.BlockSpec((None, 1, MOE_BLOCK),
                            lambda b, j, be, nu, va: (jnp.minimum(b + shift, n_blocks - 1), 0, 0),
                            memory_space=pltpu.SMEM)

    return pl.pallas_call(
        functools.partial(_experts_kernel, n_blocks=n_blocks, nj=nj),
        grid_spec=pltpu.PrefetchScalarGridSpec(
            num_scalar_prefetch=3,
            grid=(n_blocks, nj),
            in_specs=[
                tok_spec(0), tok_spec(1),
                pl.BlockSpec(memory_space=pl.ANY),
                pl.BlockSpec((None, D, tf), lambda b, j, be, nu, va: (be[b], 0, jj(b, j, nu))),
                pl.BlockSpec((None, D, tf), lambda b, j, be, nu, va: (be[b], 0, jj(b, j, nu))),
                pl.BlockSpec((None, tf, D), lambda b, j, be, nu, va: (be[b], jj(b, j, nu), 0)),
            ],
            out_specs=pl.BlockSpec((MOE_BLOCK, D), lambda b, j, be, nu, va: (b, 0)),
            scratch_shapes=[pltpu.VMEM((2, xrows, D), F32), pltpu.VMEM((MOE_BLOCK, D), BF16),
                            pltpu.SemaphoreType.DMA((2,))],
        ),
        out_shape=jax.ShapeDtypeStruct((n_blocks * MOE_BLOCK, D), F32),
        compiler_params=_params(("arbitrary", "arbitrary")),
        name="experts",
    )(blk_e, nused, valid, tok3, tok3, h, wg, wu, wd)


def _combine_kernel(slot_cur_ref, slot_next_ref, gate_ref, x_ref, g_ref, yb_ref, o_ref, buf, sem, *, tc):
    i = pl.program_id(0)
    slot = i % 2

    def gather(slot_ref, s):
        def issue(t8, carry):
            for u in range(SUBLANES):
                t = t8 * SUBLANES + u
                for k in range(TOP_K):
                    pltpu.make_async_copy(yb_ref.at[pl.ds(slot_ref[k, t], 1)],
                                          buf.at[s, k, pl.ds(t, 1)], sem.at[s]).start()
            return carry

        lax.fori_loop(0, tc // SUBLANES, issue, 0)

    @pl.when(i == 0)
    def _():
        gather(slot_cur_ref, 0)

    has_next = i + 1 < pl.num_programs(0)

    @pl.when(has_next & (slot == 0))
    def _():
        gather(slot_next_ref, 1)

    @pl.when(has_next & (slot == 1))
    def _():
        gather(slot_next_ref, 0)

    for k in range(TOP_K):
        pltpu.make_async_copy(yb_ref.at[pl.ds(0, tc)], buf.at[slot, k], sem.at[slot]).wait()

    gates = gate_ref[...]
    moe = gates[:, 0:1] * buf[slot, 0] + gates[:, 1:2] * buf[slot, 1]
    o_ref[...] = x_ref[...] + g_ref[0] * moe


def _combine(slots, gates_t, x2d, g, yb, *, seq, tc):
    T, D = x2d.shape
    kern = functools.partial(_combine_kernel, tc=tc)
    return pl.pallas_call(
        kern,
        grid=(T // tc,),
        in_specs=[
            pl.BlockSpec((TOP_K, tc), lambda i: (0, i), memory_space=pltpu.SMEM),
            pl.BlockSpec((TOP_K, tc), lambda i: (0, jnp.minimum(i + 1, T // tc - 1)),
                         memory_space=pltpu.SMEM),
            pl.BlockSpec((tc, N_EXPERTS), lambda i: (i, 0)),
            pl.BlockSpec((tc, D), lambda i: (i, 0)),
            pl.BlockSpec((1, 1, D), lambda i: (i // (seq // tc), 0, 0)),
            pl.BlockSpec(memory_space=pl.ANY),
        ],
        out_specs=pl.BlockSpec((tc, D), lambda i: (i, 0)),
        out_shape=jax.ShapeDtypeStruct((T, D), F32),
        scratch_shapes=[pltpu.VMEM((2, TOP_K, tc, D), F32), pltpu.SemaphoreType.DMA((2,))],
        compiler_params=_params(("arbitrary",)),
        name="combine",
    )(slots, slots, gates_t, x2d, g, yb)


def _mod_params(c, w_mod, b_mod):
    B, D = c.shape
    c_pad = jnp.zeros((SUBLANES, D), F32).at[:B].set(c)
    mod = _adaln(c_pad, w_mod, b_mod)[:B]
    return [m.reshape(B, 1, D) for m in jnp.split(mod, 6, axis=-1)]


def kernel(x, c, positions, l0_w_mod, l0_b_mod, l0_w_in, l0_pool_w, l0_pool_scale, l0_conv_w, l0_conv_b, l0_conv_ln_g, l0_conv_ln_b, l0_w_out, l0_ffn_w_gate, l0_ffn_w_up, l0_ffn_w_down, l1_w_mod, l1_b_mod, l1_w_qkv, l1_q_norm, l1_k_norm, l1_sinks, l1_w_o, l1_router_w, l1_exp_w_gate, l1_exp_w_up, l1_exp_w_down):
    B, L, D = x.shape
    T = B * L
    bf = lambda w: w.astype(BF16)

    sh1, sc1, g1, sh2, sc2, g2 = _mod_params(c, l0_w_mod, l0_b_mod)
    x, wg0, wu0, wd0, w_qkv, w_o = _l0_mixer(
        x, sh1, sc1, g1, bf(l0_w_in), bf(l0_pool_w), l0_pool_scale, l0_conv_w, l0_conv_b, l0_conv_ln_g,
        l0_conv_ln_b, bf(l0_w_out), (l0_ffn_w_gate, l0_ffn_w_up, l0_ffn_w_down, l1_w_qkv, l1_w_o),
        tm=_fit(TM_MIXER, L))
    n_e, _, dfe = l1_exp_w_gate.shape
    x, wg_e = _l0_ffn(x.reshape(T, D), sh2, sc2, g2, wg0, wu0, wd0, l1_exp_w_gate.reshape(n_e * D, dfe),
                      seq=L, tm=_fit(TM_FFN, L), tf=_fit(TF_FFN, l0_ffn_w_gate.shape[1]))
    x = x.reshape(B, L, D)

    sh1, sc1, g1, sh2, sc2, g2 = _mod_params(c, l1_w_mod, l1_b_mod)
    half = HEAD_DIM // 2
    inv = ROPE_THETA ** (-jnp.arange(half, dtype=F32) / half)
    inv_t = jnp.tile(inv, LANES // half).reshape(1, LANES)
    qn_t = jnp.tile(l1_q_norm, LANES // HEAD_DIM).reshape(1, LANES)
    kn_t = jnp.tile(l1_k_norm, LANES // HEAD_DIM).reshape(1, LANES)
    lane = jnp.arange(LANES)
    bd = (lane[:, None] // HEAD_DIM == lane[None, :] // HEAD_DIM).astype(BF16)
    q, k, v, wd_e = _l1_qkv(x, sh1, sc1, positions.reshape(B, L, 1), inv_t, qn_t, kn_t, bd, w_qkv,
                            l1_exp_w_down.reshape(n_e * dfe, D), tm=_fit(TM_QKV, L))
    x, wu_e = _l1_attn(l1_sinks, q, k, v, x, g1, w_o, l1_exp_w_up.reshape(n_e * D, dfe),
                       tq=_fit(TQ_ATTN, L))

    x2d = x.reshape(T, D)
    h, meta, gates, cnt = _l1_router(x2d, sh2, sc2, l1_router_w.T, seq=L, tm=_fit(TM_ROUTER, L))
    counts = cnt[:, 0]
    padded = ((counts + MOE_BLOCK - 1) // MOE_BLOCK) * MOE_BLOCK
    pend = jnp.cumsum(padded)
    pstart = pend - padded
    n_blocks = (T * TOP_K + N_EXPERTS * (MOE_BLOCK - 1) + MOE_BLOCK - 1) // MOE_BLOCK
    nused = (pend[-1] // MOE_BLOCK).astype(jnp.int32).reshape(1)
    blk_start = jnp.minimum(jnp.arange(n_blocks, dtype=jnp.int32), nused[0] - 1) * MOE_BLOCK
    blk_e = jnp.minimum(jnp.sum(blk_start[:, None] >= pend[None, :], axis=1), N_EXPERTS - 1).astype(jnp.int32)
    blk_onehot = blk_e[:, None] == jnp.arange(N_EXPERTS, dtype=jnp.int32)
    blk_end = jnp.sum(jnp.where(blk_onehot, (pstart + counts).astype(jnp.int32), 0), axis=1)
    valid = jnp.clip(blk_end - blk_start, 0, MOE_BLOCK).astype(jnp.int32)
    sel = meta[:TOP_K, :, None] == jnp.arange(N_EXPERTS, dtype=jnp.int32)
    slots = meta[TOP_K:2 * TOP_K] + jnp.sum(jnp.where(sel, pstart.astype(jnp.int32), 0), axis=-1)
    n_slots = n_blocks * MOE_BLOCK
    pad_lo = jnp.concatenate([pstart + counts, pend[-1:]]).astype(jnp.int32)
    pad_hi = jnp.concatenate([pend, jnp.full((1,), n_slots, pend.dtype)]).astype(jnp.int32)
    slot_tok = _dispatch(pad_lo, pad_hi, slots, n_slots=n_slots, td=_fit(TD_DISPATCH, T))
    yb = _experts(blk_e, nused, valid, slot_tok, h, wg_e.reshape(n_e, D, dfe), wu_e.reshape(n_e, D, dfe),
                  wd_e.reshape(n_e, dfe, D), n_blocks=n_blocks, tf=_fit(TF_EXPERT, dfe))
    out = _combine(slots, gates.T, x2d, g2, yb, seq=L, tc=_fit(TC_COMBINE, L))
    return out.reshape(B, L, D)
```

```python
import functools

import jax
import jax.numpy as jnp
from jax import lax
from jax.experimental import pallas as pl
from jax.experimental.pallas import tpu as pltpu

F32 = jnp.float32
BF16 = jnp.bfloat16

HEAD_DIM = 64
Q_PER_KV = 8
POOL_WINDOWS = (2, 4, 8, 16)
CONV_WIDTH = 31
WINDOW = 128
ROPE_THETA = 10000.0
N_EXPERTS = 8
TOP_K = 2
NORM_EPS = 1e-6
LN_EPS = 1e-5
LOG2E = 1.4426950408889634

LANES = 128
SUBLANES = 8
V7X_VMEM_BYTES = 64 * 1024 * 1024
VMEM_LIMIT = V7X_VMEM_BYTES - 8 * 1024 * 1024
VMEM_LIMIT_ATTN = V7X_VMEM_BYTES - 4 * 1024 * 1024
HALO = 32
MOE_BLOCK = 512
NEG_INF = float("-inf")

TN_ADALN = 1024
TM_MIXER = 256
CONV_ROWS = 64
TM_FFN = 512
TF_FFN = 512
TM_QKV = 256
TQ_ATTN = 256
TM_ROUTER = 512
TD_DISPATCH = 2048
TF_EXPERT = 1024
TC_COMBINE = 512


def _fit(preferred, extent):
    tile = min(preferred, extent)
    while extent % tile:
        tile //= 2
    return tile


def _params(sem, vmem=VMEM_LIMIT):
    return pltpu.CompilerParams(dimension_semantics=sem, vmem_limit_bytes=vmem)


def _rms_modulate(x, shift, scale):
    ms = jnp.mean(x * x, axis=-1, keepdims=True)
    return x * lax.rsqrt(ms + NORM_EPS) * (1.0 + scale) + shift


def _dot(a, b):
    return jnp.dot(a, b, preferred_element_type=F32)


def _dot_nt(a, b):
    return lax.dot_general(a, b, (((1,), (1,)), ((), ())), preferred_element_type=F32)


def _adaln_kernel(c_ref, w_ref, b_ref, o_ref):
    sc = jax.nn.silu(c_ref[...]).astype(BF16)
    o_ref[...] = _dot(sc, w_ref[...].astype(BF16)) + b_ref[...]


def _adaln(c_pad, w_mod, b_mod):
    d, n = w_mod.shape
    tn = TN_ADALN
    return pl.pallas_call(
        _adaln_kernel,
        grid=(n // tn,),
        in_specs=[
            pl.BlockSpec((SUBLANES, d), lambda j: (0, 0)),
            pl.BlockSpec((d, tn), lambda j: (0, j)),
            pl.BlockSpec((1, tn), lambda j: (0, j)),
        ],
        out_specs=pl.BlockSpec((SUBLANES, tn), lambda j: (0, j)),
        out_shape=jax.ShapeDtypeStruct((SUBLANES, n), F32),
        compiler_params=_params(("arbitrary",)),
        name="adaln",
    )(c_pad, w_mod, b_mod.reshape(1, n))


def _l0_mixer_kernel(*refs, tm, d_pool, d_conv, gd, n_cast):
    (x_ref, sh_ref, sc_ref, g_ref, w_in_ref, pool_w_ref, pool_scale_ref,
     conv_w_ref, conv_b_ref, ln_g_ref, ln_b_ref, w_out_ref) = refs[:12]
    cast_in = refs[12:12 + n_cast]
    o_ref = refs[12 + n_cast]
    cast_out = refs[13 + n_cast:13 + 2 * n_cast]
    u_ext, glu_ext, shift_ref, y_ref, mixed_ref = refs[13 + 2 * n_cast:]
    l = pl.program_id(1)
    for cw_ref, cwo_ref in zip(cast_in, cast_out):
        cwo_ref[...] = cw_ref[...].astype(BF16)

    @pl.when(l == 0)
    def _():
        u_ext[0:HALO, :] = jnp.zeros((HALO, d_pool), F32)
        glu_ext[0:HALO, :] = jnp.zeros((HALO, d_conv), F32)

    @pl.when(l > 0)
    def _():
        u_ext[0:HALO, :] = u_ext[tm:tm + HALO, :]
        glu_ext[0:HALO, :] = glu_ext[tm:tm + HALO, :]

    x = x_ref[0]
    h = _rms_modulate(x, sh_ref[0], sc_ref[0]).astype(BF16)
    z = _dot(h, w_in_ref[...])
    u_ext[HALO:HALO + tm, :] = z[:, :d_pool]
    glu_ext[HALO:HALO + tm, :] = z[:, d_pool:d_pool + d_conv] * jax.nn.sigmoid(z[:, d_pool + d_conv:])

    t1 = l * tm + lax.broadcasted_iota(jnp.int32, (tm, 1), 0) + 1
    for g, w in enumerate(POOL_WINDOWS):
        c0 = g * gd
        tok = u_ext[HALO:HALO + tm, c0:c0 + gd]
        s = tok
        for k in range(1, w):
            s = s + u_ext[HALO - k:HALO - k + tm, c0:c0 + gd]
        inv_cnt = 1.0 / jnp.minimum(t1, w).astype(F32)
        pooled = s * inv_cnt - tok
        mixed = _dot(pooled.astype(BF16), pool_w_ref[g]) * pool_scale_ref[:, c0:c0 + gd]
        mixed_ref[:, c0:c0 + gd] = mixed.astype(BF16)

    span = HALO + tm - SUBLANES
    for s in range(1, SUBLANES):
        shift_ref[s - 1, 0:span, :] = glu_ext[s:s + span, :]

    rows = CONV_ROWS
    base = HALO - (CONV_WIDTH - 1)
    for r0 in range(0, tm, rows):
        for c0 in range(0, d_conv, LANES):
            acc = jnp.broadcast_to(conv_b_ref[:, c0:c0 + LANES], (rows, LANES))
            for j in range(CONV_WIDTH):
                s = (base + j) % SUBLANES
                a0 = base + j - s + r0
                if s == 0:
                    tap = glu_ext[a0:a0 + rows, c0:c0 + LANES]
                else:
                    tap = shift_ref[s - 1, a0:a0 + rows, c0:c0 + LANES]
                acc = acc + conv_w_ref[j:j + 1, c0:c0 + LANES] * tap
            y_ref[r0:r0 + rows, c0:c0 + LANES] = acc

    y = y_ref[...]
    mu = jnp.mean(y, axis=-1, keepdims=True)
    yc = y - mu
    var = jnp.mean(yc * yc, axis=-1, keepdims=True)
    ln = yc * lax.rsqrt(var + LN_EPS) * ln_g_ref[...] + ln_b_ref[...]
    mixed_ref[:, d_pool:] = jax.nn.silu(ln).astype(BF16)

    out = _dot(mixed_ref[...], w_out_ref[...])
    o_ref[0] = x + g_ref[0] * out


def _l0_mixer(x, sh, sc, g, w_in, pool_w, pool_scale, conv_w, conv_b, ln_g, ln_b, w_out, cast_ws, *, tm):
    B, L, D = x.shape
    d_pool = pool_scale.shape[0]
    d_conv = conv_b.shape[0]
    gd = d_pool // len(POOL_WINDOWS)
    const2 = lambda b, l: (0, 0)
    const3 = lambda b, l: (0, 0, 0)
    per_b = pl.BlockSpec((1, 1, D), lambda b, l: (b, 0, 0))
    kern = functools.partial(_l0_mixer_kernel, tm=tm, d_pool=d_pool, d_conv=d_conv, gd=gd,
                             n_cast=len(cast_ws))
    nl = L // tm
    riders = []
    for w in cast_ws:
        n = _rider_chunks(w.shape[0], B * nl)
        riders.append(_cast_rider(w, n, lambda b, l, n=n: jnp.minimum(b * nl + l, n - 1)))
    return pl.pallas_call(
        kern,
        grid=(B, nl),
        in_specs=[
            pl.BlockSpec((1, tm, D), lambda b, l: (b, l, 0)),
            per_b, per_b, per_b,
            pl.BlockSpec(w_in.shape, const2, pipeline_mode=pl.Buffered(1)),
            pl.BlockSpec(pool_w.shape, const3, pipeline_mode=pl.Buffered(1)),
            pl.BlockSpec((1, d_pool), const2),
            pl.BlockSpec((CONV_WIDTH, d_conv), const2),
            pl.BlockSpec((1, d_conv), const2),
            pl.BlockSpec((1, d_conv), const2),
            pl.BlockSpec((1, d_conv), const2),
            pl.BlockSpec(w_out.shape, const2, pipeline_mode=pl.Buffered(1)),
        ] + [r[0] for r in riders],
        out_specs=[pl.BlockSpec((1, tm, D), lambda b, l: (b, l, 0))] + [r[1] for r in riders],
        out_shape=[jax.ShapeDtypeStruct((B, L, D), F32)] + [r[2] for r in riders],
        scratch_shapes=[
            pltpu.VMEM((HALO + tm, d_pool), F32),
            pltpu.VMEM((HALO + tm, d_conv), F32),
            pltpu.VMEM((SUBLANES - 1, HALO + tm - SUBLANES, d_conv), F32),
            pltpu.VMEM((tm, d_conv), F32),
            pltpu.VMEM((tm, d_pool + d_conv), BF16),
        ],
        compiler_params=_params(("arbitrary", "arbitrary")),
        name="l0_mixer",
    )(x, sh, sc, g, w_in, pool_w, pool_scale.reshape(1, d_pool), conv_w.reshape(CONV_WIDTH, d_conv),
      conv_b.reshape(1, d_conv), ln_g.reshape(1, d_conv), ln_b.reshape(1, d_conv), w_out, *cast_ws)


def _cast_rider(w2d, n_chunks, chunk_of):
    rows, cols = w2d.shape
    assert rows % n_chunks == 0 and (rows // n_chunks) % (2 * SUBLANES) == 0
    spec = lambda: pl.BlockSpec((rows // n_chunks, cols), lambda *g: (chunk_of(*g), 0))
    return spec(), spec(), jax.ShapeDtypeStruct(w2d.shape, BF16)


def _rider_chunks(rows, steps):
    for n in range(steps, 0, -1):
        if rows % n == 0 and (rows // n) % (2 * SUBLANES) == 0:
            return n
    raise ValueError(f"{rows} rows cannot be chunked")


def _l0_ffn_kernel(x_ref, sh_ref, sc_ref, g_ref, wg_ref, wu_ref, wd_ref, cw_ref, o_ref, cwo_ref, h_ref):
    j = pl.program_id(1)

    @pl.when(j == 0)
    def _():
        h_ref[...] = _rms_modulate(x_ref[...], sh_ref[0], sc_ref[0]).astype(BF16)
        o_ref[...] = jnp.zeros_like(o_ref)

    h = h_ref[...]
    a = (jax.nn.silu(_dot(h, wg_ref[...])) * _dot(h, wu_ref[...])).astype(BF16)
    o_ref[...] += _dot(a, wd_ref[...])
    cwo_ref[...] = cw_ref[...].astype(BF16)

    @pl.when(j == pl.num_programs(1) - 1)
    def _():
        o_ref[...] = x_ref[...] + g_ref[0] * o_ref[...]


def _l0_ffn(x2d, sh, sc, g, wg, wu, wd, cast_w, *, seq, tm, tf):
    T, D = x2d.shape
    dff = wg.shape[1]
    nj = dff // tf
    per_b = pl.BlockSpec((1, 1, D), lambda i, j: (i // (seq // tm), 0, 0))
    jc = min(8, nj)
    cw_in, cw_out, cw_shape = _cast_rider(cast_w, (T // tm) * jc, lambda i, j: i * jc + jnp.minimum(j, jc - 1))
    return pl.pallas_call(
        _l0_ffn_kernel,
        grid=(T // tm, nj),
        in_specs=[
            pl.BlockSpec((tm, D), lambda i, j: (i, 0)),
            per_b, per_b, per_b,
            pl.BlockSpec((D, tf), lambda i, j: (0, j)),
            pl.BlockSpec((D, tf), lambda i, j: (0, j)),
            pl.BlockSpec((tf, D), lambda i, j: (j, 0)),
            cw_in,
        ],
        out_specs=[pl.BlockSpec((tm, D), lambda i, j: (i, 0)), cw_out],
        out_shape=[jax.ShapeDtypeStruct((T, D), F32), cw_shape],
        scratch_shapes=[pltpu.VMEM((tm, D), BF16)],
        compiler_params=_params(("arbitrary", "arbitrary")),
        name="l0_ffn",
    )(x2d, sh, sc, g, wg, wu, wd, cast_w)


def _l1_qkv_kernel(x_ref, sh_ref, sc_ref, pos_ref, inv_ref, qn_ref, kn_ref, bd_ref, w_ref, cw_ref,
                   q_ref, k_ref, v_ref, cwo_ref, *, n_q, n_kv):
    cwo_ref[...] = cw_ref[...].astype(BF16)
    x = x_ref[0]
    h = _rms_modulate(x, sh_ref[0], sc_ref[0]).astype(BF16)
    qkv = _dot(h, w_ref[...])

    ang = pos_ref[0].astype(F32) * inv_ref[...]
    cos = jnp.cos(ang)
    sin = jnp.sin(ang)
    lane = lax.broadcasted_iota(jnp.int32, (1, LANES), 1)
    first_half = (lane % HEAD_DIM) < (HEAD_DIM // 2)
    sin_signed = jnp.where(first_half, -sin, sin)
    low_head = lane < HEAD_DIM
    bd = bd_ref[...]

    def norm_rope(blk, nw):
        sq = blk * blk
        hi = sq.astype(BF16)
        lo = (sq - hi.astype(F32)).astype(BF16)
        ss = _dot(hi, bd) + _dot(lo, bd)
        n = blk * lax.rsqrt(ss + NORM_EPS) * nw
        partner = jnp.where(first_half, pltpu.roll(n, LANES - HEAD_DIM // 2, 1),
                            pltpu.roll(n, HEAD_DIM // 2, 1))
        return n * cos + partner * sin_signed

    def split_heads(blk):
        a_lo = jnp.where(low_head, blk, 0.0)
        b_hi = jnp.where(low_head, 0.0, blk)
        return (a_lo, pltpu.roll(a_lo, HEAD_DIM, 1), pltpu.roll(b_hi, HEAD_DIM, 1), b_hi)

    for cb in range(n_q):
        blk = qkv[:, cb * LANES:(cb + 1) * LANES]
        q_ref[0, :, cb * LANES:(cb + 1) * LANES] = norm_rope(blk, qn_ref[...]).astype(BF16)
    k0 = n_q * LANES
    v0 = k0 + n_kv * LANES
    for cb in range(n_kv):
        kr = norm_rope(qkv[:, k0 + cb * LANES:k0 + (cb + 1) * LANES], kn_ref[...])
        for i, part in enumerate(split_heads(kr)):
            k_ref[0, :, (4 * cb + i) * LANES:(4 * cb + i + 1) * LANES] = part.astype(BF16)
        vr = qkv[:, v0 + cb * LANES:v0 + (cb + 1) * LANES]
        for i, part in enumerate(split_heads(vr)):
            v_ref[0, :, (4 * cb + i) * LANES:(4 * cb + i + 1) * LANES] = part.astype(BF16)


def _l1_qkv(x, sh, sc, pos3, inv_t, qn_t, kn_t, bd, w_qkv, cast_w, *, tm):
    B, L, D = x.shape
    cw_in, cw_out, cw_shape = _cast_rider(cast_w, B * (L // tm), lambda b, l: b * (L // tm) + l)
    n_heads = D // HEAD_DIM
    n_kvh = n_heads // Q_PER_KV
    n_q = n_heads * HEAD_DIM // LANES
    n_kv = n_kvh * HEAD_DIM // LANES
    kw = n_kvh * 2 * LANES
    const2 = lambda b, l: (0, 0)
    per_b = pl.BlockSpec((1, 1, D), lambda b, l: (b, 0, 0))
    row = lambda w: pl.BlockSpec((1, tm, w), lambda b, l: (b, l, 0))
    kern = functools.partial(_l1_qkv_kernel, n_q=n_q, n_kv=n_kv)
    return pl.pallas_call(
        kern,
        grid=(B, L // tm),
        in_specs=[
            row(D), per_b, per_b, row(1),
            pl.BlockSpec((1, LANES), const2),
            pl.BlockSpec((1, LANES), const2),
            pl.BlockSpec((1, LANES), const2),
            pl.BlockSpec((LANES, LANES), const2),
            pl.BlockSpec(w_qkv.shape, const2, pipeline_mode=pl.Buffered(1)),
            cw_in,
        ],
        out_specs=[row(D), row(kw), row(kw), cw_out],
        out_shape=[jax.ShapeDtypeStruct((B, L, D), BF16),
                   jax.ShapeDtypeStruct((B, L, kw), BF16),
                   jax.ShapeDtypeStruct((B, L, kw), BF16),
                   cw_shape],
        compiler_params=_params(("arbitrary", "arbitrary")),
        name="l1_qkv",
    )(x, sh, sc, pos3, inv_t, qn_t, kn_t, bd, w_qkv, cast_w)


def _l1_attn_kernel(sinks_ref, q_ref, kc_ref, kp_ref, vc_ref, vp_ref, x_ref, g_ref, wo_ref, cw_ref,
                    o_ref, cwo_ref, attn_ref, *, tq, n_pairs):
    cwo_ref[...] = cw_ref[...].astype(BF16)
    i = pl.program_id(1)
    gp = Q_PER_KV // 2
    rows = gp * WINDOW
    qi = lax.broadcasted_iota(jnp.int32, (rows, WINDOW), 0) % WINDOW
    kj = lax.broadcasted_iota(jnp.int32, (rows, WINDOW), 1)
    pair_of_row = lax.broadcasted_iota(jnp.int32, (rows, 1), 0) // WINDOW
    mask_cur = kj <= qi
    mask_prev_band = kj > qi

    for n in range(tq // WINDOW):
        r0 = n * WINDOW
        if n == 0:
            k_prev, v_prev = kp_ref[0], vp_ref[0]
            mask_prev = kj > qi + jnp.where(i > 0, 0, WINDOW)
        else:
            k_prev, v_prev = kc_ref[0, r0 - WINDOW:r0, :], vc_ref[0, r0 - WINDOW:r0, :]
            mask_prev = mask_prev_band
        k_cur, v_cur = kc_ref[0, r0:r0 + WINDOW, :], vc_ref[0, r0:r0 + WINDOW, :]
        for kv in range(n_pairs // gp):
            p0 = kv * gp
            qs = jnp.concatenate([q_ref[0, r0:r0 + WINDOW, (p0 + j) * LANES:(p0 + j + 1) * LANES]
                                  for j in range(gp)], axis=0)
            acc = jnp.zeros((rows, LANES), F32)
            for half in range(2):
                c0 = (2 * kv + half) * LANES
                s_p = jnp.where(mask_prev, _dot_nt(qs, k_prev[:, c0:c0 + LANES]), NEG_INF)
                s_c = jnp.where(mask_cur, _dot_nt(qs, k_cur[:, c0:c0 + LANES]), NEG_INF)
                sink = jnp.full((rows, 1), sinks_ref[2 * p0 + half], F32)
                for j in range(1, gp):
                    sink = jnp.where(pair_of_row == j, sinks_ref[2 * (p0 + j) + half], sink)
                m = jnp.maximum(jnp.max(jnp.maximum(s_p, s_c), axis=1, keepdims=True), sink)
                e_p = jnp.exp2(s_p - m)
                e_c = jnp.exp2(s_c - m)
                denom = jnp.sum(e_p + e_c, axis=1, keepdims=True) + jnp.exp2(sink - m)
                o_h = _dot(e_p.astype(BF16), v_prev[:, c0:c0 + LANES]) + \
                    _dot(e_c.astype(BF16), v_cur[:, c0:c0 + LANES])
                acc = acc + o_h * (1.0 / denom)
            for j in range(gp):
                attn_ref[r0:r0 + WINDOW, (p0 + j) * LANES:(p0 + j + 1) * LANES] = \
                    acc[j * WINDOW:(j + 1) * WINDOW].astype(BF16)

    out = _dot(attn_ref[...], wo_ref[...])
    o_ref[0] = x_ref[0] + g_ref[0] * out


def _l1_attn(sinks, q, k, v, x, g, w_o, cast_w, *, tq):
    B, L, D = x.shape
    kw = k.shape[2]
    nb = tq // WINDOW
    cur = lambda w: pl.BlockSpec((1, tq, w), lambda b, i: (b, i, 0))
    prev = pl.BlockSpec((1, WINDOW, kw), lambda b, i: (b, jnp.maximum(i * nb - 1, 0), 0))
    kern = functools.partial(_l1_attn_kernel, tq=tq, n_pairs=D // LANES)
    cw_in, cw_out, cw_shape = _cast_rider(cast_w, B * (L // tq), lambda b, i: b * (L // tq) + i)
    return pl.pallas_call(
        kern,
        grid=(B, L // tq),
        in_specs=[
            pl.BlockSpec(memory_space=pltpu.SMEM),
            cur(D), cur(kw), prev, cur(kw), prev, cur(D),
            pl.BlockSpec((1, 1, D), lambda b, i: (b, 0, 0)),
            pl.BlockSpec(w_o.shape, lambda b, i: (0, 0), pipeline_mode=pl.Buffered(1)),
            cw_in,
        ],
        out_specs=[cur(D), cw_out],
        out_shape=[jax.ShapeDtypeStruct((B, L, D), F32), cw_shape],
        scratch_shapes=[pltpu.VMEM((tq, D), BF16)],
        compiler_params=_params(("arbitrary", "arbitrary"), vmem=VMEM_LIMIT_ATTN),
        name="l1_attn",
    )(sinks, q, k, k, v, v, x, g, w_o, cast_w)


def _l1_router_kernel(x_ref, sh_ref, sc_ref, rw_ref, h_ref, meta_ref, gate_ref, cnt_ref, carry,
                      *, tm):
    i = pl.program_id(0)

    @pl.when(i == 0)
    def _():
        carry[...] = jnp.zeros_like(carry)

    h = _rms_modulate(x_ref[...], sh_ref[0], sc_ref[0])
    h_ref[...] = h

    rw = rw_ref[...]
    h_hi = h.astype(BF16)
    h_lo = (h - h_hi.astype(F32)).astype(BF16)
    w_hi = rw.astype(BF16)
    w_lo = (rw - w_hi.astype(F32)).astype(BF16)
    logits = _dot_nt(w_hi, h_hi) + (_dot_nt(w_hi, h_lo) + _dot_nt(w_lo, h_hi))

    eidx = lax.broadcasted_iota(jnp.int32, (N_EXPERTS, tm), 0)
    m1 = jnp.max(logits, axis=0, keepdims=True)
    i1 = jnp.min(jnp.where(logits == m1, eidx, N_EXPERTS), axis=0, keepdims=True)
    rest = jnp.where(eidx == i1, NEG_INF, logits)
    m2 = jnp.max(rest, axis=0, keepdims=True)
    i2 = jnp.min(jnp.where(rest == m2, eidx, N_EXPERTS), axis=0, keepdims=True)
    e2 = jnp.exp(m2 - m1)
    gate1 = 1.0 / (1.0 + e2)
    gate2 = e2 / (1.0 + e2)

    sel1 = eidx == i1
    sel2 = eidx == i2
    ind = (sel1 | sel2).astype(F32)
    before = lax.broadcasted_iota(jnp.int32, (tm, tm), 0) < lax.broadcasted_iota(jnp.int32, (tm, tm), 1)
    excl = _dot(ind.astype(BF16), before.astype(BF16)) + carry[...]
    r1 = jnp.sum(jnp.where(sel1, excl, 0.0), axis=0, keepdims=True).astype(jnp.int32)
    r2 = jnp.sum(jnp.where(sel2, excl, 0.0), axis=0, keepdims=True).astype(jnp.int32)
    carry[...] = carry[...] + jnp.sum(ind, axis=1, keepdims=True)
    cnt_ref[...] = jnp.broadcast_to(carry[...], cnt_ref.shape).astype(jnp.int32)

    meta_ref[...] = jnp.where(eidx == 0, i1, jnp.where(eidx == 1, i2, jnp.where(eidx == 2, r1,
                              jnp.where(eidx == 3, r2, 0))))
    gate_ref[...] = jnp.where(eidx == 0, gate1, jnp.where(eidx == 1, gate2, 0.0))


def _l1_router(x2d, sh, sc, rw_t, *, seq, tm):
    T, D = x2d.shape
    per_b = pl.BlockSpec((1, 1, D), lambda i: (i // (seq // tm), 0, 0))
    kern = functools.partial(_l1_router_kernel, tm=tm)
    return pl.pallas_call(
        kern,
        grid=(T // tm,),
        in_specs=[
            pl.BlockSpec((tm, D), lambda i: (i, 0)),
            per_b, per_b,
            pl.BlockSpec((N_EXPERTS, D), lambda i: (0, 0)),
        ],
        out_specs=[
            pl.BlockSpec((tm, D), lambda i: (i, 0)),
            pl.BlockSpec((N_EXPERTS, tm), lambda i: (0, i)),
            pl.BlockSpec((N_EXPERTS, tm), lambda i: (0, i)),
            pl.BlockSpec((N_EXPERTS, LANES), lambda i: (0, 0)),
        ],
        out_shape=[
            jax.ShapeDtypeStruct((T, D), F32),
            jax.ShapeDtypeStruct((N_EXPERTS, T), jnp.int32),
            jax.ShapeDtypeStruct((N_EXPERTS, T), F32),
            jax.ShapeDtypeStruct((N_EXPERTS, LANES), jnp.int32),
        ],
        scratch_shapes=[pltpu.VMEM((N_EXPERTS, 1), F32)],
        compiler_params=_params(("arbitrary",)),
        name="l1_router",
    )(x2d, sh, sc, rw_t)


def _dispatch_kernel(pad_lo_ref, pad_hi_ref, slot_ref, tok_ref, *, td):
    i = pl.program_id(0)

    @pl.when(i == 0)
    def _():
        def clear(p, carry):
            tok_ref[p] = 0
            return carry

        for e in range(N_EXPERTS + 1):
            lax.fori_loop(pad_lo_ref[e], pad_hi_ref[e], clear, 0)

    def place(t, carry):
        for k in range(TOP_K):
            tok_ref[slot_ref[k, t]] = i * td + t
        return carry

    lax.fori_loop(0, td, place, 0, unroll=8)


def _dispatch(pad_lo, pad_hi, slots, *, n_slots, td):
    T = slots.shape[1]
    kern = functools.partial(_dispatch_kernel, td=td)
    return pl.pallas_call(
        kern,
        grid_spec=pltpu.PrefetchScalarGridSpec(
            num_scalar_prefetch=2,
            grid=(T // td,),
            in_specs=[pl.BlockSpec((TOP_K, td), lambda i, *_: (0, i), memory_space=pltpu.SMEM)],
            out_specs=pl.BlockSpec(memory_space=pltpu.SMEM),
        ),
        out_shape=jax.ShapeDtypeStruct((n_slots,), jnp.int32),
        compiler_params=_params(("arbitrary",)),
        name="dispatch",
    )(pad_lo, pad_hi, slots)


def _experts_kernel(blk_e_ref, nused_ref, valid_ref, tok_cur_ref, tok_next_ref, h_ref, wg_ref, wu_ref,
                    wd_ref, o_ref, xbuf, xs_ref, sem, *, n_blocks, nj):
    b = pl.program_id(0)
    j = pl.program_id(1)
    nused = nused_ref[0]
    slot = b % 2

    per_step = -(-MOE_BLOCK // nj)
    total = per_step * nj

    def row_copy(tok_ref, r, s):
        tok = tok_ref[0, jnp.minimum(r, MOE_BLOCK - 1)]
        return pltpu.make_async_copy(h_ref.at[pl.ds(tok, 1)], xbuf.at[s, pl.ds(r, 1)], sem.at[s])

    def wait_block(s):
        pltpu.make_async_copy(h_ref.at[pl.ds(0, MOE_BLOCK)], xbuf.at[s, pl.ds(0, MOE_BLOCK)],
                              sem.at[s]).wait()
        for r in range(MOE_BLOCK, total):
            pltpu.make_async_copy(h_ref.at[pl.ds(0, 1)], xbuf.at[s, pl.ds(r, 1)], sem.at[s]).wait()

    @pl.when(b < nused)
    def _():
        @pl.when(j == 0)
        def _():
            @pl.when(b == 0)
            def _():
                def issue(r, carry):
                    row_copy(tok_cur_ref, r, 0).start()
                    return carry

                lax.fori_loop(0, total, issue, 0)

            wait_block(slot)
            xs_ref[...] = xbuf[slot, 0:MOE_BLOCK].astype(BF16)
            o_ref[...] = jnp.zeros_like(o_ref)

        def ffn_rows(n):
            for u in range(per_step):
                row_copy(tok_next_ref, j * per_step + u, 1 - slot).start()

            xs = xs_ref[0:n]
            a = (jax.nn.silu(_dot(xs, wg_ref[...])) * _dot(xs, wu_ref[...])).astype(BF16)
            o_ref[0:n] += _dot(a, wd_ref[...])

        half_full = valid_ref[b] <= MOE_BLOCK // 2

        @pl.when(jnp.logical_not(half_full))
        def _():
            ffn_rows(MOE_BLOCK)

        @pl.when(half_full)
        def _():
            ffn_rows(MOE_BLOCK // 2)

    @pl.when((b == nused) & (j == 0))
    def _():
        wait_block(slot)

    @pl.when((b == n_blocks - 1) & (j == nj - 1) & (nused == n_blocks))
    def _():
        wait_block(1 - slot)

    @pl.when((b >= nused) & (j == 0))
    def _():
        o_ref[...] = jnp.zeros_like(o_ref)


def _experts(blk_e, nused, valid, slot_tok, h, wg, wu, wd, *, n_blocks, tf):
    D = h.shape[1]
    dff = wg.shape[2]
    nj = dff // tf
    tok3 = slot_tok.reshape(n_blocks, 1, MOE_BLOCK)
    spare = -(-MOE_BLOCK // nj) * nj - MOE_BLOCK
    xrows = MOE_BLOCK + -(-spare // SUBLANES) * SUBLANES

    def jj(b, j, nu):
        return jnp.where(b < nu[0], j, nj - 1)

    def tok_spec(shift):
        return pl.BlockSpec((None, 1, MOE_BLOCK),
                            lambda b, j, be, nu, va: (jnp.minimum(b + shift, n_blocks - 1), 0, 0),
                            memory_space=pltpu.SMEM)

    return pl.pallas_call(
        functools.partial(_experts_kernel, n_blocks=n_blocks, nj=nj),
        grid_spec=pltpu.PrefetchScalarGridSpec(
            num_scalar_prefetch=3,
            grid=(n_blocks, nj),
            in_specs=[
                tok_spec(0), tok_spec(1),
                pl.BlockSpec(memory_space=pl.ANY),
                pl.BlockSpec((None, D, tf), lambda b, j, be, nu, va: (be[b], 0, jj(b, j, nu))),
                pl.BlockSpec((None, D, tf), lambda b, j, be, nu, va: (be[b], 0, jj(b, j, nu))),
                pl.BlockSpec((None, tf, D), lambda b, j, be, nu, va: (be[b], jj(b, j, nu), 0)),
            ],
            out_specs=pl.BlockSpec((MOE_BLOCK, D), lambda b, j, be, nu, va: (b, 0)),
            scratch_shapes=[pltpu.VMEM((2, xrows, D), F32), pltpu.VMEM((MOE_BLOCK, D), BF16),
                            pltpu.SemaphoreType.DMA((2,))],
        ),
        out_shape=jax.ShapeDtypeStruct((n_blocks * MOE_BLOCK, D), F32),
        compiler_params=_params(("arbitrary", "arbitrary")),
        name="experts",
    )(blk_e, nused, valid, tok3, tok3, h, wg, wu, wd)


def _combine_kernel(slot_cur_ref, slot_next_ref, gate_ref, x_ref, g_ref, yb_ref, o_ref, buf, sem, *, tc):
    i = pl.program_id(0)
    slot = i % 2

    def gather(slot_ref, s):
        def issue(t8, carry):
            for u in range(SUBLANES):
                t = t8 * SUBLANES + u
                for k in range(TOP_K):
                    pltpu.make_async_copy(yb_ref.at[pl.ds(slot_ref[k, t], 1)],
                                          buf.at[s, k, pl.ds(t, 1)], sem.at[s]).start()
            return carry

        lax.fori_loop(0, tc // SUBLANES, issue, 0)

    @pl.when(i == 0)
    def _():
        gather(slot_cur_ref, 0)

    has_next = i + 1 < pl.num_programs(0)

    @pl.when(has_next & (slot == 0))
    def _():
        gather(slot_next_ref, 1)

    @pl.when(has_next & (slot == 1))
    def _():
        gather(slot_next_ref, 0)

    for k in range(TOP_K):
        pltpu.make_async_copy(yb_ref.at[pl.ds(0, tc)], buf.at[slot, k], sem.at[slot]).wait()

    gates = gate_ref[...]
    moe = gates[:, 0:1] * buf[slot, 0] + gates[:, 1:2] * buf[slot, 1]
    o_ref[...] = x_ref[...] + g_ref[0] * moe


def _combine(slots, gates_t, x2d, g, yb, *, seq, tc):
    T, D = x2d.shape
    kern = functools.partial(_combine_kernel, tc=tc)
    return pl.pallas_call(
        kern,
        grid=(T // tc,),
        in_specs=[
            pl.BlockSpec((TOP_K, tc), lambda i: (0, i), memory_space=pltpu.SMEM),
            pl.BlockSpec((TOP_K, tc), lambda i: (0, jnp.minimum(i + 1, T // tc - 1)),
                         memory_space=pltpu.SMEM),
            pl.BlockSpec((tc, N_EXPERTS), lambda i: (i, 0)),
            pl.BlockSpec((tc, D), lambda i: (i, 0)),
            pl.BlockSpec((1, 1, D), lambda i: (i // (seq // tc), 0, 0)),
            pl.BlockSpec(memory_space=pl.ANY),
        ],
        out_specs=pl.BlockSpec((tc, D), lambda i: (i, 0)),
        out_shape=jax.ShapeDtypeStruct((T, D), F32),
        scratch_shapes=[pltpu.VMEM((2, TOP_K, tc, D), F32), pltpu.SemaphoreType.DMA((2,))],
        compiler_params=_params(("arbitrary",)),
        name="combine",
    )(slots, slots, gates_t, x2d, g, yb)


def _mod_params(c, w_mod, b_mod):
    B, D = c.shape
    c_pad = jnp.zeros((SUBLANES, D), F32).at[:B].set(c)
    mod = _adaln(c_pad, w_mod, b_mod)[:B]
    return [m.reshape(B, 1, D) for m in jnp.split(mod, 6, axis=-1)]


def kernel(x, c, positions, l0_w_mod, l0_b_mod, l0_w_in, l0_pool_w, l0_pool_scale, l0_conv_w, l0_conv_b, l0_conv_ln_g, l0_conv_ln_b, l0_w_out, l0_ffn_w_gate, l0_ffn_w_up, l0_ffn_w_down, l1_w_mod, l1_b_mod, l1_w_qkv, l1_q_norm, l1_k_norm, l1_sinks, l1_w_o, l1_router_w, l1_exp_w_gate, l1_exp_w_up, l1_exp_w_down):
    B, L, D = x.shape
    T = B * L
    bf = lambda w: w.astype(BF16)

    sh1, sc1, g1, sh2, sc2, g2 = _mod_params(c, l0_w_mod, l0_b_mod)
    x, wg0, wu0, wd0, w_qkv, w_o = _l0_mixer(
        x, sh1, sc1, g1, bf(l0_w_in), bf(l0_pool_w), l0_pool_scale, l0_conv_w, l0_conv_b, l0_conv_ln_g,
        l0_conv_ln_b, bf(l0_w_out), (l0_ffn_w_gate, l0_ffn_w_up, l0_ffn_w_down, l1_w_qkv, l1_w_o),
        tm=_fit(TM_MIXER, L))
    n_e, _, dfe = l1_exp_w_gate.shape
    x, wg_e = _l0_ffn(x.reshape(T, D), sh2, sc2, g2, wg0, wu0, wd0, l1_exp_w_gate.reshape(n_e * D, dfe),
                      seq=L, tm=_fit(TM_FFN, L), tf=_fit(TF_FFN, l0_ffn_w_gate.shape[1]))
    x = x.reshape(B, L, D)

    sh1, sc1, g1, sh2, sc2, g2 = _mod_params(c, l1_w_mod, l1_b_mod)
    half = HEAD_DIM // 2
    inv = ROPE_THETA ** (-jnp.arange(half, dtype=F32) / half)
    inv_t = jnp.tile(inv, LANES // half).reshape(1, LANES)
    qn_t = jnp.tile(l1_q_norm * (HEAD_DIM ** -0.5 * LOG2E), LANES // HEAD_DIM).reshape(1, LANES)
    kn_t = jnp.tile(l1_k_norm, LANES // HEAD_DIM).reshape(1, LANES)
    sinks2 = l1_sinks * LOG2E
    lane = jnp.arange(LANES)
    bd = ((lane[:, None] // HEAD_DIM == lane[None, :] // HEAD_DIM) * (1.0 / HEAD_DIM)).astype(BF16)
    q, k, v, wd_e = _l1_qkv(x, sh1, sc1, positions.reshape(B, L, 1), inv_t, qn_t, kn_t, bd, w_qkv,
                            l1_exp_w_down.reshape(n_e * dfe, D), tm=_fit(TM_QKV, L))
    x, wu_e = _l1_attn(sinks2, q, k, v, x, g1, w_o, l1_exp_w_up.reshape(n_e * D, dfe),
                       tq=_fit(TQ_ATTN, L))

    x2d = x.reshape(T, D)
    h, meta, gates, cnt = _l1_router(x2d, sh2, sc2, l1_router_w.T, seq=L, tm=_fit(TM_ROUTER, L))
    counts = cnt[:, 0]
    padded = ((counts + MOE_BLOCK - 1) // MOE_BLOCK) * MOE_BLOCK
    pend = jnp.cumsum(padded)
    pstart = pend - padded
    n_blocks = (T * TOP_K + N_EXPERTS * (MOE_BLOCK - 1) + MOE_BLOCK - 1) // MOE_BLOCK
    nused = (pend[-1] // MOE_BLOCK).astype(jnp.int32).reshape(1)
    blk_start = jnp.minimum(jnp.arange(n_blocks, dtype=jnp.int32), nused[0] - 1) * MOE_BLOCK
    blk_e = jnp.minimum(jnp.sum(blk_start[:, None] >= pend[None, :], axis=1), N_EXPERTS - 1).astype(jnp.int32)
    blk_onehot = blk_e[:, None] == jnp.arange(N_EXPERTS, dtype=jnp.int32)
    blk_end = jnp.sum(jnp.where(blk_onehot, (pstart + counts).astype(jnp.int32), 0), axis=1)
    valid = jnp.clip(blk_end - blk_start, 0, MOE_BLOCK).astype(jnp.int32)
    sel = meta[:TOP_K, :, None] == jnp.arange(N_EXPERTS, dtype=jnp.int32)
    slots = meta[TOP_K:2 * TOP_K] + jnp.sum(jnp.where(sel, pstart.astype(jnp.int32), 0), axis=-1)
    n_slots = n_blocks * MOE_BLOCK
    pad_lo = jnp.concatenate([pstart + counts, pend[-1:]]).astype(jnp.int32)
    pad_hi = jnp.concatenate([pend, jnp.full((1,), n_slots, pend.dtype)]).astype(jnp.int32)
    slot_tok = _dispatch(pad_lo, pad_hi, slots, n_slots=n_slots, td=_fit(TD_DISPATCH, T))
    yb = _experts(blk_e, nused, valid, slot_tok, h, wg_e.reshape(n_e, D, dfe), wu_e.reshape(n_e, D, dfe),
                  wd_e.reshape(n_e, dfe, D), n_blocks=n_blocks, tf=_fit(TF_EXPERT, dfe))
    out = _combine(slots, gates.T, x2d, g2, yb, seq=L, tc=_fit(TC_COMBINE, L))
    return out.reshape(B, L, D)
```

```python
import functools

import jax
import jax.numpy as jnp
from jax import lax
from jax.experimental import pallas as pl
from jax.experimental.pallas import tpu as pltpu

F32 = jnp.float32
BF16 = jnp.bfloat16

HEAD_DIM = 64
Q_PER_KV = 8
POOL_WINDOWS = (2, 4, 8, 16)
CONV_WIDTH = 31
WINDOW = 128
ROPE_THETA = 10000.0
N_EXPERTS = 8
TOP_K = 2
NORM_EPS = 1e-6
LN_EPS = 1e-5
LOG2E = 1.4426950408889634

LANES = 128
SUBLANES = 8
V7X_VMEM_BYTES = 64 * 1024 * 1024
VMEM_LIMIT = V7X_VMEM_BYTES - 8 * 1024 * 1024
VMEM_LIMIT_ATTN = V7X_VMEM_BYTES - 4 * 1024 * 1024
HALO = 32
MOE_BLOCK = 512
NEG_INF = float("-inf")

TN_ADALN = 1024
TM_MIXER = 256
CONV_ROWS = 64
TM_FFN = 512
TF_FFN = 512
TM_QKV = 256
TQ_ATTN = 256
TM_ROUTER = 512
TD_DISPATCH = 2048
TF_EXPERT = 1024
TC_COMBINE = 512


def _fit(preferred, extent):
    tile = min(preferred, extent)
    while extent % tile:
        tile //= 2
    return tile


def _params(sem, vmem=VMEM_LIMIT):
    return pltpu.CompilerParams(dimension_semantics=sem, vmem_limit_bytes=vmem)


def _rms_modulate(x, shift, scale):
    ms = jnp.mean(x * x, axis=-1, keepdims=True)
    return x * lax.rsqrt(ms + NORM_EPS) * (1.0 + scale) + shift


def _dot(a, b):
    return jnp.dot(a, b, preferred_element_type=F32)


def _dot_nt(a, b):
    return lax.dot_general(a, b, (((1,), (1,)), ((), ())), preferred_element_type=F32)


def _adaln_kernel(c_ref, w_ref, b_ref, o_ref):
    sc = jax.nn.silu(c_ref[...]).astype(BF16)
    o_ref[...] = _dot(sc, w_ref[...].astype(BF16)) + b_ref[...]


def _adaln(c_pad, w_mod, b_mod):
    d, n = w_mod.shape
    tn = TN_ADALN
    return pl.pallas_call(
        _adaln_kernel,
        grid=(n // tn,),
        in_specs=[
            pl.BlockSpec((SUBLANES, d), lambda j: (0, 0)),
            pl.BlockSpec((d, tn), lambda j: (0, j)),
            pl.BlockSpec((1, tn), lambda j: (0, j)),
        ],
        out_specs=pl.BlockSpec((SUBLANES, tn), lambda j: (0, j)),
        out_shape=jax.ShapeDtypeStruct((SUBLANES, n), F32),
        compiler_params=_params(("arbitrary",)),
        name="adaln",
    )(c_pad, w_mod, b_mod.reshape(1, n))


def _l0_mixer_kernel(*refs, tm, d_pool, d_conv, gd, n_cast):
    (x_ref, sh_ref, sc_ref, g_ref, w_in_ref, pool_w_ref, pool_scale_ref,
     conv_w_ref, conv_b_ref, ln_g_ref, ln_b_ref, w_out_ref) = refs[:12]
    cast_in = refs[12:12 + n_cast]
    o_ref = refs[12 + n_cast]
    cast_out = refs[13 + n_cast:13 + 2 * n_cast]
    u_ext, glu_ext, shift_ref, y_ref, mixed_ref = refs[13 + 2 * n_cast:]
    l = pl.program_id(1)
    for cw_ref, cwo_ref in zip(cast_in, cast_out):
        cwo_ref[...] = cw_ref[...].astype(BF16)

    @pl.when(l == 0)
    def _():
        u_ext[0:HALO, :] = jnp.zeros((HALO, d_pool), F32)
        glu_ext[0:HALO, :] = jnp.zeros((HALO, d_conv), F32)

    @pl.when(l > 0)
    def _():
        u_ext[0:HALO, :] = u_ext[tm:tm + HALO, :]
        glu_ext[0:HALO, :] = glu_ext[tm:tm + HALO, :]

    x = x_ref[0]
    h = _rms_modulate(x, sh_ref[0], sc_ref[0]).astype(BF16)
    z = _dot(h, w_in_ref[...])
    u_ext[HALO:HALO + tm, :] = z[:, :d_pool]
    glu_ext[HALO:HALO + tm, :] = z[:, d_pool:d_pool + d_conv] * jax.nn.sigmoid(z[:, d_pool + d_conv:])

    t1 = l * tm + lax.broadcasted_iota(jnp.int32, (tm, 1), 0) + 1
    for g, w in enumerate(POOL_WINDOWS):
        c0 = g * gd
        tok = u_ext[HALO:HALO + tm, c0:c0 + gd]
        s = tok
        for k in range(1, w):
            s = s + u_ext[HALO - k:HALO - k + tm, c0:c0 + gd]
        inv_cnt = 1.0 / jnp.minimum(t1, w).astype(F32)
        pooled = s * inv_cnt - tok
        mixed = _dot(pooled.astype(BF16), pool_w_ref[g]) * pool_scale_ref[:, c0:c0 + gd]
        mixed_ref[:, c0:c0 + gd] = mixed.astype(BF16)

    span = HALO + tm - SUBLANES
    for s in range(1, SUBLANES):
        shift_ref[s - 1, 0:span, :] = glu_ext[s:s + span, :]

    rows = CONV_ROWS
    base = HALO - (CONV_WIDTH - 1)
    for r0 in range(0, tm, rows):
        for c0 in range(0, d_conv, LANES):
            acc = jnp.broadcast_to(conv_b_ref[:, c0:c0 + LANES], (rows, LANES))
            for j in range(CONV_WIDTH):
                s = (base + j) % SUBLANES
                a0 = base + j - s + r0
                if s == 0:
                    tap = glu_ext[a0:a0 + rows, c0:c0 + LANES]
                else:
                    tap = shift_ref[s - 1, a0:a0 + rows, c0:c0 + LANES]
                acc = acc + conv_w_ref[j:j + 1, c0:c0 + LANES] * tap
            y_ref[r0:r0 + rows, c0:c0 + LANES] = acc

    y = y_ref[...]
    mu = jnp.mean(y, axis=-1, keepdims=True)
    yc = y - mu
    var = jnp.mean(yc * yc, axis=-1, keepdims=True)
    ln = yc * lax.rsqrt(var + LN_EPS) * ln_g_ref[...] + ln_b_ref[...]
    mixed_ref[:, d_pool:] = jax.nn.silu(ln).astype(BF16)

    out = _dot(mixed_ref[...], w_out_ref[...])
    o_ref[0] = x + g_ref[0] * out


def _l0_mixer(x, sh, sc, g, w_in, pool_w, pool_scale, conv_w, conv_b, ln_g, ln_b, w_out, cast_ws, *, tm):
    B, L, D = x.shape
    d_pool = pool_scale.shape[0]
    d_conv = conv_b.shape[0]
    gd = d_pool // len(POOL_WINDOWS)
    const2 = lambda b, l: (0, 0)
    const3 = lambda b, l: (0, 0, 0)
    per_b = pl.BlockSpec((1, 1, D), lambda b, l: (b, 0, 0))
    kern = functools.partial(_l0_mixer_kernel, tm=tm, d_pool=d_pool, d_conv=d_conv, gd=gd,
                             n_cast=len(cast_ws))
    nl = L // tm
    riders = []
    for w in cast_ws:
        n = _rider_chunks(w.shape[0], B * nl)
        riders.append(_cast_rider(w, n, lambda b, l, n=n: jnp.minimum(b * nl + l, n - 1)))
    return pl.pallas_call(
        kern,
        grid=(B, nl),
        in_specs=[
            pl.BlockSpec((1, tm, D), lambda b, l: (b, l, 0)),
            per_b, per_b, per_b,
            pl.BlockSpec(w_in.shape, const2, pipeline_mode=pl.Buffered(1)),
            pl.BlockSpec(pool_w.shape, const3, pipeline_mode=pl.Buffered(1)),
            pl.BlockSpec((1, d_pool), const2),
            pl.BlockSpec((CONV_WIDTH, d_conv), const2),
            pl.BlockSpec((1, d_conv), const2),
            pl.BlockSpec((1, d_conv), const2),
            pl.BlockSpec((1, d_conv), const2),
            pl.BlockSpec(w_out.shape, const2, pipeline_mode=pl.Buffered(1)),
        ] + [r[0] for r in riders],
        out_specs=[pl.BlockSpec((1, tm, D), lambda b, l: (b, l, 0))] + [r[1] for r in riders],
        out_shape=[jax.ShapeDtypeStruct((B, L, D), F32)] + [r[2] for r in riders],
        scratch_shapes=[
            pltpu.VMEM((HALO + tm, d_pool), F32),
            pltpu.VMEM((HALO + tm, d_conv), F32),
            pltpu.VMEM((SUBLANES - 1, HALO + tm - SUBLANES, d_conv), F32),
            pltpu.VMEM((tm, d_conv), F32),
            pltpu.VMEM((tm, d_pool + d_conv), BF16),
        ],
        compiler_params=_params(("arbitrary", "arbitrary")),
        name="l0_mixer",
    )(x, sh, sc, g, w_in, pool_w, pool_scale.reshape(1, d_pool), conv_w.reshape(CONV_WIDTH, d_conv),
      conv_b.reshape(1, d_conv), ln_g.reshape(1, d_conv), ln_b.reshape(1, d_conv), w_out, *cast_ws)


def _cast_rider(w2d, n_chunks, chunk_of):
    rows, cols = w2d.shape
    assert rows % n_chunks == 0 and (rows // n_chunks) % (2 * SUBLANES) == 0
    spec = lambda: pl.BlockSpec((rows // n_chunks, cols), lambda *g: (chunk_of(*g), 0))
    return spec(), spec(), jax.ShapeDtypeStruct(w2d.shape, BF16)


def _rider_chunks(rows, steps):
    for n in range(steps, 0, -1):
        if rows % n == 0 and (rows // n) % (2 * SUBLANES) == 0:
            return n
    raise ValueError(f"{rows} rows cannot be chunked")


def _l0_ffn_kernel(x_ref, sh_ref, sc_ref, g_ref, wg_ref, wu_ref, wd_ref, cw_ref, o_ref, cwo_ref, h_ref):
    j = pl.program_id(1)

    @pl.when(j == 0)
    def _():
        h_ref[...] = _rms_modulate(x_ref[...], sh_ref[0], sc_ref[0]).astype(BF16)
        o_ref[...] = jnp.zeros_like(o_ref)

    h = h_ref[...]
    a = (jax.nn.silu(_dot(h, wg_ref[...])) * _dot(h, wu_ref[...])).astype(BF16)
    o_ref[...] += _dot(a, wd_ref[...])
    cwo_ref[...] = cw_ref[...].astype(BF16)

    @pl.when(j == pl.num_programs(1) - 1)
    def _():
        o_ref[...] = x_ref[...] + g_ref[0] * o_ref[...]


def _l0_ffn(x2d, sh, sc, g, wg, wu, wd, cast_w, *, seq, tm, tf):
    T, D = x2d.shape
    dff = wg.shape[1]
    nj = dff // tf
    per_b = pl.BlockSpec((1, 1, D), lambda i, j: (i // (seq // tm), 0, 0))
    jc = min(8, nj)
    cw_in, cw_out, cw_shape = _cast_rider(cast_w, (T // tm) * jc, lambda i, j: i * jc + jnp.minimum(j, jc - 1))
    return pl.pallas_call(
        _l0_ffn_kernel,
        grid=(T // tm, nj),
        in_specs=[
            pl.BlockSpec((tm, D), lambda i, j: (i, 0)),
            per_b, per_b, per_b,
            pl.BlockSpec((D, tf), lambda i, j: (0, j)),
            pl.BlockSpec((D, tf), lambda i, j: (0, j)),
            pl.BlockSpec((tf, D), lambda i, j: (j, 0)),
            cw_in,
        ],
        out_specs=[pl.BlockSpec((tm, D), lambda i, j: (i, 0)), cw_out],
        out_shape=[jax.ShapeDtypeStruct((T, D), F32), cw_shape],
        scratch_shapes=[pltpu.VMEM((tm, D), BF16)],
        compiler_params=_params(("arbitrary", "arbitrary")),
        name="l0_ffn",
    )(x2d, sh, sc, g, wg, wu, wd, cast_w)


def _l1_qkv_kernel(x_ref, sh_ref, sc_ref, pos_ref, inv_ref, qn_ref, kn_ref, bd_ref, w_ref, cw_ref,
                   q_ref, k_ref, v_ref, cwo_ref, *, n_q, n_kv):
    cwo_ref[...] = cw_ref[...].astype(BF16)
    x = x_ref[0]
    h = _rms_modulate(x, sh_ref[0], sc_ref[0]).astype(BF16)
    qkv = _dot(h, w_ref[...])

    ang = pos_ref[0].astype(F32) * inv_ref[...]
    cos = jnp.cos(ang)
    sin = jnp.sin(ang)
    lane = lax.broadcasted_iota(jnp.int32, (1, LANES), 1)
    first_half = (lane % HEAD_DIM) < (HEAD_DIM // 2)
    sin_signed = jnp.where(first_half, -sin, sin)
    low_head = lane < HEAD_DIM
    bd = bd_ref[...]

    def norm_rope(blk, nw):
        sq = blk * blk
        hi = sq.astype(BF16)
        lo = (sq - hi.astype(F32)).astype(BF16)
        ss = _dot(hi, bd) + _dot(lo, bd)
        n = blk * lax.rsqrt(ss + NORM_EPS) * nw
        partner = jnp.where(first_half, pltpu.roll(n, LANES - HEAD_DIM // 2, 1),
                            pltpu.roll(n, HEAD_DIM // 2, 1))
        return n * cos + partner * sin_signed

    def split_heads(blk):
        a_lo = jnp.where(low_head, blk, 0.0)
        b_hi = jnp.where(low_head, 0.0, blk)
        return (a_lo, pltpu.roll(a_lo, HEAD_DIM, 1), pltpu.roll(b_hi, HEAD_DIM, 1), b_hi)

    for cb in range(n_q):
        blk = qkv[:, cb * LANES:(cb + 1) * LANES]
        q_ref[0, :, cb * LANES:(cb + 1) * LANES] = norm_rope(blk, qn_ref[...]).astype(BF16)
    k0 = n_q * LANES
    v0 = k0 + n_kv * LANES
    for cb in range(n_kv):
        kr = norm_rope(qkv[:, k0 + cb * LANES:k0 + (cb + 1) * LANES], kn_ref[...])
        for i, part in enumerate(split_heads(kr)):
            k_ref[0, :, (4 * cb + i) * LANES:(4 * cb + i + 1) * LANES] = part.astype(BF16)
        vr = qkv[:, v0 + cb * LANES:v0 + (cb + 1) * LANES]
        for i, part in enumerate(split_heads(vr)):
            v_ref[0, :, (4 * cb + i) * LANES:(4 * cb + i + 1) * LANES] = part.astype(BF16)


def _l1_qkv(x, sh, sc, pos3, inv_t, qn_t, kn_t, bd, w_qkv, cast_w, *, tm):
    B, L, D = x.shape
    cw_in, cw_out, cw_shape = _cast_rider(cast_w, B * (L // tm), lambda b, l: b * (L // tm) + l)
    n_heads = D // HEAD_DIM
    n_kvh = n_heads // Q_PER_KV
    n_q = n_heads * HEAD_DIM // LANES
    n_kv = n_kvh * HEAD_DIM // LANES
    kw = n_kvh * 2 * LANES
    const2 = lambda b, l: (0, 0)
    per_b = pl.BlockSpec((1, 1, D), lambda b, l: (b, 0, 0))
    row = lambda w: pl.BlockSpec((1, tm, w), lambda b, l: (b, l, 0))
    kern = functools.partial(_l1_qkv_kernel, n_q=n_q, n_kv=n_kv)
    return pl.pallas_call(
        kern,
        grid=(B, L // tm),
        in_specs=[
            row(D), per_b, per_b, row(1),
            pl.BlockSpec((1, LANES), const2),
            pl.BlockSpec((1, LANES), const2),
            pl.BlockSpec((1, LANES), const2),
            pl.BlockSpec((LANES, LANES), const2),
            pl.BlockSpec(w_qkv.shape, const2, pipeline_mode=pl.Buffered(1)),
            cw_in,
        ],
        out_specs=[row(D), row(kw), row(kw), cw_out],
        out_shape=[jax.ShapeDtypeStruct((B, L, D), BF16),
                   jax.ShapeDtypeStruct((B, L, kw), BF16),
                   jax.ShapeDtypeStruct((B, L, kw), BF16),
                   cw_shape],
        compiler_params=_params(("arbitrary", "arbitrary")),
        name="l1_qkv",
    )(x, sh, sc, pos3, inv_t, qn_t, kn_t, bd, w_qkv, cast_w)


def _l1_attn_kernel(sinks_ref, q_ref, kc_ref, kp_ref, vc_ref, vp_ref, x_ref, g_ref, wo_ref, cw_ref,
                    o_ref, cwo_ref, attn_ref, *, tq, n_pairs):
    cwo_ref[...] = cw_ref[...].astype(BF16)
    i = pl.program_id(1)
    gp = Q_PER_KV // 2
    rows = gp * WINDOW
    qi = lax.broadcasted_iota(jnp.int32, (rows, WINDOW), 0) % WINDOW
    kj = lax.broadcasted_iota(jnp.int32, (rows, WINDOW), 1)
    pair_of_row = lax.broadcasted_iota(jnp.int32, (rows, 1), 0) // WINDOW
    mask_cur = kj <= qi
    mask_prev_band = kj > qi

    for n in range(tq // WINDOW):
        r0 = n * WINDOW
        if n == 0:
            k_prev, v_prev = kp_ref[0], vp_ref[0]
            mask_prev = kj > qi + jnp.where(i > 0, 0, WINDOW)
        else:
            k_prev, v_prev = kc_ref[0, r0 - WINDOW:r0, :], vc_ref[0, r0 - WINDOW:r0, :]
            mask_prev = mask_prev_band
        k_cur, v_cur = kc_ref[0, r0:r0 + WINDOW, :], vc_ref[0, r0:r0 + WINDOW, :]
        for kv in range(n_pairs // gp):
            p0 = kv * gp
            qs = jnp.concatenate([q_ref[0, r0:r0 + WINDOW, (p0 + j) * LANES:(p0 + j + 1) * LANES]
                                  for j in range(gp)], axis=0)
            acc = jnp.zeros((rows, LANES), F32)
            for half in range(2):
                c0 = (2 * kv + half) * LANES
                s_p = jnp.where(mask_prev, _dot_nt(qs, k_prev[:, c0:c0 + LANES]), NEG_INF)
                s_c = jnp.where(mask_cur, _dot_nt(qs, k_cur[:, c0:c0 + LANES]), NEG_INF)
                sink = jnp.full((rows, 1), sinks_ref[2 * p0 + half], F32)
                for j in range(1, gp):
                    sink = jnp.where(pair_of_row == j, sinks_ref[2 * (p0 + j) + half], sink)
                m = jnp.maximum(jnp.max(jnp.maximum(s_p, s_c), axis=1, keepdims=True), sink)
                e_p = jnp.exp2(s_p - m)
                e_c = jnp.exp2(s_c - m)
                denom = jnp.sum(e_p + e_c, axis=1, keepdims=True) + jnp.exp2(sink - m)
                o_h = _dot(e_p.astype(BF16), v_prev[:, c0:c0 + LANES]) + \
                    _dot(e_c.astype(BF16), v_cur[:, c0:c0 + LANES])
                acc = acc + o_h * (1.0 / denom)
            for j in range(gp):
                attn_ref[r0:r0 + WINDOW, (p0 + j) * LANES:(p0 + j + 1) * LANES] = \
                    acc[j * WINDOW:(j + 1) * WINDOW].astype(BF16)

    out = _dot(attn_ref[...], wo_ref[...])
    o_ref[0] = x_ref[0] + g_ref[0] * out


def _l1_attn(sinks, q, k, v, x, g, w_o, cast_w, *, tq):
    B, L, D = x.shape
    kw = k.shape[2]
    nb = tq // WINDOW
    cur = lambda w: pl.BlockSpec((1, tq, w), lambda b, i: (b, i, 0))
    prev = pl.BlockSpec((1, WINDOW, kw), lambda b, i: (b, jnp.maximum(i * nb - 1, 0), 0))
    kern = functools.partial(_l1_attn_kernel, tq=tq, n_pairs=D // LANES)
    cw_in, cw_out, cw_shape = _cast_rider(cast_w, B * (L // tq), lambda b, i: b * (L // tq) + i)
    return pl.pallas_call(
        kern,
        grid=(B, L // tq),
        in_specs=[
            pl.BlockSpec(memory_space=pltpu.SMEM),
            cur(D), cur(kw), prev, cur(kw), prev, cur(D),
            pl.BlockSpec((1, 1, D), lambda b, i: (b, 0, 0)),
            pl.BlockSpec(w_o.shape, lambda b, i: (0, 0), pipeline_mode=pl.Buffered(1)),
            cw_in,
        ],
        out_specs=[cur(D), cw_out],
        out_shape=[jax.ShapeDtypeStruct((B, L, D), F32), cw_shape],
        scratch_shapes=[pltpu.VMEM((tq, D), BF16)],
        compiler_params=_params(("arbitrary", "arbitrary"), vmem=VMEM_LIMIT_ATTN),
        name="l1_attn",
    )(sinks, q, k, k, v, v, x, g, w_o, cast_w)


def _l1_router_kernel(x_ref, sh_ref, sc_ref, rw_ref, h_ref, meta_ref, gate_ref, cnt_ref, carry,
                      *, tm):
    i = pl.program_id(0)

    @pl.when(i == 0)
    def _():
        carry[...] = jnp.zeros_like(carry)

    h = _rms_modulate(x_ref[...], sh_ref[0], sc_ref[0])
    h_ref[...] = h

    rw = rw_ref[...]
    h_hi = h.astype(BF16)
    h_lo = (h - h_hi.astype(F32)).astype(BF16)
    w_hi = rw.astype(BF16)
    w_lo = (rw - w_hi.astype(F32)).astype(BF16)
    logits = _dot_nt(w_hi, h_hi) + (_dot_nt(w_hi, h_lo) + _dot_nt(w_lo, h_hi))

    eidx = lax.broadcasted_iota(jnp.int32, (N_EXPERTS, tm), 0)
    m1 = jnp.max(logits, axis=0, keepdims=True)
    i1 = jnp.min(jnp.where(logits == m1, eidx, N_EXPERTS), axis=0, keepdims=True)
    rest = jnp.where(eidx == i1, NEG_INF, logits)
    m2 = jnp.max(rest, axis=0, keepdims=True)
    i2 = jnp.min(jnp.where(rest == m2, eidx, N_EXPERTS), axis=0, keepdims=True)
    e2 = jnp.exp(m2 - m1)
    gate1 = 1.0 / (1.0 + e2)
    gate2 = e2 / (1.0 + e2)

    sel1 = eidx == i1
    sel2 = eidx == i2
    ind = (sel1 | sel2).astype(F32)
    before = lax.broadcasted_iota(jnp.int32, (tm, tm), 0) < lax.broadcasted_iota(jnp.int32, (tm, tm), 1)
    excl = _dot(ind.astype(BF16), before.astype(BF16)) + carry[...]
    r1 = jnp.sum(jnp.where(sel1, excl, 0.0), axis=0, keepdims=True).astype(jnp.int32)
    r2 = jnp.sum(jnp.where(sel2, excl, 0.0), axis=0, keepdims=True).astype(jnp.int32)
    carry[...] = carry[...] + jnp.sum(ind, axis=1, keepdims=True)
    cnt_ref[...] = jnp.broadcast_to(carry[...], cnt_ref.shape).astype(jnp.int32)

    meta_ref[...] = jnp.where(eidx == 0, i1, jnp.where(eidx == 1, i2, jnp.where(eidx == 2, r1,
                              jnp.where(eidx == 3, r2, 0))))
    gate_ref[...] = jnp.where(eidx == 0, gate1, jnp.where(eidx == 1, gate2, 0.0))


def _l1_router(x2d, sh, sc, rw_t, *, seq, tm):
    T, D = x2d.shape
    per_b = pl.BlockSpec((1, 1, D), lambda i: (i // (seq // tm), 0, 0))
    kern = functools.partial(_l1_router_kernel, tm=tm)
    return pl.pallas_call(
        kern,
        grid=(T // tm,),
        in_specs=[
            pl.BlockSpec((tm, D), lambda i: (i, 0)),
            per_b, per_b,
            pl.BlockSpec((N_EXPERTS, D), lambda i: (0, 0)),
        ],
        out_specs=[
            pl.BlockSpec((tm, D), lambda i: (i, 0)),
            pl.BlockSpec((N_EXPERTS, tm), lambda i: (0, i)),
            pl.BlockSpec((N_EXPERTS, tm), lambda i: (0, i)),
            pl.BlockSpec((N_EXPERTS, LANES), lambda i: (0, 0)),
        ],
        out_shape=[
            jax.ShapeDtypeStruct((T, D), F32),
            jax.ShapeDtypeStruct((N_EXPERTS, T), jnp.int32),
            jax.ShapeDtypeStruct((N_EXPERTS, T), F32),
            jax.ShapeDtypeStruct((N_EXPERTS, LANES), jnp.int32),
        ],
        scratch_shapes=[pltpu.VMEM((N_EXPERTS, 1), F32)],
        compiler_params=_params(("arbitrary",)),
        name="l1_router",
    )(x2d, sh, sc, rw_t)


def _dispatch_kernel(pad_lo_ref, pad_hi_ref, slot_ref, tok_ref, *, td):
    i = pl.program_id(0)

    @pl.when(i == 0)
    def _():
        def clear(p, carry):
            tok_ref[p] = 0
            return carry

        for e in range(N_EXPERTS + 1):
            lax.fori_loop(pad_lo_ref[e], pad_hi_ref[e], clear, 0)

    def place(t, carry):
        for k in range(TOP_K):
            tok_ref[slot_ref[k, t]] = i * td + t
        return carry

    lax.fori_loop(0, td, place, 0, unroll=8)


def _dispatch(pad_lo, pad_hi, slots, *, n_slots, td):
    T = slots.shape[1]
    kern = functools.partial(_dispatch_kernel, td=td)
    return pl.pallas_call(
        kern,
        grid_spec=pltpu.PrefetchScalarGridSpec(
            num_scalar_prefetch=2,
            grid=(T // td,),
            in_specs=[pl.BlockSpec((TOP_K, td), lambda i, *_: (0, i), memory_space=pltpu.SMEM)],
            out_specs=pl.BlockSpec(memory_space=pltpu.SMEM),
        ),
        out_shape=jax.ShapeDtypeStruct((n_slots,), jnp.int32),
        compiler_params=_params(("arbitrary",)),
        name="dispatch",
    )(pad_lo, pad_hi, slots)


def _gather_per_step(nj):
    per_step = -(-MOE_BLOCK // max(nj - 1, 1))
    return -(-per_step // SUBLANES) * SUBLANES


def _experts_kernel(blk_e_ref, nused_ref, valid_ref, tok_cur_ref, tok_next_ref, h_ref, wg_ref, wu_ref,
                    wd_ref, o_ref, xbuf, xs_ref, sem, *, n_blocks, nj):
    b = pl.program_id(0)
    j = pl.program_id(1)
    nused = nused_ref[0]
    slot = b % 2

    per_step = _gather_per_step(nj)
    total = per_step * nj

    def row_copy(tok_ref, r, s):
        tok = tok_ref[0, jnp.minimum(r, MOE_BLOCK - 1)]
        return pltpu.make_async_copy(h_ref.at[pl.ds(tok, 1)], xbuf.at[s, pl.ds(r, 1)], sem.at[s])

    def wait_block(s):
        pltpu.make_async_copy(h_ref.at[pl.ds(0, MOE_BLOCK)], xbuf.at[s, pl.ds(0, MOE_BLOCK)],
                              sem.at[s]).wait()
        if total > MOE_BLOCK:
            pltpu.make_async_copy(h_ref.at[pl.ds(0, total - MOE_BLOCK)],
                                  xbuf.at[s, pl.ds(MOE_BLOCK, total - MOE_BLOCK)], sem.at[s]).wait()

    @pl.when(b < nused)
    def _():
        @pl.when(j == 0)
        def _():
            @pl.when(b == 0)
            def _():
                def issue(r, carry):
                    row_copy(tok_cur_ref, r, 0).start()
                    return carry

                lax.fori_loop(0, total, issue, 0)

            wait_block(slot)
            xs_ref[...] = xbuf[slot, 0:MOE_BLOCK].astype(BF16)
            o_ref[...] = jnp.zeros_like(o_ref)

        def ffn_rows(n):
            for u in range(per_step):
                row_copy(tok_next_ref, j * per_step + u, 1 - slot).start()

            xs = xs_ref[0:n]
            a = (jax.nn.silu(_dot(xs, wg_ref[...])) * _dot(xs, wu_ref[...])).astype(BF16)
            o_ref[0:n] += _dot(a, wd_ref[...])

        half_full = valid_ref[b] <= MOE_BLOCK // 2

        @pl.when(jnp.logical_not(half_full))
        def _():
            ffn_rows(MOE_BLOCK)

        @pl.when(half_full)
        def _():
            ffn_rows(MOE_BLOCK // 2)

    @pl.when((b == nused) & (j == 0))
    def _():
        wait_block(slot)

    @pl.when((b == n_blocks - 1) & (j == nj - 1) & (nused == n_blocks))
    def _():
        wait_block(1 - slot)

    @pl.when((b >= nused) & (j == 0))
    def _():
        o_ref[...] = jnp.zeros_like(o_ref)


def _experts(blk_e, nused, valid, slot_tok, h, wg, wu, wd, *, n_blocks, tf):
    D = h.shape[1]
    dff = wg.shape[2]
    nj = dff // tf
    tok3 = slot_tok.reshape(n_blocks, 1, MOE_BLOCK)
    spare = _gather_per_step(nj) * nj - MOE_BLOCK
    xrows = MOE_BLOCK + -(-spare // SUBLANES) * SUBLANES

    def jj(b, j, nu):
        return jnp.where(b < nu[0], j, nj - 1)

    def tok_spec(shift):
        return pl.BlockSpec((None, 1, MOE_BLOCK),
                            lambda b, j, be, nu, va: (jnp.minimum(b + shift, n_blocks - 1), 0, 0),
                            memory_space=pltpu.SMEM)

    return pl.pallas_call(
        functools.partial(_experts_kernel, n_blocks=n_blocks, nj=nj),
        grid_spec=pltpu.PrefetchScalarGridSpec(
            num_scalar_prefetch=3,
            grid=(n_blocks, nj),
            in_specs=[
                tok_spec(0), tok_spec(1),
                pl.BlockSpec(memory_space=pl.ANY),
                pl.BlockSpec((None, D, tf), lambda b, j, be, nu, va: (be[b], 0, jj(b, j, nu))),
                pl.BlockSpec((None, D, tf), lambda b, j, be, nu, va: (be[b], 0, jj(b, j, nu))),
                pl.BlockSpec((None, tf, D), lambda b, j, be, nu, va: (be[b], jj(b, j, nu), 0)),
            ],
            out_specs=pl.BlockSpec((MOE_BLOCK, D), lambda b, j, be, nu, va: (b, 0)),
            scratch_shapes=[pltpu.VMEM((2, xrows, D), F32), pltpu.VMEM((MOE_BLOCK, D), BF16),
                            pltpu.SemaphoreType.DMA((2,))],
        ),
        out_shape=jax.ShapeDtypeStruct((n_blocks * MOE_BLOCK, D), F32),
        compiler_params=_params(("arbitrary", "arbitrary")),
        name="experts",
    )(blk_e, nused, valid, tok3, tok3, h, wg, wu, wd)


def _combine_kernel(slot_cur_ref, slot_next_ref, gate_ref, x_ref, g_ref, yb_ref, o_ref, buf, sem, *, tc):
    i = pl.program_id(0)
    slot = i % 2

    def gather(slot_ref, s):
        def issue(t8, carry):
            for u in range(SUBLANES):
                t = t8 * SUBLANES + u
                for k in range(TOP_K):
                    pltpu.make_async_copy(yb_ref.at[pl.ds(slot_ref[k, t], 1)],
                                          buf.at[s, k, pl.ds(t, 1)], sem.at[s]).start()
            return carry

        lax.fori_loop(0, tc // SUBLANES, issue, 0)

    @pl.when(i == 0)
    def _():
        gather(slot_cur_ref, 0)

    has_next = i + 1 < pl.num_programs(0)

    @pl.when(has_next & (slot == 0))
    def _():
        gather(slot_next_ref, 1)

    @pl.when(has_next & (slot == 1))
    def _():
        gather(slot_next_ref, 0)

    for k in range(TOP_K):
        pltpu.make_async_copy(yb_ref.at[pl.ds(0, tc)], buf.at[slot, k], sem.at[slot]).wait()

    gates = gate_ref[...]
    moe = gates[:, 0:1] * buf[slot, 0] + gates[:, 1:2] * buf[slot, 1]
    o_ref[...] = x_ref[...] + g_ref[0] * moe


def _combine(slots, gates_t, x2d, g, yb, *, seq, tc):
    T, D = x2d.shape
    kern = functools.partial(_combine_kernel, tc=tc)
    return pl.pallas_call(
        kern,
        grid=(T // tc,),
        in_specs=[
            pl.BlockSpec((TOP_K, tc), lambda i: (0, i), memory_space=pltpu.SMEM),
            pl.BlockSpec((TOP_K, tc), lambda i: (0, jnp.minimum(i + 1, T // tc - 1)),
                         memory_space=pltpu.SMEM),
            pl.BlockSpec((tc, N_EXPERTS), lambda i: (i, 0)),
            pl.BlockSpec((tc, D), lambda i: (i, 0)),
            pl.BlockSpec((1, 1, D), lambda i: (i // (seq // tc), 0, 0)),
            pl.BlockSpec(memory_space=pl.ANY),
        ],
        out_specs=pl.BlockSpec((tc, D), lambda i: (i, 0)),
        out_shape=jax.ShapeDtypeStruct((T, D), F32),
        scratch_shapes=[pltpu.VMEM((2, TOP_K, tc, D), F32), pltpu.SemaphoreType.DMA((2,))],
        compiler_params=_params(("arbitrary",)),
        name="combine",
    )(slots, slots, gates_t, x2d, g, yb)


def _mod_params(c, w_mod, b_mod):
    B, D = c.shape
    c_pad = jnp.zeros((SUBLANES, D), F32).at[:B].set(c)
    mod = _adaln(c_pad, w_mod, b_mod)[:B]
    return [m.reshape(B, 1, D) for m in jnp.split(mod, 6, axis=-1)]


def kernel(x, c, positions, l0_w_mod, l0_b_mod, l0_w_in, l0_pool_w, l0_pool_scale, l0_conv_w, l0_conv_b, l0_conv_ln_g, l0_conv_ln_b, l0_w_out, l0_ffn_w_gate, l0_ffn_w_up, l0_ffn_w_down, l1_w_mod, l1_b_mod, l1_w_qkv, l1_q_norm, l1_k_norm, l1_sinks, l1_w_o, l1_router_w, l1_exp_w_gate, l1_exp_w_up, l1_exp_w_down):
    B, L, D = x.shape
    T = B * L
    bf = lambda w: w.astype(BF16)

    sh1, sc1, g1, sh2, sc2, g2 = _mod_params(c, l0_w_mod, l0_b_mod)
    x, wg0, wu0, wd0, w_qkv, w_o = _l0_mixer(
        x, sh1, sc1, g1, bf(l0_w_in), bf(l0_pool_w), l0_pool_scale, l0_conv_w, l0_conv_b, l0_conv_ln_g,
        l0_conv_ln_b, bf(l0_w_out), (l0_ffn_w_gate, l0_ffn_w_up, l0_ffn_w_down, l1_w_qkv, l1_w_o),
        tm=_fit(TM_MIXER, L))
    n_e, _, dfe = l1_exp_w_gate.shape
    x, wg_e = _l0_ffn(x.reshape(T, D), sh2, sc2, g2, wg0, wu0, wd0, l1_exp_w_gate.reshape(n_e * D, dfe),
                      seq=L, tm=_fit(TM_FFN, L), tf=_fit(TF_FFN, l0_ffn_w_gate.shape[1]))
    x = x.reshape(B, L, D)

    sh1, sc1, g1, sh2, sc2, g2 = _mod_params(c, l1_w_mod, l1_b_mod)
    half = HEAD_DIM // 2
    inv = ROPE_THETA ** (-jnp.arange(half, dtype=F32) / half)
    inv_t = jnp.tile(inv, LANES // half).reshape(1, LANES)
    qn_t = jnp.tile(l1_q_norm * (HEAD_DIM ** -0.5 * LOG2E), LANES // HEAD_DIM).reshape(1, LANES)
    kn_t = jnp.tile(l1_k_norm, LANES // HEAD_DIM).reshape(1, LANES)
    sinks2 = l1_sinks * LOG2E
    lane = jnp.arange(LANES)
    bd = ((lane[:, None] // HEAD_DIM == lane[None, :] // HEAD_DIM) * (1.0 / HEAD_DIM)).astype(BF16)
    q, k, v, wd_e = _l1_qkv(x, sh1, sc1, positions.reshape(B, L, 1), inv_t, qn_t, kn_t, bd, w_qkv,
                            l1_exp_w_down.reshape(n_e * dfe, D), tm=_fit(TM_QKV, L))
    x, wu_e = _l1_attn(sinks2, q, k, v, x, g1, w_o, l1_exp_w_up.reshape(n_e * D, dfe),
                       tq=_fit(TQ_ATTN, L))

    x2d = x.reshape(T, D)
    h, meta, gates, cnt = _l1_router(x2d, sh2, sc2, l1_router_w.T, seq=L, tm=_fit(TM_ROUTER, L))
    counts = cnt[:, 0]
    padded = ((counts + MOE_BLOCK - 1) // MOE_BLOCK) * MOE_BLOCK
    pend = jnp.cumsum(padded)
    pstart = pend - padded
    n_blocks = (T * TOP_K + N_EXPERTS * (MOE_BLOCK - 1) + MOE_BLOCK - 1) // MOE_BLOCK
    nused = (pend[-1] // MOE_BLOCK).astype(jnp.int32).reshape(1)
    blk_start = jnp.minimum(jnp.arange(n_blocks, dtype=jnp.int32), nused[0] - 1) * MOE_BLOCK
    blk_e = jnp.minimum(jnp.sum(blk_start[:, None] >= pend[None, :], axis=1), N_EXPERTS - 1).astype(jnp.int32)
    blk_onehot = blk_e[:, None] == jnp.arange(N_EXPERTS, dtype=jnp.int32)
    blk_end = jnp.sum(jnp.where(blk_onehot, (pstart + counts).astype(jnp.int32), 0), axis=1)
    valid = jnp.clip(blk_end - blk_start, 0, MOE_BLOCK).astype(jnp.int32)
    sel = meta[:TOP_K, :, None] == jnp.arange(N_EXPERTS, dtype=jnp.int32)
    slots = meta[TOP_K:2 * TOP_K] + jnp.sum(jnp.where(sel, pstart.astype(jnp.int32), 0), axis=-1)
    n_slots = n_blocks * MOE_BLOCK
    pad_lo = jnp.concatenate([pstart + counts, pend[-1:]]).astype(jnp.int32)
    pad_hi = jnp.concatenate([pend, jnp.full((1,), n_slots, pend.dtype)]).astype(jnp.int32)
    slot_tok = _dispatch(pad_lo, pad_hi, slots, n_slots=n_slots, td=_fit(TD_DISPATCH, T))
    yb = _experts(blk_e, nused, valid, slot_tok, h, wg_e.reshape(n_e, D, dfe), wu_e.reshape(n_e, D, dfe),
                  wd_e.reshape(n_e, dfe, D), n_blocks=n_blocks, tf=_fit(TF_EXPERT, dfe))
    out = _combine(slots, gates.T, x2d, g2, yb, seq=L, tc=_fit(TC_COMBINE, L))
    return out.reshape(B, L, D)
```

```python
import functools

import jax
import jax.numpy as jnp
from jax import lax
from jax.experimental import pallas as pl
from jax.experimental.pallas import tpu as pltpu

F32 = jnp.float32
BF16 = jnp.bfloat16

HEAD_DIM = 64
Q_PER_KV = 8
POOL_WINDOWS = (2, 4, 8, 16)
CONV_WIDTH = 31
WINDOW = 128
ROPE_THETA = 10000.0
N_EXPERTS = 8
TOP_K = 2
NORM_EPS = 1e-6
LN_EPS = 1e-5
LOG2E = 1.4426950408889634

LANES = 128
SUBLANES = 8
V7X_VMEM_BYTES = 64 * 1024 * 1024
VMEM_LIMIT = V7X_VMEM_BYTES - 8 * 1024 * 1024
VMEM_LIMIT_ATTN = V7X_VMEM_BYTES - 4 * 1024 * 1024
HALO = 32
MOE_BLOCK = 512
NEG_INF = float("-inf")

TN_ADALN = 1024
TM_MIXER = 256
CONV_ROWS = 64
TM_FFN = 512
TF_FFN = 512
TM_QKV = 256
TQ_ATTN = 256
ATTN_STACK = 2
TM_ROUTER = 512
TD_DISPATCH = 2048
TF_EXPERT = 1024
TC_COMBINE = 512


def _fit(preferred, extent):
    tile = min(preferred, extent)
    while extent % tile:
        tile //= 2
    return tile


def _params(sem, vmem=VMEM_LIMIT):
    return pltpu.CompilerParams(dimension_semantics=sem, vmem_limit_bytes=vmem)


def _rms_modulate(x, shift, scale):
    ms = jnp.mean(x * x, axis=-1, keepdims=True)
    return x * lax.rsqrt(ms + NORM_EPS) * (1.0 + scale) + shift


def _dot(a, b):
    return jnp.dot(a, b, preferred_element_type=F32)


def _dot_nt(a, b):
    return lax.dot_general(a, b, (((1,), (1,)), ((), ())), preferred_element_type=F32)


def _adaln_kernel(c_ref, w_ref, b_ref, o_ref):
    sc = jax.nn.silu(c_ref[...]).astype(BF16)
    o_ref[...] = _dot(sc, w_ref[...].astype(BF16)) + b_ref[...]


def _adaln(c_pad, w_mod, b_mod):
    d, n = w_mod.shape
    tn = TN_ADALN
    return pl.pallas_call(
        _adaln_kernel,
        grid=(n // tn,),
        in_specs=[
            pl.BlockSpec((SUBLANES, d), lambda j: (0, 0)),
            pl.BlockSpec((d, tn), lambda j: (0, j)),
            pl.BlockSpec((1, tn), lambda j: (0, j)),
        ],
        out_specs=pl.BlockSpec((SUBLANES, tn), lambda j: (0, j)),
        out_shape=jax.ShapeDtypeStruct((SUBLANES, n), F32),
        compiler_params=_params(("arbitrary",)),
        name="adaln",
    )(c_pad, w_mod, b_mod.reshape(1, n))


def _l0_mixer_kernel(*refs, tm, d_pool, d_conv, gd, n_cast):
    (x_ref, sh_ref, sc_ref, g_ref, w_in_ref, pool_w_ref, pool_scale_ref,
     conv_w_ref, conv_b_ref, ln_g_ref, ln_b_ref, w_out_ref) = refs[:12]
    cast_in = refs[12:12 + n_cast]
    o_ref = refs[12 + n_cast]
    cast_out = refs[13 + n_cast:13 + 2 * n_cast]
    u_ext, glu_ext, shift_ref, y_ref, mixed_ref = refs[13 + 2 * n_cast:]
    l = pl.program_id(1)
    for cw_ref, cwo_ref in zip(cast_in, cast_out):
        cwo_ref[...] = cw_ref[...].astype(BF16)

    @pl.when(l == 0)
    def _():
        u_ext[0:HALO, :] = jnp.zeros((HALO, d_pool), F32)
        glu_ext[0:HALO, :] = jnp.zeros((HALO, d_conv), F32)

    @pl.when(l > 0)
    def _():
        u_ext[0:HALO, :] = u_ext[tm:tm + HALO, :]
        glu_ext[0:HALO, :] = glu_ext[tm:tm + HALO, :]

    x = x_ref[0]
    h = _rms_modulate(x, sh_ref[0], sc_ref[0]).astype(BF16)
    z = _dot(h, w_in_ref[...])
    u_ext[HALO:HALO + tm, :] = z[:, :d_pool]
    glu_ext[HALO:HALO + tm, :] = z[:, d_pool:d_pool + d_conv] * jax.nn.sigmoid(z[:, d_pool + d_conv:])

    t1 = l * tm + lax.broadcasted_iota(jnp.int32, (tm, 1), 0) + 1
    for g, w in enumerate(POOL_WINDOWS):
        c0 = g * gd
        tok = u_ext[HALO:HALO + tm, c0:c0 + gd]
        s = tok
        for k in range(1, w):
            s = s + u_ext[HALO - k:HALO - k + tm, c0:c0 + gd]
        inv_cnt = 1.0 / jnp.minimum(t1, w).astype(F32)
        pooled = s * inv_cnt - tok
        mixed = _dot(pooled.astype(BF16), pool_w_ref[g]) * pool_scale_ref[:, c0:c0 + gd]
        mixed_ref[:, c0:c0 + gd] = mixed.astype(BF16)

    span = HALO + tm - SUBLANES
    for s in range(1, SUBLANES):
        shift_ref[s - 1, 0:span, :] = glu_ext[s:s + span, :]

    rows = CONV_ROWS
    base = HALO - (CONV_WIDTH - 1)
    for r0 in range(0, tm, rows):
        for c0 in range(0, d_conv, LANES):
            acc = jnp.broadcast_to(conv_b_ref[:, c0:c0 + LANES], (rows, LANES))
            for j in range(CONV_WIDTH):
                s = (base + j) % SUBLANES
                a0 = base + j - s + r0
                if s == 0:
                    tap = glu_ext[a0:a0 + rows, c0:c0 + LANES]
                else:
                    tap = shift_ref[s - 1, a0:a0 + rows, c0:c0 + LANES]
                acc = acc + conv_w_ref[j:j + 1, c0:c0 + LANES] * tap
            y_ref[r0:r0 + rows, c0:c0 + LANES] = acc

    y = y_ref[...]
    mu = jnp.mean(y, axis=-1, keepdims=True)
    yc = y - mu
    var = jnp.mean(yc * yc, axis=-1, keepdims=True)
    ln = yc * lax.rsqrt(var + LN_EPS) * ln_g_ref[...] + ln_b_ref[...]
    mixed_ref[:, d_pool:] = jax.nn.silu(ln).astype(BF16)

    out = _dot(mixed_ref[...], w_out_ref[...])
    o_ref[0] = x + g_ref[0] * out


def _l0_mixer(x, sh, sc, g, w_in, pool_w, pool_scale, conv_w, conv_b, ln_g, ln_b, w_out, cast_ws, *, tm):
    B, L, D = x.shape
    d_pool = pool_scale.shape[0]
    d_conv = conv_b.shape[0]
    gd = d_pool // len(POOL_WINDOWS)
    const2 = lambda b, l: (0, 0)
    const3 = lambda b, l: (0, 0, 0)
    per_b = pl.BlockSpec((1, 1, D), lambda b, l: (b, 0, 0))
    kern = functools.partial(_l0_mixer_kernel, tm=tm, d_pool=d_pool, d_conv=d_conv, gd=gd,
                             n_cast=len(cast_ws))
    nl = L // tm
    riders = []
    for w in cast_ws:
        n = _rider_chunks(w.shape[0], B * nl)
        riders.append(_cast_rider(w, n, lambda b, l, n=n: jnp.minimum(b * nl + l, n - 1)))
    return pl.pallas_call(
        kern,
        grid=(B, nl),
        in_specs=[
            pl.BlockSpec((1, tm, D), lambda b, l: (b, l, 0)),
            per_b, per_b, per_b,
            pl.BlockSpec(w_in.shape, const2, pipeline_mode=pl.Buffered(1)),
            pl.BlockSpec(pool_w.shape, const3, pipeline_mode=pl.Buffered(1)),
            pl.BlockSpec((1, d_pool), const2),
            pl.BlockSpec((CONV_WIDTH, d_conv), const2),
            pl.BlockSpec((1, d_conv), const2),
            pl.BlockSpec((1, d_conv), const2),
            pl.BlockSpec((1, d_conv), const2),
            pl.BlockSpec(w_out.shape, const2, pipeline_mode=pl.Buffered(1)),
        ] + [r[0] for r in riders],
        out_specs=[pl.BlockSpec((1, tm, D), lambda b, l: (b, l, 0))] + [r[1] for r in riders],
        out_shape=[jax.ShapeDtypeStruct((B, L, D), F32)] + [r[2] for r in riders],
        scratch_shapes=[
            pltpu.VMEM((HALO + tm, d_pool), F32),
            pltpu.VMEM((HALO + tm, d_conv), F32),
            pltpu.VMEM((SUBLANES - 1, HALO + tm - SUBLANES, d_conv), F32),
            pltpu.VMEM((tm, d_conv), F32),
            pltpu.VMEM((tm, d_pool + d_conv), BF16),
        ],
        compiler_params=_params(("arbitrary", "arbitrary")),
        name="l0_mixer",
    )(x, sh, sc, g, w_in, pool_w, pool_scale.reshape(1, d_pool), conv_w.reshape(CONV_WIDTH, d_conv),
      conv_b.reshape(1, d_conv), ln_g.reshape(1, d_conv), ln_b.reshape(1, d_conv), w_out, *cast_ws)


def _cast_rider(w2d, n_chunks, chunk_of):
    rows, cols = w2d.shape
    assert rows % n_chunks == 0 and (rows // n_chunks) % (2 * SUBLANES) == 0
    spec = lambda: pl.BlockSpec((rows // n_chunks, cols), lambda *g: (chunk_of(*g), 0))
    return spec(), spec(), jax.ShapeDtypeStruct(w2d.shape, BF16)


def _rider_chunks(rows, steps):
    for n in range(steps, 0, -1):
        if rows % n == 0 and (rows // n) % (2 * SUBLANES) == 0:
            return n
    raise ValueError(f"{rows} rows cannot be chunked")


def _l0_ffn_kernel(x_ref, sh_ref, sc_ref, g_ref, wg_ref, wu_ref, wd_ref, cw_ref, o_ref, cwo_ref, h_ref):
    j = pl.program_id(1)

    @pl.when(j == 0)
    def _():
        h_ref[...] = _rms_modulate(x_ref[...], sh_ref[0], sc_ref[0]).astype(BF16)
        o_ref[...] = jnp.zeros_like(o_ref)

    h = h_ref[...]
    a = (jax.nn.silu(_dot(h, wg_ref[...])) * _dot(h, wu_ref[...])).astype(BF16)
    o_ref[...] += _dot(a, wd_ref[...])
    cwo_ref[...] = cw_ref[...].astype(BF16)

    @pl.when(j == pl.num_programs(1) - 1)
    def _():
        o_ref[...] = x_ref[...] + g_ref[0] * o_ref[...]


def _l0_ffn(x2d, sh, sc, g, wg, wu, wd, cast_w, *, seq, tm, tf):
    T, D = x2d.shape
    dff = wg.shape[1]
    nj = dff // tf
    per_b = pl.BlockSpec((1, 1, D), lambda i, j: (i // (seq // tm), 0, 0))
    jc = min(8, nj)
    cw_in, cw_out, cw_shape = _cast_rider(cast_w, (T // tm) * jc, lambda i, j: i * jc + jnp.minimum(j, jc - 1))
    return pl.pallas_call(
        _l0_ffn_kernel,
        grid=(T // tm, nj),
        in_specs=[
            pl.BlockSpec((tm, D), lambda i, j: (i, 0)),
            per_b, per_b, per_b,
            pl.BlockSpec((D, tf), lambda i, j: (0, j)),
            pl.BlockSpec((D, tf), lambda i, j: (0, j)),
            pl.BlockSpec((tf, D), lambda i, j: (j, 0)),
            cw_in,
        ],
        out_specs=[pl.BlockSpec((tm, D), lambda i, j: (i, 0)), cw_out],
        out_shape=[jax.ShapeDtypeStruct((T, D), F32), cw_shape],
        scratch_shapes=[pltpu.VMEM((tm, D), BF16)],
        compiler_params=_params(("arbitrary", "arbitrary")),
        name="l0_ffn",
    )(x2d, sh, sc, g, wg, wu, wd, cast_w)


def _l1_qkv_kernel(x_ref, sh_ref, sc_ref, pos_ref, inv_ref, qn_ref, kn_ref, bd_ref, w_ref, cw_ref,
                   q_ref, k_ref, v_ref, cwo_ref, *, n_q, n_kv):
    cwo_ref[...] = cw_ref[...].astype(BF16)
    x = x_ref[0]
    h = _rms_modulate(x, sh_ref[0], sc_ref[0]).astype(BF16)
    qkv = _dot(h, w_ref[...])

    ang = pos_ref[0].astype(F32) * inv_ref[...]
    cos = jnp.cos(ang)
    sin = jnp.sin(ang)
    lane = lax.broadcasted_iota(jnp.int32, (1, LANES), 1)
    first_half = (lane % HEAD_DIM) < (HEAD_DIM // 2)
    sin_signed = jnp.where(first_half, -sin, sin)
    low_head = lane < HEAD_DIM
    bd = bd_ref[...]

    def norm_rope(blk, nw):
        sq = blk * blk
        hi = sq.astype(BF16)
        lo = (sq - hi.astype(F32)).astype(BF16)
        ss = _dot(hi, bd) + _dot(lo, bd)
        n = blk * lax.rsqrt(ss + NORM_EPS) * nw
        partner = jnp.where(first_half, pltpu.roll(n, LANES - HEAD_DIM // 2, 1),
                            pltpu.roll(n, HEAD_DIM // 2, 1))
        return n * cos + partner * sin_signed

    def split_heads(blk):
        a_lo = jnp.where(low_head, blk, 0.0)
        b_hi = jnp.where(low_head, 0.0, blk)
        return (a_lo, pltpu.roll(a_lo, HEAD_DIM, 1), pltpu.roll(b_hi, HEAD_DIM, 1), b_hi)

    for cb in range(n_q):
        blk = qkv[:, cb * LANES:(cb + 1) * LANES]
        q_ref[0, :, cb * LANES:(cb + 1) * LANES] = norm_rope(blk, qn_ref[...]).astype(BF16)
    k0 = n_q * LANES
    v0 = k0 + n_kv * LANES
    for cb in range(n_kv):
        kr = norm_rope(qkv[:, k0 + cb * LANES:k0 + (cb + 1) * LANES], kn_ref[...])
        for i, part in enumerate(split_heads(kr)):
            k_ref[0, :, (4 * cb + i) * LANES:(4 * cb + i + 1) * LANES] = part.astype(BF16)
        vr = qkv[:, v0 + cb * LANES:v0 + (cb + 1) * LANES]
        for i, part in enumerate(split_heads(vr)):
            v_ref[0, :, (4 * cb + i) * LANES:(4 * cb + i + 1) * LANES] = part.astype(BF16)


def _l1_qkv(x, sh, sc, pos3, inv_t, qn_t, kn_t, bd, w_qkv, cast_w, *, tm):
    B, L, D = x.shape
    cw_in, cw_out, cw_shape = _cast_rider(cast_w, B * (L // tm), lambda b, l: b * (L // tm) + l)
    n_heads = D // HEAD_DIM
    n_kvh = n_heads // Q_PER_KV
    n_q = n_heads * HEAD_DIM // LANES
    n_kv = n_kvh * HEAD_DIM // LANES
    kw = n_kvh * 2 * LANES
    const2 = lambda b, l: (0, 0)
    per_b = pl.BlockSpec((1, 1, D), lambda b, l: (b, 0, 0))
    row = lambda w: pl.BlockSpec((1, tm, w), lambda b, l: (b, l, 0))
    kern = functools.partial(_l1_qkv_kernel, n_q=n_q, n_kv=n_kv)
    return pl.pallas_call(
        kern,
        grid=(B, L // tm),
        in_specs=[
            row(D), per_b, per_b, row(1),
            pl.BlockSpec((1, LANES), const2),
            pl.BlockSpec((1, LANES), const2),
            pl.BlockSpec((1, LANES), const2),
            pl.BlockSpec((LANES, LANES), const2),
            pl.BlockSpec(w_qkv.shape, const2, pipeline_mode=pl.Buffered(1)),
            cw_in,
        ],
        out_specs=[row(D), row(kw), row(kw), cw_out],
        out_shape=[jax.ShapeDtypeStruct((B, L, D), BF16),
                   jax.ShapeDtypeStruct((B, L, kw), BF16),
                   jax.ShapeDtypeStruct((B, L, kw), BF16),
                   cw_shape],
        compiler_params=_params(("arbitrary", "arbitrary")),
        name="l1_qkv",
    )(x, sh, sc, pos3, inv_t, qn_t, kn_t, bd, w_qkv, cast_w)


def _l1_attn_kernel(sinks_ref, q_ref, kc_ref, kp_ref, vc_ref, vp_ref, x_ref, g_ref, wo_ref, cw_ref,
                    o_ref, cwo_ref, attn_ref, *, tq, n_pairs):
    cwo_ref[...] = cw_ref[...].astype(BF16)
    i = pl.program_id(1)
    gp = ATTN_STACK
    rows = gp * WINDOW
    qi = lax.broadcasted_iota(jnp.int32, (rows, WINDOW), 0) % WINDOW
    kj = lax.broadcasted_iota(jnp.int32, (rows, WINDOW), 1)
    pair_of_row = lax.broadcasted_iota(jnp.int32, (rows, 1), 0) // WINDOW
    mask_cur = kj <= qi
    mask_prev_band = kj > qi

    for n in range(tq // WINDOW):
        r0 = n * WINDOW
        if n == 0:
            k_prev, v_prev = kp_ref[0], vp_ref[0]
            mask_prev = kj > qi + jnp.where(i > 0, 0, WINDOW)
        else:
            k_prev, v_prev = kc_ref[0, r0 - WINDOW:r0, :], vc_ref[0, r0 - WINDOW:r0, :]
            mask_prev = mask_prev_band
        k_cur, v_cur = kc_ref[0, r0:r0 + WINDOW, :], vc_ref[0, r0:r0 + WINDOW, :]
        for grp in range(n_pairs // gp):
            p0 = grp * gp
            kv = 2 * p0 // Q_PER_KV
            qs = jnp.concatenate([q_ref[0, r0:r0 + WINDOW, (p0 + j) * LANES:(p0 + j + 1) * LANES]
                                  for j in range(gp)], axis=0)
            acc = jnp.zeros((rows, LANES), F32)
            for half in range(2):
                c0 = (2 * kv + half) * LANES
                s_p = jnp.where(mask_prev, _dot_nt(qs, k_prev[:, c0:c0 + LANES]), NEG_INF)
                s_c = jnp.where(mask_cur, _dot_nt(qs, k_cur[:, c0:c0 + LANES]), NEG_INF)
                sink = jnp.full((rows, 1), sinks_ref[2 * p0 + half], F32)
                for j in range(1, gp):
                    sink = jnp.where(pair_of_row == j, sinks_ref[2 * (p0 + j) + half], sink)
                m = jnp.maximum(jnp.max(jnp.maximum(s_p, s_c), axis=1, keepdims=True), sink)
                e_p = jnp.exp2(s_p - m)
                e_c = jnp.exp2(s_c - m)
                denom = jnp.sum(e_p + e_c, axis=1, keepdims=True) + jnp.exp2(sink - m)
                o_h = _dot(e_p.astype(BF16), v_prev[:, c0:c0 + LANES]) + \
                    _dot(e_c.astype(BF16), v_cur[:, c0:c0 + LANES])
                acc = acc + o_h * (1.0 / denom)
            for j in range(gp):
                attn_ref[r0:r0 + WINDOW, (p0 + j) * LANES:(p0 + j + 1) * LANES] = \
                    acc[j * WINDOW:(j + 1) * WINDOW].astype(BF16)

    out = _dot(attn_ref[...], wo_ref[...])
    o_ref[0] = x_ref[0] + g_ref[0] * out


def _l1_attn(sinks, q, k, v, x, g, w_o, cast_w, *, tq):
    B, L, D = x.shape
    kw = k.shape[2]
    nb = tq // WINDOW
    cur = lambda w: pl.BlockSpec((1, tq, w), lambda b, i: (b, i, 0))
    prev = pl.BlockSpec((1, WINDOW, kw), lambda b, i: (b, jnp.maximum(i * nb - 1, 0), 0))
    kern = functools.partial(_l1_attn_kernel, tq=tq, n_pairs=D // LANES)
    cw_in, cw_out, cw_shape = _cast_rider(cast_w, B * (L // tq), lambda b, i: b * (L // tq) + i)
    return pl.pallas_call(
        kern,
        grid=(B, L // tq),
        in_specs=[
            pl.BlockSpec(memory_space=pltpu.SMEM),
            cur(D), cur(kw), prev, cur(kw), prev, cur(D),
            pl.BlockSpec((1, 1, D), lambda b, i: (b, 0, 0)),
            pl.BlockSpec(w_o.shape, lambda b, i: (0, 0), pipeline_mode=pl.Buffered(1)),
            cw_in,
        ],
        out_specs=[cur(D), cw_out],
        out_shape=[jax.ShapeDtypeStruct((B, L, D), F32), cw_shape],
        scratch_shapes=[pltpu.VMEM((tq, D), BF16)],
        compiler_params=_params(("arbitrary", "arbitrary"), vmem=VMEM_LIMIT_ATTN),
        name="l1_attn",
    )(sinks, q, k, k, v, v, x, g, w_o, cast_w)


def _l1_router_kernel(x_ref, sh_ref, sc_ref, rw_ref, h_ref, meta_ref, gate_ref, cnt_ref, carry,
                      *, tm):
    i = pl.program_id(0)

    @pl.when(i == 0)
    def _():
        carry[...] = jnp.zeros_like(carry)

    h = _rms_modulate(x_ref[...], sh_ref[0], sc_ref[0])
    h_ref[...] = h

    rw = rw_ref[...]
    h_hi = h.astype(BF16)
    h_lo = (h - h_hi.astype(F32)).astype(BF16)
    w_hi = rw.astype(BF16)
    w_lo = (rw - w_hi.astype(F32)).astype(BF16)
    logits = _dot_nt(w_hi, h_hi) + (_dot_nt(w_hi, h_lo) + _dot_nt(w_lo, h_hi))

    eidx = lax.broadcasted_iota(jnp.int32, (N_EXPERTS, tm), 0)
    m1 = jnp.max(logits, axis=0, keepdims=True)
    i1 = jnp.min(jnp.where(logits == m1, eidx, N_EXPERTS), axis=0, keepdims=True)
    rest = jnp.where(eidx == i1, NEG_INF, logits)
    m2 = jnp.max(rest, axis=0, keepdims=True)
    i2 = jnp.min(jnp.where(rest == m2, eidx, N_EXPERTS), axis=0, keepdims=True)
    e2 = jnp.exp(m2 - m1)
    gate1 = 1.0 / (1.0 + e2)
    gate2 = e2 / (1.0 + e2)

    sel1 = eidx == i1
    sel2 = eidx == i2
    ind = (sel1 | sel2).astype(F32)
    before = lax.broadcasted_iota(jnp.int32, (tm, tm), 0) < lax.broadcasted_iota(jnp.int32, (tm, tm), 1)
    excl = _dot(ind.astype(BF16), before.astype(BF16)) + carry[...]
    r1 = jnp.sum(jnp.where(sel1, excl, 0.0), axis=0, keepdims=True).astype(jnp.int32)
    r2 = jnp.sum(jnp.where(sel2, excl, 0.0), axis=0, keepdims=True).astype(jnp.int32)
    carry[...] = carry[...] + jnp.sum(ind, axis=1, keepdims=True)
    cnt_ref[...] = jnp.broadcast_to(carry[...], cnt_ref.shape).astype(jnp.int32)

    meta_ref[...] = jnp.where(eidx == 0, i1, jnp.where(eidx == 1, i2, jnp.where(eidx == 2, r1,
                              jnp.where(eidx == 3, r2, 0))))
    gate_ref[...] = jnp.where(eidx == 0, gate1, jnp.where(eidx == 1, gate2, 0.0))


def _l1_router(x2d, sh, sc, rw_t, *, seq, tm):
    T, D = x2d.shape
    per_b = pl.BlockSpec((1, 1, D), lambda i: (i // (seq // tm), 0, 0))
    kern = functools.partial(_l1_router_kernel, tm=tm)
    return pl.pallas_call(
        kern,
        grid=(T // tm,),
        in_specs=[
            pl.BlockSpec((tm, D), lambda i: (i, 0)),
            per_b, per_b,
            pl.BlockSpec((N_EXPERTS, D), lambda i: (0, 0)),
        ],
        out_specs=[
            pl.BlockSpec((tm, D), lambda i: (i, 0)),
            pl.BlockSpec((N_EXPERTS, tm), lambda i: (0, i)),
            pl.BlockSpec((N_EXPERTS, tm), lambda i: (0, i)),
            pl.BlockSpec((N_EXPERTS, LANES), lambda i: (0, 0)),
        ],
        out_shape=[
            jax.ShapeDtypeStruct((T, D), F32),
            jax.ShapeDtypeStruct((N_EXPERTS, T), jnp.int32),
            jax.ShapeDtypeStruct((N_EXPERTS, T), F32),
            jax.ShapeDtypeStruct((N_EXPERTS, LANES), jnp.int32),
        ],
        scratch_shapes=[pltpu.VMEM((N_EXPERTS, 1), F32)],
        compiler_params=_params(("arbitrary",)),
        name="l1_router",
    )(x2d, sh, sc, rw_t)


def _dispatch_kernel(pad_lo_ref, pad_hi_ref, slot_ref, tok_ref, *, td, n_tokens):
    i = pl.program_id(0)

    @pl.when(i == 0)
    def _():
        def clear(p, carry):
            tok_ref[p] = lax.rem(p, n_tokens)
            return carry

        for e in range(N_EXPERTS + 1):
            lax.fori_loop(pad_lo_ref[e], pad_hi_ref[e], clear, 0)

    def place(t, carry):
        for k in range(TOP_K):
            tok_ref[slot_ref[k, t]] = i * td + t
        return carry

    lax.fori_loop(0, td, place, 0, unroll=8)


def _dispatch(pad_lo, pad_hi, slots, *, n_slots, td):
    T = slots.shape[1]
    kern = functools.partial(_dispatch_kernel, td=td, n_tokens=T)
    return pl.pallas_call(
        kern,
        grid_spec=pltpu.PrefetchScalarGridSpec(
            num_scalar_prefetch=2,
            grid=(T // td,),
            in_specs=[pl.BlockSpec((TOP_K, td), lambda i, *_: (0, i), memory_space=pltpu.SMEM)],
            out_specs=pl.BlockSpec(memory_space=pltpu.SMEM),
        ),
        out_shape=jax.ShapeDtypeStruct((n_slots,), jnp.int32),
        compiler_params=_params(("arbitrary",)),
        name="dispatch",
    )(pad_lo, pad_hi, slots)


def _experts_kernel(blk_e_ref, nused_ref, valid_ref, tok_cur_ref, tok_next_ref, h_ref, wg_ref, wu_ref,
                    wd_ref, o_ref, xbuf, xs_ref, sem, *, n_blocks, nj):
    b = pl.program_id(0)
    j = pl.program_id(1)
    nused = nused_ref[0]
    slot = b % 2

    per_step = -(-MOE_BLOCK // nj)
    total = per_step * nj

    def row_copy(tok_ref, r, s):
        tok = tok_ref[0, lax.rem(r, MOE_BLOCK)]
        return pltpu.make_async_copy(h_ref.at[pl.ds(tok, 1)], xbuf.at[s, pl.ds(r, 1)], sem.at[s])

    def wait_block(s):
        pltpu.make_async_copy(h_ref.at[pl.ds(0, MOE_BLOCK)], xbuf.at[s, pl.ds(0, MOE_BLOCK)],
                              sem.at[s]).wait()
        for r in range(MOE_BLOCK, total):
            pltpu.make_async_copy(h_ref.at[pl.ds(0, 1)], xbuf.at[s, pl.ds(r, 1)], sem.at[s]).wait()

    @pl.when(b < nused)
    def _():
        @pl.when(j == 0)
        def _():
            @pl.when(b == 0)
            def _():
                def issue(r, carry):
                    row_copy(tok_cur_ref, r, 0).start()
                    return carry

                lax.fori_loop(0, total, issue, 0)

            wait_block(slot)
            xs_ref[...] = xbuf[slot, 0:MOE_BLOCK].astype(BF16)
            o_ref[...] = jnp.zeros_like(o_ref)

        def ffn_rows(n):
            for u in range(per_step):
                row_copy(tok_next_ref, j * per_step + u, 1 - slot).start()

            xs = xs_ref[0:n]
            a = (jax.nn.silu(_dot(xs, wg_ref[...])) * _dot(xs, wu_ref[...])).astype(BF16)
            o_ref[0:n] += _dot(a, wd_ref[...])

        half_full = valid_ref[b] <= MOE_BLOCK // 2

        @pl.when(jnp.logical_not(half_full))
        def _():
            ffn_rows(MOE_BLOCK)

        @pl.when(half_full)
        def _():
            ffn_rows(MOE_BLOCK // 2)

    @pl.when((b == nused) & (j == 0))
    def _():
        wait_block(slot)

    @pl.when((b == n_blocks - 1) & (j == nj - 1) & (nused == n_blocks))
    def _():
        wait_block(1 - slot)

    @pl.when((b >= nused) & (j == 0))
    def _():
        o_ref[...] = jnp.zeros_like(o_ref)


def _experts(blk_e, nused, valid, slot_tok, h, wg, wu, wd, *, n_blocks, tf):
    D = h.shape[1]
    dff = wg.shape[2]
    nj = dff // tf
    tok3 = slot_tok.reshape(n_blocks, 1, MOE_BLOCK)
    spare = -(-MOE_BLOCK // nj) * nj - MOE_BLOCK
    xrows = MOE_BLOCK + -(-spare // SUBLANES) * SUBLANES

    def jj(b, j, nu):
        return jnp.where(b < nu[0], j, nj - 1)

    def tok_spec(shift):
        return pl.BlockSpec((None, 1, MOE_BLOCK),
                            lambda b, j, be, nu, va: (jnp.minimum(b + shift, n_blocks - 1), 0, 0),
                            memory_space=pltpu.SMEM)

    return pl.pallas_call(
        functools.partial(_experts_kernel, n_blocks=n_blocks, nj=nj),
        grid_spec=pltpu.PrefetchScalarGridSpec(
            num_scalar_prefetch=3,
            grid=(n_blocks, nj),
            in_specs=[
                tok_spec(0), tok_spec(1),
                pl.BlockSpec(memory_space=pl.ANY),
                pl.BlockSpec((None, D, tf), lambda b, j, be, nu, va: (be[b], 0, jj(b, j, nu))),
                pl.BlockSpec((None, D, tf), lambda b, j, be, nu, va: (be[b], 0, jj(b, j, nu))),
                pl.BlockSpec((None, tf, D), lambda b, j, be, nu, va: (be[b], jj(b, j, nu), 0)),
            ],
            out_specs=pl.BlockSpec((MOE_BLOCK, D), lambda b, j, be, nu, va: (b, 0)),
            scratch_shapes=[pltpu.VMEM((2, xrows, D), F32), pltpu.VMEM((MOE_BLOCK, D), BF16),
                            pltpu.SemaphoreType.DMA((2,))],
        ),
        out_shape=jax.ShapeDtypeStruct((n_blocks * MOE_BLOCK, D), F32),
        compiler_params=_params(("arbitrary", "arbitrary")),
        name="experts",
    )(blk_e, nused, valid, tok3, tok3, h, wg, wu, wd)


def _combine_kernel(slot_cur_ref, slot_next_ref, gate_ref, x_ref, g_ref, yb_ref, o_ref, buf, sem, *, tc):
    i = pl.program_id(0)
    slot = i % 2

    def gather(slot_ref, s):
        def issue(t8, carry):
            for u in range(SUBLANES):
                t = t8 * SUBLANES + u
                for k in range(TOP_K):
                    pltpu.make_async_copy(yb_ref.at[pl.ds(slot_ref[k, t], 1)],
                                          buf.at[s, k, pl.ds(t, 1)], sem.at[s]).start()
            return carry

        lax.fori_loop(0, tc // SUBLANES, issue, 0)

    @pl.when(i == 0)
    def _():
        gather(slot_cur_ref, 0)

    has_next = i + 1 < pl.num_programs(0)

    @pl.when(has_next & (slot == 0))
    def _():
        gather(slot_next_ref, 1)

    @pl.when(has_next & (slot == 1))
    def _():
        gather(slot_next_ref, 0)

    for k in range(TOP_K):
        pltpu.make_async_copy(yb_ref.at[pl.ds(0, tc)], buf.at[slot, k], sem.at[slot]).wait()

    gates = gate_ref[...]
    moe = gates[:, 0:1] * buf[slot, 0] + gates[:, 1:2] * buf[slot, 1]
    o_ref[...] = x_ref[...] + g_ref[0] * moe


def _combine(slots, gates_t, x2d, g, yb, *, seq, tc):
    T, D = x2d.shape
    kern = functools.partial(_combine_kernel, tc=tc)
    return pl.pallas_call(
        kern,
        grid=(T // tc,),
        in_specs=[
            pl.BlockSpec((TOP_K, tc), lambda i: (0, i), memory_space=pltpu.SMEM),
            pl.BlockSpec((TOP_K, tc), lambda i: (0, jnp.minimum(i + 1, T // tc - 1)),
                         memory_space=pltpu.SMEM),
            pl.BlockSpec((tc, N_EXPERTS), lambda i: (i, 0)),
            pl.BlockSpec((tc, D), lambda i: (i, 0)),
            pl.BlockSpec((1, 1, D), lambda i: (i // (seq // tc), 0, 0)),
            pl.BlockSpec(memory_space=pl.ANY),
        ],
        out_specs=pl.BlockSpec((tc, D), lambda i: (i, 0)),
        out_shape=jax.ShapeDtypeStruct((T, D), F32),
        scratch_shapes=[pltpu.VMEM((2, TOP_K, tc, D), F32), pltpu.SemaphoreType.DMA((2,))],
        compiler_params=_params(("arbitrary",)),
        name="combine",
    )(slots, slots, gates_t, x2d, g, yb)


def _mod_params(c, w_mod, b_mod):
    B, D = c.shape
    c_pad = jnp.zeros((SUBLANES, D), F32).at[:B].set(c)
    mod = _adaln(c_pad, w_mod, b_mod)[:B]
    return [m.reshape(B, 1, D) for m in jnp.split(mod, 6, axis=-1)]


def kernel(x, c, positions, l0_w_mod, l0_b_mod, l0_w_in, l0_pool_w, l0_pool_scale, l0_conv_w, l0_conv_b, l0_conv_ln_g, l0_conv_ln_b, l0_w_out, l0_ffn_w_gate, l0_ffn_w_up, l0_ffn_w_down, l1_w_mod, l1_b_mod, l1_w_qkv, l1_q_norm, l1_k_norm, l1_sinks, l1_w_o, l1_router_w, l1_exp_w_gate, l1_exp_w_up, l1_exp_w_down):
    B, L, D = x.shape
    T = B * L
    bf = lambda w: w.astype(BF16)

    sh1, sc1, g1, sh2, sc2, g2 = _mod_params(c, l0_w_mod, l0_b_mod)
    x, wg0, wu0, wd0, w_qkv, w_o = _l0_mixer(
        x, sh1, sc1, g1, bf(l0_w_in), bf(l0_pool_w), l0_pool_scale, l0_conv_w, l0_conv_b, l0_conv_ln_g,
        l0_conv_ln_b, bf(l0_w_out), (l0_ffn_w_gate, l0_ffn_w_up, l0_ffn_w_down, l1_w_qkv, l1_w_o),
        tm=_fit(TM_MIXER, L))
    n_e, _, dfe = l1_exp_w_gate.shape
    x, wg_e = _l0_ffn(x.reshape(T, D), sh2, sc2, g2, wg0, wu0, wd0, l1_exp_w_gate.reshape(n_e * D, dfe),
                      seq=L, tm=_fit(TM_FFN, L), tf=_fit(TF_FFN, l0_ffn_w_gate.shape[1]))
    x = x.reshape(B, L, D)

    sh1, sc1, g1, sh2, sc2, g2 = _mod_params(c, l1_w_mod, l1_b_mod)
    half = HEAD_DIM // 2
    inv = ROPE_THETA ** (-jnp.arange(half, dtype=F32) / half)
    inv_t = jnp.tile(inv, LANES // half).reshape(1, LANES)
    qn_t = jnp.tile(l1_q_norm * (HEAD_DIM ** -0.5 * LOG2E), LANES // HEAD_DIM).reshape(1, LANES)
    kn_t = jnp.tile(l1_k_norm, LANES // HEAD_DIM).reshape(1, LANES)
    sinks2 = l1_sinks * LOG2E
    lane = jnp.arange(LANES)
    bd = ((lane[:, None] // HEAD_DIM == lane[None, :] // HEAD_DIM) * (1.0 / HEAD_DIM)).astype(BF16)
    q, k, v, wd_e = _l1_qkv(x, sh1, sc1, positions.reshape(B, L, 1), inv_t, qn_t, kn_t, bd, w_qkv,
                            l1_exp_w_down.reshape(n_e * dfe, D), tm=_fit(TM_QKV, L))
    x, wu_e = _l1_attn(sinks2, q, k, v, x, g1, w_o, l1_exp_w_up.reshape(n_e * D, dfe),
                       tq=_fit(TQ_ATTN, L))

    x2d = x.reshape(T, D)
    h, meta, gates, cnt = _l1_router(x2d, sh2, sc2, l1_router_w.T, seq=L, tm=_fit(TM_ROUTER, L))
    counts = cnt[:, 0]
    padded = ((counts + MOE_BLOCK - 1) // MOE_BLOCK) * MOE_BLOCK
    pend = jnp.cumsum(padded)
    pstart = pend - padded
    n_blocks = (T * TOP_K + N_EXPERTS * (MOE_BLOCK - 1) + MOE_BLOCK - 1) // MOE_BLOCK
    nused = (pend[-1] // MOE_BLOCK).astype(jnp.int32).reshape(1)
    blk_start = jnp.minimum(jnp.arange(n_blocks, dtype=jnp.int32), nused[0] - 1) * MOE_BLOCK
    blk_e = jnp.minimum(jnp.sum(blk_start[:, None] >= pend[None, :], axis=1), N_EXPERTS - 1).astype(jnp.int32)
    blk_onehot = blk_e[:, None] == jnp.arange(N_EXPERTS, dtype=jnp.int32)
    blk_end = jnp.sum(jnp.where(blk_onehot, (pstart + counts).astype(jnp.int32), 0), axis=1)
    valid = jnp.clip(blk_end - blk_start, 0, MOE_BLOCK).astype(jnp.int32)
    sel = meta[:TOP_K, :, None] == jnp.arange(N_EXPERTS, dtype=jnp.int32)
    slots = meta[TOP_K:2 * TOP_K] + jnp.sum(jnp.where(sel, pstart.astype(jnp.int32), 0), axis=-1)
    n_slots = n_blocks * MOE_BLOCK
    pad_lo = jnp.concatenate([pstart + counts, pend[-1:]]).astype(jnp.int32)
    pad_hi = jnp.concatenate([pend, jnp.full((1,), n_slots, pend.dtype)]).astype(jnp.int32)
    slot_tok = _dispatch(pad_lo, pad_hi, slots, n_slots=n_slots, td=_fit(TD_DISPATCH, T))
    yb = _experts(blk_e, nused, valid, slot_tok, h, wg_e.reshape(n_e, D, dfe), wu_e.reshape(n_e, D, dfe),
                  wd_e.reshape(n_e, dfe, D), n_blocks=n_blocks, tf=_fit(TF_EXPERT, dfe))
    out = _combine(slots, gates.T, x2d, g2, yb, seq=L, tc=_fit(TC_COMBINE, L))
    return out.reshape(B, L, D)
```

```python
import functools

import jax
import jax.numpy as jnp
from jax import lax
from jax.experimental import pallas as pl
from jax.experimental.pallas import tpu as pltpu

F32 = jnp.float32
BF16 = jnp.bfloat16

HEAD_DIM = 64
Q_PER_KV = 8
POOL_WINDOWS = (2, 4, 8, 16)
CONV_WIDTH = 31
WINDOW = 128
ROPE_THETA = 10000.0
N_EXPERTS = 8
TOP_K = 2
NORM_EPS = 1e-6
LN_EPS = 1e-5
LOG2E = 1.4426950408889634

LANES = 128
SUBLANES = 8
V7X_VMEM_BYTES = 64 * 1024 * 1024
VMEM_LIMIT = V7X_VMEM_BYTES - 8 * 1024 * 1024
VMEM_LIMIT_ATTN = V7X_VMEM_BYTES - 4 * 1024 * 1024
HALO = 32
MOE_BLOCK = 512
NEG_INF = float("-inf")

TN_ADALN = 1024
TM_MIXER = 256
CONV_ROWS = 64
TM_FFN = 512
TF_FFN = 512
MODULATE_ROWS = 32
TM_QKV = 256
TQ_ATTN = 256
TM_ROUTER = 512
TD_DISPATCH = 2048
TF_EXPERT = 1024
TC_COMBINE = 512


def _fit(preferred, extent):
    tile = min(preferred, extent)
    while extent % tile:
        tile //= 2
    return tile


def _params(sem, vmem=VMEM_LIMIT):
    return pltpu.CompilerParams(dimension_semantics=sem, vmem_limit_bytes=vmem)


def _rms_modulate(x, shift, scale):
    ms = jnp.mean(x * x, axis=-1, keepdims=True)
    return x * lax.rsqrt(ms + NORM_EPS) * (1.0 + scale) + shift


def _dot(a, b):
    return jnp.dot(a, b, preferred_element_type=F32)


def _dot_nt(a, b):
    return lax.dot_general(a, b, (((1,), (1,)), ((), ())), preferred_element_type=F32)


def _adaln_kernel(c_ref, w_ref, b_ref, o_ref):
    sc = jax.nn.silu(c_ref[...]).astype(BF16)
    o_ref[...] = _dot(sc, w_ref[...].astype(BF16)) + b_ref[...]


def _adaln(c_pad, w_mod, b_mod):
    d, n = w_mod.shape
    tn = TN_ADALN
    return pl.pallas_call(
        _adaln_kernel,
        grid=(n // tn,),
        in_specs=[
            pl.BlockSpec((SUBLANES, d), lambda j: (0, 0)),
            pl.BlockSpec((d, tn), lambda j: (0, j)),
            pl.BlockSpec((1, tn), lambda j: (0, j)),
        ],
        out_specs=pl.BlockSpec((SUBLANES, tn), lambda j: (0, j)),
        out_shape=jax.ShapeDtypeStruct((SUBLANES, n), F32),
        compiler_params=_params(("arbitrary",)),
        name="adaln",
    )(c_pad, w_mod, b_mod.reshape(1, n))


def _l0_mixer_kernel(*refs, tm, d_pool, d_conv, gd, n_cast):
    (x_ref, sh_ref, sc_ref, g_ref, w_in_ref, pool_w_ref, pool_scale_ref,
     conv_w_ref, conv_b_ref, ln_g_ref, ln_b_ref, w_out_ref) = refs[:12]
    cast_in = refs[12:12 + n_cast]
    o_ref = refs[12 + n_cast]
    cast_out = refs[13 + n_cast:13 + 2 * n_cast]
    u_ext, glu_ext, shift_ref, y_ref, mixed_ref = refs[13 + 2 * n_cast:]
    l = pl.program_id(1)
    for cw_ref, cwo_ref in zip(cast_in, cast_out):
        cwo_ref[...] = cw_ref[...].astype(BF16)

    @pl.when(l == 0)
    def _():
        u_ext[0:HALO, :] = jnp.zeros((HALO, d_pool), F32)
        glu_ext[0:HALO, :] = jnp.zeros((HALO, d_conv), F32)

    @pl.when(l > 0)
    def _():
        u_ext[0:HALO, :] = u_ext[tm:tm + HALO, :]
        glu_ext[0:HALO, :] = glu_ext[tm:tm + HALO, :]

    x = x_ref[0]
    h = _rms_modulate(x, sh_ref[0], sc_ref[0]).astype(BF16)
    z = _dot(h, w_in_ref[...])
    u_ext[HALO:HALO + tm, :] = z[:, :d_pool]
    glu_ext[HALO:HALO + tm, :] = z[:, d_pool:d_pool + d_conv] * jax.nn.sigmoid(z[:, d_pool + d_conv:])

    t1 = l * tm + lax.broadcasted_iota(jnp.int32, (tm, 1), 0) + 1
    for g, w in enumerate(POOL_WINDOWS):
        c0 = g * gd
        tok = u_ext[HALO:HALO + tm, c0:c0 + gd]
        s = tok
        for k in range(1, w):
            s = s + u_ext[HALO - k:HALO - k + tm, c0:c0 + gd]
        inv_cnt = 1.0 / jnp.minimum(t1, w).astype(F32)
        pooled = s * inv_cnt - tok
        mixed = _dot(pooled.astype(BF16), pool_w_ref[g]) * pool_scale_ref[:, c0:c0 + gd]
        mixed_ref[:, c0:c0 + gd] = mixed.astype(BF16)

    span = HALO + tm - SUBLANES
    for s in range(1, SUBLANES):
        shift_ref[s - 1, 0:span, :] = glu_ext[s:s + span, :]

    rows = CONV_ROWS
    base = HALO - (CONV_WIDTH - 1)
    for r0 in range(0, tm, rows):
        for c0 in range(0, d_conv, LANES):
            acc = jnp.broadcast_to(conv_b_ref[:, c0:c0 + LANES], (rows, LANES))
            for j in range(CONV_WIDTH):
                s = (base + j) % SUBLANES
                a0 = base + j - s + r0
                if s == 0:
                    tap = glu_ext[a0:a0 + rows, c0:c0 + LANES]
                else:
                    tap = shift_ref[s - 1, a0:a0 + rows, c0:c0 + LANES]
                acc = acc + conv_w_ref[j:j + 1, c0:c0 + LANES] * tap
            y_ref[r0:r0 + rows, c0:c0 + LANES] = acc

    y = y_ref[...]
    mu = jnp.mean(y, axis=-1, keepdims=True)
    yc = y - mu
    var = jnp.mean(yc * yc, axis=-1, keepdims=True)
    ln = yc * lax.rsqrt(var + LN_EPS) * ln_g_ref[...] + ln_b_ref[...]
    mixed_ref[:, d_pool:] = jax.nn.silu(ln).astype(BF16)

    out = _dot(mixed_ref[...], w_out_ref[...])
    o_ref[0] = x + g_ref[0] * out


def _l0_mixer(x, sh, sc, g, w_in, pool_w, pool_scale, conv_w, conv_b, ln_g, ln_b, w_out, cast_ws, *, tm):
    B, L, D = x.shape
    d_pool = pool_scale.shape[0]
    d_conv = conv_b.shape[0]
    gd = d_pool // len(POOL_WINDOWS)
    const2 = lambda b, l: (0, 0)
    const3 = lambda b, l: (0, 0, 0)
    per_b = pl.BlockSpec((1, 1, D), lambda b, l: (b, 0, 0))
    kern = functools.partial(_l0_mixer_kernel, tm=tm, d_pool=d_pool, d_conv=d_conv, gd=gd,
                             n_cast=len(cast_ws))
    nl = L // tm
    riders = []
    for w in cast_ws:
        n = _rider_chunks(w.shape[0], B * nl)
        riders.append(_cast_rider(w, n, lambda b, l, n=n: jnp.minimum(b * nl + l, n - 1)))
    return pl.pallas_call(
        kern,
        grid=(B, nl),
        in_specs=[
            pl.BlockSpec((1, tm, D), lambda b, l: (b, l, 0)),
            per_b, per_b, per_b,
            pl.BlockSpec(w_in.shape, const2, pipeline_mode=pl.Buffered(1)),
            pl.BlockSpec(pool_w.shape, const3, pipeline_mode=pl.Buffered(1)),
            pl.BlockSpec((1, d_pool), const2),
            pl.BlockSpec((CONV_WIDTH, d_conv), const2),
            pl.BlockSpec((1, d_conv), const2),
            pl.BlockSpec((1, d_conv), const2),
            pl.BlockSpec((1, d_conv), const2),
            pl.BlockSpec(w_out.shape, const2, pipeline_mode=pl.Buffered(1)),
        ] + [r[0] for r in riders],
        out_specs=[pl.BlockSpec((1, tm, D), lambda b, l: (b, l, 0))] + [r[1] for r in riders],
        out_shape=[jax.ShapeDtypeStruct((B, L, D), F32)] + [r[2] for r in riders],
        scratch_shapes=[
            pltpu.VMEM((HALO + tm, d_pool), F32),
            pltpu.VMEM((HALO + tm, d_conv), F32),
            pltpu.VMEM((SUBLANES - 1, HALO + tm - SUBLANES, d_conv), F32),
            pltpu.VMEM((tm, d_conv), F32),
            pltpu.VMEM((tm, d_pool + d_conv), BF16),
        ],
        compiler_params=_params(("arbitrary", "arbitrary")),
        name="l0_mixer",
    )(x, sh, sc, g, w_in, pool_w, pool_scale.reshape(1, d_pool), conv_w.reshape(CONV_WIDTH, d_conv),
      conv_b.reshape(1, d_conv), ln_g.reshape(1, d_conv), ln_b.reshape(1, d_conv), w_out, *cast_ws)


def _cast_rider(w2d, n_chunks, chunk_of):
    rows, cols = w2d.shape
    assert rows % n_chunks == 0 and (rows // n_chunks) % (2 * SUBLANES) == 0
    spec = lambda: pl.BlockSpec((rows // n_chunks, cols), lambda *g: (chunk_of(*g), 0))
    return spec(), spec(), jax.ShapeDtypeStruct(w2d.shape, BF16)


def _rider_chunks(rows, steps):
    for n in range(steps, 0, -1):
        if rows % n == 0 and (rows // n) % (2 * SUBLANES) == 0:
            return n
    raise ValueError(f"{rows} rows cannot be chunked")


def _l0_ffn_kernel(x_ref, sh_ref, sc_ref, g_ref, wg_ref, wu_ref, wd_ref, cw_ref, o_ref, cwo_ref, h_ref):
    j = pl.program_id(1)

    @pl.when(j == 0)
    def _():
        for r0 in range(0, h_ref.shape[0], MODULATE_ROWS):
            rows = pl.ds(r0, MODULATE_ROWS)
            h_ref[rows, :] = _rms_modulate(x_ref[rows, :], sh_ref[0], sc_ref[0]).astype(BF16)
        o_ref[...] = jnp.zeros_like(o_ref)

    h = h_ref[...]
    a = (jax.nn.silu(_dot(h, wg_ref[...])) * _dot(h, wu_ref[...])).astype(BF16)
    o_ref[...] += _dot(a, wd_ref[...])
    cwo_ref[...] = cw_ref[...].astype(BF16)

    @pl.when(j == pl.num_programs(1) - 1)
    def _():
        o_ref[...] = x_ref[...] + g_ref[0] * o_ref[...]


def _l0_ffn(x2d, sh, sc, g, wg, wu, wd, cast_w, *, seq, tm, tf):
    T, D = x2d.shape
    dff = wg.shape[1]
    nj = dff // tf
    per_b = pl.BlockSpec((1, 1, D), lambda i, j: (i // (seq // tm), 0, 0))
    jc = min(8, nj)
    cw_in, cw_out, cw_shape = _cast_rider(cast_w, (T // tm) * jc, lambda i, j: i * jc + jnp.minimum(j, jc - 1))
    return pl.pallas_call(
        _l0_ffn_kernel,
        grid=(T // tm, nj),
        in_specs=[
            pl.BlockSpec((tm, D), lambda i, j: (i, 0)),
            per_b, per_b, per_b,
            pl.BlockSpec((D, tf), lambda i, j: (0, j)),
            pl.BlockSpec((D, tf), lambda i, j: (0, j)),
            pl.BlockSpec((tf, D), lambda i, j: (j, 0)),
            cw_in,
        ],
        out_specs=[pl.BlockSpec((tm, D), lambda i, j: (i, 0)), cw_out],
        out_shape=[jax.ShapeDtypeStruct((T, D), F32), cw_shape],
        scratch_shapes=[pltpu.VMEM((tm, D), BF16)],
        compiler_params=_params(("arbitrary", "arbitrary")),
        name="l0_ffn",
    )(x2d, sh, sc, g, wg, wu, wd, cast_w)


def _l1_qkv_kernel(x_ref, sh_ref, sc_ref, pos_ref, inv_ref, qn_ref, kn_ref, bd_ref, w_ref, cw_ref,
                   q_ref, k_ref, v_ref, cwo_ref, *, n_q, n_kv):
    cwo_ref[...] = cw_ref[...].astype(BF16)
    x = x_ref[0]
    h = _rms_modulate(x, sh_ref[0], sc_ref[0]).astype(BF16)
    qkv = _dot(h, w_ref[...])

    ang = pos_ref[0].astype(F32) * inv_ref[...]
    cos = jnp.cos(ang)
    sin = jnp.sin(ang)
    lane = lax.broadcasted_iota(jnp.int32, (1, LANES), 1)
    first_half = (lane % HEAD_DIM) < (HEAD_DIM // 2)
    sin_signed = jnp.where(first_half, -sin, sin)
    low_head = lane < HEAD_DIM
    bd = bd_ref[...]

    def norm_rope(blk, nw):
        sq = blk * blk
        hi = sq.astype(BF16)
        lo = (sq - hi.astype(F32)).astype(BF16)
        ss = _dot(hi, bd) + _dot(lo, bd)
        n = blk * lax.rsqrt(ss + NORM_EPS) * nw
        partner = jnp.where(first_half, pltpu.roll(n, LANES - HEAD_DIM // 2, 1),
                            pltpu.roll(n, HEAD_DIM // 2, 1))
        return n * cos + partner * sin_signed

    def split_heads(blk):
        a_lo = jnp.where(low_head, blk, 0.0)
        b_hi = jnp.where(low_head, 0.0, blk)
        return (a_lo, pltpu.roll(a_lo, HEAD_DIM, 1), pltpu.roll(b_hi, HEAD_DIM, 1), b_hi)

    for cb in range(n_q):
        blk = qkv[:, cb * LANES:(cb + 1) * LANES]
        q_ref[0, :, cb * LANES:(cb + 1) * LANES] = norm_rope(blk, qn_ref[...]).astype(BF16)
    k0 = n_q * LANES
    v0 = k0 + n_kv * LANES
    for cb in range(n_kv):
        kr = norm_rope(qkv[:, k0 + cb * LANES:k0 + (cb + 1) * LANES], kn_ref[...])
        for i, part in enumerate(split_heads(kr)):
            k_ref[0, :, (4 * cb + i) * LANES:(4 * cb + i + 1) * LANES] = part.astype(BF16)
        vr = qkv[:, v0 + cb * LANES:v0 + (cb + 1) * LANES]
        for i, part in enumerate(split_heads(vr)):
            v_ref[0, :, (4 * cb + i) * LANES:(4 * cb + i + 1) * LANES] = part.astype(BF16)


def _l1_qkv(x, sh, sc, pos3, inv_t, qn_t, kn_t, bd, w_qkv, cast_w, *, tm):
    B, L, D = x.shape
    cw_in, cw_out, cw_shape = _cast_rider(cast_w, B * (L // tm), lambda b, l: b * (L // tm) + l)
    n_heads = D // HEAD_DIM
    n_kvh = n_heads // Q_PER_KV
    n_q = n_heads * HEAD_DIM // LANES
    n_kv = n_kvh * HEAD_DIM // LANES
    kw = n_kvh * 2 * LANES
    const2 = lambda b, l: (0, 0)
    per_b = pl.BlockSpec((1, 1, D), lambda b, l: (b, 0, 0))
    row = lambda w: pl.BlockSpec((1, tm, w), lambda b, l: (b, l, 0))
    kern = functools.partial(_l1_qkv_kernel, n_q=n_q, n_kv=n_kv)
    return pl.pallas_call(
        kern,
        grid=(B, L // tm),
        in_specs=[
            row(D), per_b, per_b, row(1),
            pl.BlockSpec((1, LANES), const2),
            pl.BlockSpec((1, LANES), const2),
            pl.BlockSpec((1, LANES), const2),
            pl.BlockSpec((LANES, LANES), const2),
            pl.BlockSpec(w_qkv.shape, const2, pipeline_mode=pl.Buffered(1)),
            cw_in,
        ],
        out_specs=[row(D), row(kw), row(kw), cw_out],
        out_shape=[jax.ShapeDtypeStruct((B, L, D), BF16),
                   jax.ShapeDtypeStruct((B, L, kw), BF16),
                   jax.ShapeDtypeStruct((B, L, kw), BF16),
                   cw_shape],
        compiler_params=_params(("arbitrary", "arbitrary")),
        name="l1_qkv",
    )(x, sh, sc, pos3, inv_t, qn_t, kn_t, bd, w_qkv, cast_w)


def _l1_attn_kernel(sinks_ref, q_ref, kc_ref, kp_ref, vc_ref, vp_ref, x_ref, g_ref, wo_ref, cw_ref,
                    o_ref, cwo_ref, attn_ref, *, tq, n_pairs):
    cwo_ref[...] = cw_ref[...].astype(BF16)
    i = pl.program_id(1)
    gp = Q_PER_KV // 2
    rows = gp * WINDOW
    qi = lax.broadcasted_iota(jnp.int32, (rows, WINDOW), 0) % WINDOW
    kj = lax.broadcasted_iota(jnp.int32, (rows, WINDOW), 1)
    pair_of_row = lax.broadcasted_iota(jnp.int32, (rows, 1), 0) // WINDOW
    mask_cur = kj <= qi
    mask_prev_band = kj > qi

    for n in range(tq // WINDOW):
        r0 = n * WINDOW
        if n == 0:
            k_prev, v_prev = kp_ref[0], vp_ref[0]
            mask_prev = kj > qi + jnp.where(i > 0, 0, WINDOW)
        else:
            k_prev, v_prev = kc_ref[0, r0 - WINDOW:r0, :], vc_ref[0, r0 - WINDOW:r0, :]
            mask_prev = mask_prev_band
        k_cur, v_cur = kc_ref[0, r0:r0 + WINDOW, :], vc_ref[0, r0:r0 + WINDOW, :]
        for kv in range(n_pairs // gp):
            p0 = kv * gp
            qs = jnp.concatenate([q_ref[0, r0:r0 + WINDOW, (p0 + j) * LANES:(p0 + j + 1) * LANES]
                                  for j in range(gp)], axis=0)
            acc = jnp.zeros((rows, LANES), F32)
            for half in range(2):
                c0 = (2 * kv + half) * LANES
                s_p = jnp.where(mask_prev, _dot_nt(qs, k_prev[:, c0:c0 + LANES]), NEG_INF)
                s_c = jnp.where(mask_cur, _dot_nt(qs, k_cur[:, c0:c0 + LANES]), NEG_INF)
                sink = jnp.full((rows, 1), sinks_ref[2 * p0 + half], F32)
                for j in range(1, gp):
                    sink = jnp.where(pair_of_row == j, sinks_ref[2 * (p0 + j) + half], sink)
                m = jnp.maximum(jnp.max(jnp.maximum(s_p, s_c), axis=1, keepdims=True), sink)
                e_p = jnp.exp2(s_p - m)
                e_c = jnp.exp2(s_c - m)
                denom = jnp.sum(e_p + e_c, axis=1, keepdims=True) + jnp.exp2(sink - m)
                o_h = _dot(e_p.astype(BF16), v_prev[:, c0:c0 + LANES]) + \
                    _dot(e_c.astype(BF16), v_cur[:, c0:c0 + LANES])
                acc = acc + o_h * (1.0 / denom)
            for j in range(gp):
                attn_ref[r0:r0 + WINDOW, (p0 + j) * LANES:(p0 + j + 1) * LANES] = \
                    acc[j * WINDOW:(j + 1) * WINDOW].astype(BF16)

    out = _dot(attn_ref[...], wo_ref[...])
    o_ref[0] = x_ref[0] + g_ref[0] * out


def _l1_attn(sinks, q, k, v, x, g, w_o, cast_w, *, tq):
    B, L, D = x.shape
    kw = k.shape[2]
    nb = tq // WINDOW
    cur = lambda w: pl.BlockSpec((1, tq, w), lambda b, i: (b, i, 0))
    prev = pl.BlockSpec((1, WINDOW, kw), lambda b, i: (b, jnp.maximum(i * nb - 1, 0), 0))
    kern = functools.partial(_l1_attn_kernel, tq=tq, n_pairs=D // LANES)
    cw_in, cw_out, cw_shape = _cast_rider(cast_w, B * (L // tq), lambda b, i: b * (L // tq) + i)
    return pl.pallas_call(
        kern,
        grid=(B, L // tq),
        in_specs=[
            pl.BlockSpec(memory_space=pltpu.SMEM),
            cur(D), cur(kw), prev, cur(kw), prev, cur(D),
            pl.BlockSpec((1, 1, D), lambda b, i: (b, 0, 0)),
            pl.BlockSpec(w_o.shape, lambda b, i: (0, 0), pipeline_mode=pl.Buffered(1)),
            cw_in,
        ],
        out_specs=[cur(D), cw_out],
        out_shape=[jax.ShapeDtypeStruct((B, L, D), F32), cw_shape],
        scratch_shapes=[pltpu.VMEM((tq, D), BF16)],
        compiler_params=_params(("arbitrary", "arbitrary"), vmem=VMEM_LIMIT_ATTN),
        name="l1_attn",
    )(sinks, q, k, k, v, v, x, g, w_o, cast_w)


def _l1_router_kernel(x_ref, sh_ref, sc_ref, rw_ref, h_ref, meta_ref, gate_ref, cnt_ref, carry,
                      *, tm):
    i = pl.program_id(0)

    @pl.when(i == 0)
    def _():
        carry[...] = jnp.zeros_like(carry)

    h = _rms_modulate(x_ref[...], sh_ref[0], sc_ref[0])
    h_ref[...] = h

    rw = rw_ref[...]
    h_hi = h.astype(BF16)
    h_lo = (h - h_hi.astype(F32)).astype(BF16)
    w_hi = rw.astype(BF16)
    w_lo = (rw - w_hi.astype(F32)).astype(BF16)
    logits = _dot_nt(w_hi, h_hi) + (_dot_nt(w_hi, h_lo) + _dot_nt(w_lo, h_hi))

    eidx = lax.broadcasted_iota(jnp.int32, (N_EXPERTS, tm), 0)
    m1 = jnp.max(logits, axis=0, keepdims=True)
    i1 = jnp.min(jnp.where(logits == m1, eidx, N_EXPERTS), axis=0, keepdims=True)
    rest = jnp.where(eidx == i1, NEG_INF, logits)
    m2 = jnp.max(rest, axis=0, keepdims=True)
    i2 = jnp.min(jnp.where(rest == m2, eidx, N_EXPERTS), axis=0, keepdims=True)
    e2 = jnp.exp(m2 - m1)
    gate1 = 1.0 / (1.0 + e2)
    gate2 = e2 / (1.0 + e2)

    sel1 = eidx == i1
    sel2 = eidx == i2
    ind = (sel1 | sel2).astype(F32)
    before = lax.broadcasted_iota(jnp.int32, (tm, tm), 0) < lax.broadcasted_iota(jnp.int32, (tm, tm), 1)
    excl = _dot(ind.astype(BF16), before.astype(BF16)) + carry[...]
    r1 = jnp.sum(jnp.where(sel1, excl, 0.0), axis=0, keepdims=True).astype(jnp.int32)
    r2 = jnp.sum(jnp.where(sel2, excl, 0.0), axis=0, keepdims=True).astype(jnp.int32)
    carry[...] = carry[...] + jnp.sum(ind, axis=1, keepdims=True)
    cnt_ref[...] = jnp.broadcast_to(carry[...], cnt_ref.shape).astype(jnp.int32)

    meta_ref[...] = jnp.where(eidx == 0, i1, jnp.where(eidx == 1, i2, jnp.where(eidx == 2, r1,
                              jnp.where(eidx == 3, r2, 0))))
    gate_ref[...] = jnp.where(eidx == 0, gate1, jnp.where(eidx == 1, gate2, 0.0))


def _l1_router(x2d, sh, sc, rw_t, *, seq, tm):
    T, D = x2d.shape
    per_b = pl.BlockSpec((1, 1, D), lambda i: (i // (seq // tm), 0, 0))
    kern = functools.partial(_l1_router_kernel, tm=tm)
    return pl.pallas_call(
        kern,
        grid=(T // tm,),
        in_specs=[
            pl.BlockSpec((tm, D), lambda i: (i, 0)),
            per_b, per_b,
            pl.BlockSpec((N_EXPERTS, D), lambda i: (0, 0)),
        ],
        out_specs=[
            pl.BlockSpec((tm, D), lambda i: (i, 0)),
            pl.BlockSpec((N_EXPERTS, tm), lambda i: (0, i)),
            pl.BlockSpec((N_EXPERTS, tm), lambda i: (0, i)),
            pl.BlockSpec((N_EXPERTS, LANES), lambda i: (0, 0)),
        ],
        out_shape=[
            jax.ShapeDtypeStruct((T, D), F32),
            jax.ShapeDtypeStruct((N_EXPERTS, T), jnp.int32),
            jax.ShapeDtypeStruct((N_EXPERTS, T), F32),
            jax.ShapeDtypeStruct((N_EXPERTS, LANES), jnp.int32),
        ],
        scratch_shapes=[pltpu.VMEM((N_EXPERTS, 1), F32)],
        compiler_params=_params(("arbitrary",)),
        name="l1_router",
    )(x2d, sh, sc, rw_t)


def _dispatch_kernel(pad_lo_ref, pad_hi_ref, slot_ref, tok_ref, *, td, n_tokens):
    i = pl.program_id(0)

    @pl.when(i == 0)
    def _():
        def clear(p, carry):
            tok_ref[p] = lax.rem(p, n_tokens)
            return carry

        for e in range(N_EXPERTS + 1):
            lax.fori_loop(pad_lo_ref[e], pad_hi_ref[e], clear, 0)

    def place(t, carry):
        for k in range(TOP_K):
            tok_ref[slot_ref[k, t]] = i * td + t
        return carry

    lax.fori_loop(0, td, place, 0, unroll=8)


def _dispatch(pad_lo, pad_hi, slots, *, n_slots, td):
    T = slots.shape[1]
    kern = functools.partial(_dispatch_kernel, td=td, n_tokens=T)
    return pl.pallas_call(
        kern,
        grid_spec=pltpu.PrefetchScalarGridSpec(
            num_scalar_prefetch=2,
            grid=(T // td,),
            in_specs=[pl.BlockSpec((TOP_K, td), lambda i, *_: (0, i), memory_space=pltpu.SMEM)],
            out_specs=pl.BlockSpec(memory_space=pltpu.SMEM),
        ),
        out_shape=jax.ShapeDtypeStruct((n_slots,), jnp.int32),
        compiler_params=_params(("arbitrary",)),
        name="dispatch",
    )(pad_lo, pad_hi, slots)


def _experts_kernel(blk_e_ref, nused_ref, valid_ref, tok_cur_ref, tok_next_ref, h_ref, wg_ref, wu_ref,
                    wd_ref, o_ref, xbuf, xs_ref, sem, *, n_blocks, nj):
    b = pl.program_id(0)
    j = pl.program_id(1)
    nused = nused_ref[0]
    slot = b % 2

    per_step = -(-MOE_BLOCK // nj)
    total = per_step * nj

    def row_copy(tok_ref, r, s):
        tok = tok_ref[0, lax.rem(r, MOE_BLOCK)]
        return pltpu.make_async_copy(h_ref.at[pl.ds(tok, 1)], xbuf.at[s, pl.ds(r, 1)], sem.at[s])

    def wait_block(s):
        pltpu.make_async_copy(h_ref.at[pl.ds(0, MOE_BLOCK)], xbuf.at[s, pl.ds(0, MOE_BLOCK)],
                              sem.at[s]).wait()
        for r in range(MOE_BLOCK, total):
            pltpu.make_async_copy(h_ref.at[pl.ds(0, 1)], xbuf.at[s, pl.ds(r, 1)], sem.at[s]).wait()

    @pl.when(b < nused)
    def _():
        @pl.when(j == 0)
        def _():
            @pl.when(b == 0)
            def _():
                def issue(r, carry):
                    row_copy(tok_cur_ref, r, 0).start()
                    return carry

                lax.fori_loop(0, total, issue, 0)

            wait_block(slot)
            xs_ref[...] = xbuf[slot, 0:MOE_BLOCK].astype(BF16)
            o_ref[...] = jnp.zeros_like(o_ref)

        def ffn_rows(n):
            for u in range(per_step):
                row_copy(tok_next_ref, j * per_step + u, 1 - slot).start()

            xs = xs_ref[0:n]
            a = (jax.nn.silu(_dot(xs, wg_ref[...])) * _dot(xs, wu_ref[...])).astype(BF16)
            o_ref[0:n] += _dot(a, wd_ref[...])

        half_full = valid_ref[b] <= MOE_BLOCK // 2

        @pl.when(jnp.logical_not(half_full))
        def _():
            ffn_rows(MOE_BLOCK)

        @pl.when(half_full)
        def _():
            ffn_rows(MOE_BLOCK // 2)

    @pl.when((b == nused) & (j == 0))
    def _():
        wait_block(slot)

    @pl.when((b == n_blocks - 1) & (j == nj - 1) & (nused == n_blocks))
    def _():
        wait_block(1 - slot)

    @pl.when((b >= nused) & (j == 0))
    def _():
        o_ref[...] = jnp.zeros_like(o_ref)


def _experts(blk_e, nused, valid, slot_tok, h, wg, wu, wd, *, n_blocks, tf):
    D = h.shape[1]
    dff = wg.shape[2]
    nj = dff // tf
    tok3 = slot_tok.reshape(n_blocks, 1, MOE_BLOCK)
    spare = -(-MOE_BLOCK // nj) * nj - MOE_BLOCK
    xrows = MOE_BLOCK + -(-spare // SUBLANES) * SUBLANES

    def jj(b, j, nu):
        return jnp.where(b < nu[0], j, nj - 1)

    def tok_spec(shift):
        return pl.BlockSpec((None, 1, MOE_BLOCK),
                            lambda b, j, be, nu, va: (jnp.minimum(b + shift, n_blocks - 1), 0, 0),
                            memory_space=pltpu.SMEM)

    return pl.pallas_call(
        functools.partial(_experts_kernel, n_blocks=n_blocks, nj=nj),
        grid_spec=pltpu.PrefetchScalarGridSpec(
            num_scalar_prefetch=3,
            grid=(n_blocks, nj),
            in_specs=[
                tok_spec(0), tok_spec(1),
                pl.BlockSpec(memory_space=pl.ANY),
                pl.BlockSpec((None, D, tf), lambda b, j, be, nu, va: (be[b], 0, jj(b, j, nu))),
                pl.BlockSpec((None, D, tf), lambda b, j, be, nu, va: (be[b], 0, jj(b, j, nu))),
                pl.BlockSpec((None, tf, D), lambda b, j, be, nu, va: (be[b], jj(b, j, nu), 0)),
            ],
            out_specs=pl.BlockSpec((MOE_BLOCK, D), lambda b, j, be, nu, va: (b, 0)),
            scratch_shapes=[pltpu.VMEM((2, xrows, D), F32), pltpu.VMEM((MOE_BLOCK, D), BF16),
                            pltpu.SemaphoreType.DMA((2,))],
        ),
        out_shape=jax.ShapeDtypeStruct((n_blocks * MOE_BLOCK, D), F32),
        compiler_params=_params(("arbitrary", "arbitrary")),
        name="experts",
    )(blk_e, nused, valid, tok3, tok3, h, wg, wu, wd)


def _combine_kernel(slot_cur_ref, slot_next_ref, gate_ref, x_ref, g_ref, yb_ref, o_ref, buf, sem, *, tc):
    i = pl.program_id(0)
    slot = i % 2

    def gather(slot_ref, s):
        def issue(t8, carry):
            for u in range(SUBLANES):
                t = t8 * SUBLANES + u
                for k in range(TOP_K):
                    pltpu.make_async_copy(yb_ref.at[pl.ds(slot_ref[k, t], 1)],
                                          buf.at[s, k, pl.ds(t, 1)], sem.at[s]).start()
            return carry

        lax.fori_loop(0, tc // SUBLANES, issue, 0)

    @pl.when(i == 0)
    def _():
        gather(slot_cur_ref, 0)

    has_next = i + 1 < pl.num_programs(0)

    @pl.when(has_next & (slot == 0))
    def _():
        gather(slot_next_ref, 1)

    @pl.when(has_next & (slot == 1))
    def _():
        gather(slot_next_ref, 0)

    for k in range(TOP_K):
        pltpu.make_async_copy(yb_ref.at[pl.ds(0, tc)], buf.at[slot, k], sem.at[slot]).wait()

    gates = gate_ref[...]
    moe = gates[:, 0:1] * buf[slot, 0] + gates[:, 1:2] * buf[slot, 1]
    o_ref[...] = x_ref[...] + g_ref[0] * moe


def _combine(slots, gates_t, x2d, g, yb, *, seq, tc):
    T, D = x2d.shape
    kern = functools.partial(_combine_kernel, tc=tc)
    return pl.pallas_call(
        kern,
        grid=(T // tc,),
        in_specs=[
            pl.BlockSpec((TOP_K, tc), lambda i: (0, i), memory_space=pltpu.SMEM),
            pl.BlockSpec((TOP_K, tc), lambda i: (0, jnp.minimum(i + 1, T // tc - 1)),
                         memory_space=pltpu.SMEM),
            pl.BlockSpec((tc, N_EXPERTS), lambda i: (i, 0)),
            pl.BlockSpec((tc, D), lambda i: (i, 0)),
            pl.BlockSpec((1, 1, D), lambda i: (i // (seq // tc), 0, 0)),
            pl.BlockSpec(memory_space=pl.ANY),
        ],
        out_specs=pl.BlockSpec((tc, D), lambda i: (i, 0)),
        out_shape=jax.ShapeDtypeStruct((T, D), F32),
        scratch_shapes=[pltpu.VMEM((2, TOP_K, tc, D), F32), pltpu.SemaphoreType.DMA((2,))],
        compiler_params=_params(("arbitrary",)),
        name="combine",
    )(slots, slots, gates_t, x2d, g, yb)


def _mod_params(c, w_mod, b_mod):
    B, D = c.shape
    c_pad = jnp.zeros((SUBLANES, D), F32).at[:B].set(c)
    mod = _adaln(c_pad, w_mod, b_mod)[:B]
    return [m.reshape(B, 1, D) for m in jnp.split(mod, 6, axis=-1)]


def kernel(x, c, positions, l0_w_mod, l0_b_mod, l0_w_in, l0_pool_w, l0_pool_scale, l0_conv_w, l0_conv_b, l0_conv_ln_g, l0_conv_ln_b, l0_w_out, l0_ffn_w_gate, l0_ffn_w_up, l0_ffn_w_down, l1_w_mod, l1_b_mod, l1_w_qkv, l1_q_norm, l1_k_norm, l1_sinks, l1_w_o, l1_router_w, l1_exp_w_gate, l1_exp_w_up, l1_exp_w_down):
    B, L, D = x.shape
    T = B * L
    bf = lambda w: w.astype(BF16)

    sh1, sc1, g1, sh2, sc2, g2 = _mod_params(c, l0_w_mod, l0_b_mod)
    x, wg0, wu0, wd0, w_qkv, w_o = _l0_mixer(
        x, sh1, sc1, g1, bf(l0_w_in), bf(l0_pool_w), l0_pool_scale, l0_conv_w, l0_conv_b, l0_conv_ln_g,
        l0_conv_ln_b, bf(l0_w_out), (l0_ffn_w_gate, l0_ffn_w_up, l0_ffn_w_down, l1_w_qkv, l1_w_o),
        tm=_fit(TM_MIXER, L))
    n_e, _, dfe = l1_exp_w_gate.shape
    x, wg_e = _l0_ffn(x.reshape(T, D), sh2, sc2, g2, wg0, wu0, wd0, l1_exp_w_gate.reshape(n_e * D, dfe),
                      seq=L, tm=_fit(TM_FFN, L), tf=_fit(TF_FFN, l0_ffn_w_gate.shape[1]))
    x = x.reshape(B, L, D)

    sh1, sc1, g1, sh2, sc2, g2 = _mod_params(c, l1_w_mod, l1_b_mod)
    half = HEAD_DIM // 2
    inv = ROPE_THETA ** (-jnp.arange(half, dtype=F32) / half)
    inv_t = jnp.tile(inv, LANES // half).reshape(1, LANES)
    qn_t = jnp.tile(l1_q_norm * (HEAD_DIM ** -0.5 * LOG2E), LANES // HEAD_DIM).reshape(1, LANES)
    kn_t = jnp.tile(l1_k_norm, LANES // HEAD_DIM).reshape(1, LANES)
    sinks2 = l1_sinks * LOG2E
    lane = jnp.arange(LANES)
    bd = ((lane[:, None] // HEAD_DIM == lane[None, :] // HEAD_DIM) * (1.0 / HEAD_DIM)).astype(BF16)
    q, k, v, wd_e = _l1_qkv(x, sh1, sc1, positions.reshape(B, L, 1), inv_t, qn_t, kn_t, bd, w_qkv,
                            l1_exp_w_down.reshape(n_e * dfe, D), tm=_fit(TM_QKV, L))
    x, wu_e = _l1_attn(sinks2, q, k, v, x, g1, w_o, l1_exp_w_up.reshape(n_e * D, dfe),
                       tq=_fit(TQ_ATTN, L))

    x2d = x.reshape(T, D)
    h, meta, gates, cnt = _l1_router(x2d, sh2, sc2, l1_router_w.T, seq=L, tm=_fit(TM_ROUTER, L))
    counts = cnt[:, 0]
    padded = ((counts + MOE_BLOCK - 1) // MOE_BLOCK) * MOE_BLOCK
    pend = jnp.cumsum(padded)
    pstart = pend - padded
    n_blocks = (T * TOP_K + N_EXPERTS * (MOE_BLOCK - 1) + MOE_BLOCK - 1) // MOE_BLOCK
    nused = (pend[-1] // MOE_BLOCK).astype(jnp.int32).reshape(1)
    blk_start = jnp.minimum(jnp.arange(n_blocks, dtype=jnp.int32), nused[0] - 1) * MOE_BLOCK
    blk_e = jnp.minimum(jnp.sum(blk_start[:, None] >= pend[None, :], axis=1), N_EXPERTS - 1).astype(jnp.int32)
    blk_onehot = blk_e[:, None] == jnp.arange(N_EXPERTS, dtype=jnp.int32)
    blk_end = jnp.sum(jnp.where(blk_onehot, (pstart + counts).astype(jnp.int32), 0), axis=1)
    valid = jnp.clip(blk_end - blk_start, 0, MOE_BLOCK).astype(jnp.int32)
    sel = meta[:TOP_K, :, None] == jnp.arange(N_EXPERTS, dtype=jnp.int32)
    slots = meta[TOP_K:2 * TOP_K] + jnp.sum(jnp.where(sel, pstart.astype(jnp.int32), 0), axis=-1)
    n_slots = n_blocks * MOE_BLOCK
    pad_lo = jnp.concatenate([pstart + counts, pend[-1:]]).astype(jnp.int32)
    pad_hi = jnp.concatenate([pend, jnp.full((1,), n_slots, pend.dtype)]).astype(jnp.int32)
    slot_tok = _dispatch(pad_lo, pad_hi, slots, n_slots=n_slots, td=_fit(TD_DISPATCH, T))
    yb = _experts(blk_e, nused, valid, slot_tok, h, wg_e.reshape(n_e, D, dfe), wu_e.reshape(n_e, D, dfe),
                  wd_e.reshape(n_e, dfe, D), n_blocks=n_blocks, tf=_fit(TF_EXPERT, dfe))
    out = _combine(slots, gates.T, x2d, g2, yb, seq=L, tc=_fit(TC_COMBINE, L))
    return out.reshape(B, L, D)
```
